```python
import jax, jax.numpy as jnp
from jax import lax
import numpy as np

D_MODEL = 4096
BATCH = 2
SEQ = 4096
DEPTH = 1
DEC_BATCH = 128
DEC_SEQ = 4
PAST_LEN = 8192
PAGE_SIZE = 128

D_MIX = D_MODEL
D_ATT = D_MIX // 2
D_CONV = D_MIX - D_ATT
HEAD_DIM = 128
N_HEADS = D_ATT // HEAD_DIM
N_KV = N_HEADS // 4
GQA_GROUP = N_HEADS // N_KV
KV_W = N_KV * HEAD_DIM
WINDOW = 128
BLOCK = WINDOW
CONV_W = 31
D_IN = D_ATT + 2 * KV_W + D_ATT + 2 * D_CONV + D_CONV
RMS_EPS = 1e-6
LN_EPS = 1e-5
NEG_INF = -1e30

kernel_name = 'hymba_swa_sink_conformer_step'


def _rms(x, g):
    xf = x.astype(jnp.float32)
    y = xf * lax.rsqrt(jnp.mean(xf * xf, -1, keepdims=True) + RMS_EPS)
    return (y * g.astype(jnp.float32)).astype(x.dtype)


def _alibi_slopes():
    return jnp.asarray(2.0 ** (-8.0 * np.arange(1, N_HEADS + 1) / N_HEADS), jnp.float32)


def _pre(x, c, w_ada, b_ada, norm_g, w_in, q_g, k_g):
    mod = jax.nn.silu(c) @ w_ada + b_ada
    shift, scale, gate = jnp.split(mod, 3, axis=-1)
    h = _rms(x, norm_g) * (1 + scale[:, None]) + shift[:, None]
    z = h @ w_in
    offs = [int(o) for o in np.cumsum([D_ATT, KV_W, KV_W, D_ATT, D_CONV, D_CONV])]
    q, k, v, ga, cu_a, cu_b, cg = jnp.split(z, offs, axis=-1)
    n, t = x.shape[:2]
    q = _rms(q.reshape(n, t, N_KV, GQA_GROUP, HEAD_DIM), q_g)
    k = _rms(k.reshape(n, t, N_KV, HEAD_DIM), k_g)
    v = v.reshape(n, t, N_KV, HEAD_DIM)
    u = cu_a * jax.nn.sigmoid(cu_b)
    return q, k, v, ga, u, cg, gate


def _sink_attention(q, k, v, dist, sinks):
    s = jnp.einsum('...qngd,...knd->...ngqk', q, k,
                   preferred_element_type=jnp.float32) * (HEAD_DIM ** -0.5)
    s = s - _alibi_slopes().reshape(N_KV, GQA_GROUP, 1, 1) * dist.astype(jnp.float32)
    s = jnp.where((dist >= 0) & (dist < WINDOW), s, NEG_INF)
    sink = sinks.astype(jnp.float32).reshape(N_KV, GQA_GROUP, 1, 1)
    m = jnp.maximum(jnp.max(s, -1, keepdims=True), sink)
    p = jnp.exp(s - m)
    p = p / (jnp.sum(p, -1, keepdims=True) + jnp.exp(sink - m))
    return jnp.einsum('...ngqk,...knd->...qngd', p.astype(v.dtype), v)


def _prompt_attention(q, k, v, sinks):
    n, t = k.shape[:2]
    nb = t // BLOCK
    qb = q.reshape(n, nb, BLOCK, N_KV, GQA_GROUP, HEAD_DIM)

    def band(a):
        ap = jnp.pad(a, ((0, 0), (BLOCK, 0), (0, 0), (0, 0)))
        ap = ap.reshape(n, nb + 1, BLOCK, N_KV, HEAD_DIM)
        return jnp.concatenate([ap[:, :-1], ap[:, 1:]], axis=2)

    qi = jnp.arange(BLOCK)[:, None]
    kj = jnp.arange(2 * BLOCK)[None, :]
    blk = jnp.arange(nb)[:, None, None]
    dist = jnp.where((blk > 0) | (kj >= BLOCK), BLOCK + qi - kj, WINDOW)
    o = _sink_attention(qb, band(k), band(v), dist[:, None, None], sinks)
    return o.reshape(n, t, D_ATT)


def _sample_attention(q, k_new, v_new, k_buf, v_buf, sinks):
    n, s_len = q.shape[:2]
    k_all = jnp.concatenate([k_buf.astype(k_new.dtype), k_new], axis=1)
    v_all = jnp.concatenate([v_buf.astype(v_new.dtype), v_new], axis=1)
    dist = jnp.arange(s_len)[:, None] + WINDOW - jnp.arange(WINDOW + s_len)[None, :]
    o = _sink_attention(q, k_all, v_all, dist, sinks)
    return o.reshape(n, s_len, D_ATT), k_all[:, -WINDOW:], v_all[:, -WINDOW:]


def _dwconv(u_ext, w, b):
    y = lax.conv_general_dilated(u_ext, w.astype(u_ext.dtype)[:, None, :], (1,), 'VALID',
                                 dimension_numbers=('NWC', 'WIO', 'NWC'),
                                 feature_group_count=D_CONV)
    return y + b


def _conv_tail(y, ln_g, ln_b, w_pw2, cg):
    yf = y.astype(jnp.float32)
    mu = jnp.mean(yf, -1, keepdims=True)
    var = jnp.mean(jnp.square(yf - mu), -1, keepdims=True)
    yn = ((yf - mu) * lax.rsqrt(var + LN_EPS) * ln_g.astype(jnp.float32)
          + ln_b.astype(jnp.float32)).astype(y.dtype)
    return (jax.nn.silu(yn) @ w_pw2) * jax.nn.silu(cg)


def _post(x, att, ga, conv, w_out, gate):
    o = jnp.concatenate([att * jax.nn.silu(ga), conv], axis=-1) @ w_out
    return x + gate[:, None] * o


def setup_inputs(seed: int = 0) -> dict:
    key = jax.random.key(seed)
    ks = jax.random.split(key, 22)
    f32 = jnp.float32

    def nrm(k, shape, s):
        return jax.random.normal(k, shape, f32) * s

    return {
        'x_prompt': nrm(ks[0], (BATCH, SEQ, D_MODEL), 1.0),
        'x_sample': nrm(ks[1], (DEC_BATCH, DEC_SEQ, D_MODEL), 1.0),
        'c_prompt': nrm(ks[2], (BATCH, D_MODEL), 1.0),
        'c_sample': nrm(ks[3], (DEC_BATCH, D_MODEL), 1.0),
        'cache_k_win': nrm(ks[4], (DEPTH, DEC_BATCH, WINDOW, N_KV, HEAD_DIM), 1.0),
        'cache_v_win': nrm(ks[5], (DEPTH, DEC_BATCH, WINDOW, N_KV, HEAD_DIM), 1.0),
        'state_conv': nrm(ks[6], (DEPTH, DEC_BATCH, CONV_W - 1, D_CONV), 0.5),
        'w_ada': nrm(ks[7], (DEPTH, D_MODEL, 3 * D_MODEL), 0.5 * D_MODEL ** -0.5),
        'b_ada': nrm(ks[8], (DEPTH, 3 * D_MODEL), 0.01),
        'norm_g': 1.0 + nrm(ks[9], (DEPTH, D_MODEL), 0.02),
        'w_in': nrm(ks[10], (DEPTH, D_MODEL, D_IN), D_MODEL ** -0.5),
        'q_norm_g': 1.0 + nrm(ks[11], (DEPTH, HEAD_DIM), 0.02),
        'k_norm_g': 1.0 + nrm(ks[12], (DEPTH, HEAD_DIM), 0.02),
        'sinks': nrm(ks[13], (DEPTH, N_HEADS), 0.5),
        'conv_w': nrm(ks[14], (DEPTH, CONV_W, D_CONV), CONV_W ** -0.5),
        'conv_b': nrm(ks[15], (DEPTH, D_CONV), 0.01),
        'ln_g': 1.0 + nrm(ks[16], (DEPTH, D_CONV), 0.02),
        'ln_b': nrm(ks[17], (DEPTH, D_CONV), 0.01),
        'w_pw2': nrm(ks[18], (DEPTH, D_CONV, D_CONV), D_CONV ** -0.5),
        'w_out': nrm(ks[19], (DEPTH, D_MIX, D_MODEL), D_MIX ** -0.5),
    }


def reference(x_prompt, x_sample, c_prompt, c_sample, cache_k_win, cache_v_win, state_conv,
              w_ada, b_ada, norm_g, w_in, q_norm_g, k_norm_g, sinks, conv_w, conv_b,
              ln_g, ln_b, w_pw2, w_out):
    hp, hs = x_prompt, x_sample
    kp_l, vp_l, cp_l, ks_l, vs_l, cs_l = [], [], [], [], [], []
    for l in range(DEPTH):
        q, k, v, ga, u, cg, g = _pre(hp, c_prompt, w_ada[l], b_ada[l], norm_g[l], w_in[l],
                                     q_norm_g[l], k_norm_g[l])
        att = _prompt_attention(q, k, v, sinks[l])
        u_ext = jnp.pad(u, ((0, 0), (CONV_W - 1, 0), (0, 0)))
        conv = _conv_tail(_dwconv(u_ext, conv_w[l], conv_b[l]), ln_g[l], ln_b[l], w_pw2[l], cg)
        kp_l.append(k[:, -WINDOW:])
        vp_l.append(v[:, -WINDOW:])
        cp_l.append(u_ext[:, -(CONV_W - 1):])
        hp = _post(hp, att, ga, conv, w_out[l], g)

        q, k, v, ga, u, cg, g = _pre(hs, c_sample, w_ada[l], b_ada[l], norm_g[l], w_in[l],
                                     q_norm_g[l], k_norm_g[l])
        att, k_buf, v_buf = _sample_attention(q, k, v, cache_k_win[l], cache_v_win[l], sinks[l])
        u_ext = jnp.concatenate([state_conv[l].astype(u.dtype), u], axis=1)
        conv = _conv_tail(_dwconv(u_ext, conv_w[l], conv_b[l]), ln_g[l], ln_b[l], w_pw2[l], cg)
        ks_l.append(k_buf)
        vs_l.append(v_buf)
        cs_l.append(u_ext[:, -(CONV_W - 1):])
        hs = _post(hs, att, ga, conv, w_out[l], g)

    return (hp, hs, jnp.stack(kp_l), jnp.stack(vp_l), jnp.stack(cp_l),
            jnp.stack(ks_l), jnp.stack(vs_l), jnp.stack(cs_l))
```

```python
import functools

import numpy as np
import jax
import jax.numpy as jnp
from jax import lax
from jax.experimental import pallas as pl
from jax.experimental.pallas import tpu as pltpu

F32 = jnp.float32
BF16 = jnp.bfloat16

D_MODEL = 4096
D_ATT = 2048
D_CONV = 2048
HEAD_DIM = 128
N_HEADS = 16
N_KV = 4
GQA = 4
KV_W = N_KV * HEAD_DIM
WINDOW = 128
CONV_W = 31
HALO = 32
RMS_EPS = 1e-6
LN_EPS = 1e-5
NEG_INF = -1e30
ATT_SCALE = HEAD_DIM ** -0.5
OFF_Q, OFF_K, OFF_V, OFF_GA, OFF_CA, OFF_CB, OFF_CG = 0, 2048, 2560, 3072, 5120, 7168, 9216
SLOPES = [float(np.float32(2.0 ** (-8.0 * (h + 1) / N_HEADS))) for h in range(N_HEADS)]

VMEM_LIMIT = 56 * 1024 * 1024


def _params(sem):
    return pltpu.CompilerParams(dimension_semantics=sem, vmem_limit_bytes=VMEM_LIMIT)


def _silu(x):
    return x * jax.nn.sigmoid(x)


def _mod_kernel(c_ref, w_ref, b_ref, o_ref):
    c = c_ref[...]
    s = _silu(c).astype(BF16)
    o_ref[...] = jnp.dot(s, w_ref[...].astype(BF16), preferred_element_type=F32) + b_ref[...]


def _modulation(c_all, w_ada, b_ada, tn=512):
    r, d = c_all.shape
    n = w_ada.shape[1]
    return pl.pallas_call(
        _mod_kernel,
        grid=(n // tn,),
        in_specs=[
            pl.BlockSpec((r, d), lambda j: (0, 0)),
            pl.BlockSpec((d, tn), lambda j: (0, j)),
            pl.BlockSpec((1, tn), lambda j: (0, j)),
        ],
        out_specs=pl.BlockSpec((r, tn), lambda j: (0, j)),
        out_shape=jax.ShapeDtypeStruct((r, n), F32),
        compiler_params=_params(("arbitrary",)),
        name="modulation",
    )(c_all, w_ada, b_ada.reshape(1, n))


def _prenorm_kernel(x_ref, g_ref, scale_ref, shift_ref, o_ref):
    x = x_ref[...]
    y = x * lax.rsqrt(jnp.mean(x * x, axis=-1, keepdims=True) + RMS_EPS)
    y = y * g_ref[...]
    o_ref[...] = (y * (1.0 + scale_ref[...]) + shift_ref[...]).astype(o_ref.dtype)


def _prenorm(x2d, g, scale, shift, rows_per_mod, tm):
    m, d = x2d.shape
    if rows_per_mod == 1:
        mod_spec = pl.BlockSpec((tm, d), lambda i: (i, 0))
    else:
        assert rows_per_mod % tm == 0
        scale = scale.reshape(-1, 1, d)
        shift = shift.reshape(-1, 1, d)
        mod_spec = pl.BlockSpec((None, 1, d), lambda i: (i * tm // rows_per_mod, 0, 0))
    return pl.pallas_call(
        _prenorm_kernel,
        grid=(m // tm,),
        in_specs=[
            pl.BlockSpec((tm, d), lambda i: (i, 0)),
            pl.BlockSpec((1, d), lambda i: (0, 0)),
            mod_spec,
            mod_spec,
        ],
        out_specs=pl.BlockSpec((tm, d), lambda i: (i, 0)),
        out_shape=jax.ShapeDtypeStruct((m, d), BF16),
        compiler_params=_params(("parallel",)),
        name="prenorm",
    )(x2d, g.reshape(1, d), scale, shift)


def _headnorm(a, g):
    return a * lax.rsqrt(jnp.mean(a * a, axis=-1, keepdims=True) + RMS_EPS) * g


def _proj_qnorm_kernel(h_ref, w_ref, g_ref, o_ref):
    acc = jnp.dot(h_ref[...], w_ref[...], preferred_element_type=F32)
    g = g_ref[...]
    for c in range(acc.shape[1] // HEAD_DIM):
        sl = slice(c * HEAD_DIM, (c + 1) * HEAD_DIM)
        o_ref[:, sl] = _headnorm(acc[:, sl], g).astype(o_ref.dtype)


def _proj_kv_kernel(h_ref, wk_ref, wv_ref, g_ref, k_ref, v_ref):
    h = h_ref[...]
    acc = jnp.dot(h, wk_ref[...], preferred_element_type=F32)
    g = g_ref[...]
    for c in range(N_KV):
        sl = slice(c * HEAD_DIM, (c + 1) * HEAD_DIM)
        k_ref[:, sl] = _headnorm(acc[:, sl], g)
    v_ref[...] = jnp.dot(h, wv_ref[...], preferred_element_type=F32)


def _proj_silu_kernel(h_ref, w_ref, o_ref):
    acc = jnp.dot(h_ref[...], w_ref[...], preferred_element_type=F32)
    o_ref[...] = _silu(acc).astype(o_ref.dtype)


def _proj_glu_kernel(h_ref, wa_ref, wb_ref, o_ref):
    h = h_ref[...]
    a = jnp.dot(h, wa_ref[...], preferred_element_type=F32)
    b = jnp.dot(h, wb_ref[...], preferred_element_type=F32)
    o_ref[...] = a * jax.nn.sigmoid(b)


def _proj_call(kernel, h, w, col_offs, width, extra, out_dtypes, tm, tn, name):
    m, d = h.shape
    w_specs = [
        pl.BlockSpec((d, tn), functools.partial(lambda i, j, o: (0, o + j), o=off // tn))
        for off in col_offs
    ]
    extra_specs = [pl.BlockSpec(e.shape, lambda i, j: (0, 0)) for e in extra]
    out_spec = pl.BlockSpec((tm, tn), lambda i, j: (i, j))
    outs = pl.pallas_call(
        kernel,
        grid=(m // tm, width // tn),
        in_specs=[pl.BlockSpec((tm, d), lambda i, j: (i, 0))] + w_specs + extra_specs,
        out_specs=[out_spec] * len(out_dtypes),
        out_shape=[jax.ShapeDtypeStruct((m, width), dt) for dt in out_dtypes],
        compiler_params=_params(("parallel", "arbitrary")),
        name=name,
    )(h, *([w] * len(col_offs)), *extra)
    return outs


def _prompt_attn_kernel(sinks_ref, q_ref, kp_ref, kc_ref, vp_ref, vc_ref, ga_ref, o_ref, *, bq):
    first = (pl.program_id(1) == 0).astype(jnp.int32)
    nsub = bq // WINDOW
    k_all = jnp.concatenate([kp_ref[...], kc_ref[...]], axis=0).astype(BF16)
    v_all = jnp.concatenate([vp_ref[...], vc_ref[...]], axis=0).astype(BF16)
    qi = lax.broadcasted_iota(jnp.int32, (WINDOW, 2 * WINDOW), 0)
    kj = lax.broadcasted_iota(jnp.int32, (WINDOW, 2 * WINDOW), 1)
    dist = WINDOW + qi - kj
    in_band = (dist >= 0) & (dist < WINDOW)
    valid_first = in_band & (kj >= WINDOW * first)
    distf = dist.astype(F32)
    for s in range(nsub):
        valid = valid_first if s == 0 else in_band
        rows = slice(s * WINDOW, (s + 1) * WINDOW)
        for kv in range(N_KV):
            cols = slice(kv * HEAD_DIM, (kv + 1) * HEAD_DIM)
            kb = k_all[s * WINDOW:(s + 2) * WINDOW, cols]
            vb = v_all[s * WINDOW:(s + 2) * WINDOW, cols]
            for g in range(GQA):
                h = kv * GQA + g
                hc = slice(h * HEAD_DIM, (h + 1) * HEAD_DIM)
                q = q_ref[rows, hc]
                sc = lax.dot_general(q, kb, (((1,), (1,)), ((), ())),
                                     preferred_element_type=F32) * ATT_SCALE
                sc = sc - SLOPES[h] * distf
                sc = jnp.where(valid, sc, NEG_INF)
                sink = sinks_ref[h]
                m = jnp.maximum(jnp.max(sc, axis=-1, keepdims=True), sink)
                p = jnp.exp(sc - m)
                denom = jnp.sum(p, axis=-1, keepdims=True) + jnp.exp(sink - m)
                o = jnp.dot(p.astype(BF16), vb, preferred_element_type=F32) / denom
                o_ref[rows, hc] = (o * ga_ref[rows, hc].astype(F32)).astype(o_ref.dtype)


def _prompt_attention(q, k, v, ga, sinks, n_batch, seq, bq=256):
    m = q.shape[0]
    nb = seq // bq
    ratio = bq // WINDOW
    cur = lambda n, i, s: (n * nb + i, 0)
    prev = lambda n, i, s: (jnp.maximum((n * nb + i) * ratio - 1, 0), 0)
    return pl.pallas_call(
        functools.partial(_prompt_attn_kernel, bq=bq),
        grid_spec=pltpu.PrefetchScalarGridSpec(
            num_scalar_prefetch=1,
            grid=(n_batch, nb),
            in_specs=[
                pl.BlockSpec((bq, D_ATT), cur),
                pl.BlockSpec((WINDOW, KV_W), prev),
                pl.BlockSpec((bq, KV_W), cur),
                pl.BlockSpec((WINDOW, KV_W), prev),
                pl.BlockSpec((bq, KV_W), cur),
                pl.BlockSpec((bq, D_ATT), cur),
            ],
            out_specs=pl.BlockSpec((bq, D_ATT), cur),
        ),
        out_shape=jax.ShapeDtypeStruct((m, D_ATT), BF16),
        compiler_params=_params(("parallel", "arbitrary")),
        name="prompt_attention",
    )(sinks, q, k, k, v, v, ga)


def _sample_attn_kernel(slope_ref, sink_ref, q_ref, kc_ref, kn_ref, vc_ref, vn_ref, ga_ref, o_ref,
                        *, bn, s_len, pad_len):
    nk = WINDOW + pad_len
    r = GQA * s_len
    ri = lax.broadcasted_iota(jnp.int32, (r, nk), 0)
    kj = lax.broadcasted_iota(jnp.int32, (r, nk), 1)
    dist = lax.rem(ri, s_len) + WINDOW - kj
    valid = (dist >= 0) & (dist < WINDOW)
    distf = dist.astype(F32)

    def body(b, carry):
        k_all = jnp.concatenate([kc_ref[b], kn_ref[b]], axis=0).astype(BF16)
        v_all = jnp.concatenate([vc_ref[b], vn_ref[b]], axis=0).astype(BF16)
        for kv in range(N_KV):
            cols = slice(kv * HEAD_DIM, (kv + 1) * HEAD_DIM)
            qs = q_ref[b, kv].astype(BF16)
            sc = lax.dot_general(qs, k_all[:, cols], (((1,), (1,)), ((), ())),
                                 preferred_element_type=F32) * ATT_SCALE
            sc = sc - slope_ref[kv] * distf
            sc = jnp.where(valid, sc, NEG_INF)
            sink = sink_ref[kv]
            m = jnp.maximum(jnp.max(sc, axis=-1, keepdims=True), sink)
            p = jnp.exp(sc - m)
            denom = jnp.sum(p, axis=-1, keepdims=True) + jnp.exp(sink - m)
            o = jnp.dot(p.astype(BF16), v_all[:, cols], preferred_element_type=F32) / denom
            o_ref[b, kv] = o * ga_ref[b, kv]
        return carry

    lax.fori_loop(0, bn, body, 0)


def _to_head_rows(a, n, s_len):
    a = a.reshape(n, s_len, N_KV, GQA, HEAD_DIM)
    return jnp.transpose(a, (0, 2, 3, 1, 4)).reshape(n, N_KV, GQA * s_len, HEAD_DIM)


def _from_head_rows(a, n, s_len):
    a = a.reshape(n, N_KV, GQA, s_len, HEAD_DIM)
    return jnp.transpose(a, (0, 3, 1, 2, 4)).reshape(n * s_len, D_ATT)


def _sample_attention(q, kc, kn3, vc, vn3, ga, sinks, n, s_len, bn=8):
    r = GQA * s_len
    pad_len = -(-s_len // 8) * 8
    slopes = jnp.asarray(SLOPES, F32).reshape(N_KV, GQA, 1, 1)
    slope_rows = jnp.broadcast_to(slopes, (N_KV, GQA, s_len, 1)).reshape(N_KV, r, 1)
    sink_rows = jnp.broadcast_to(sinks.astype(F32).reshape(N_KV, GQA, 1, 1),
                                 (N_KV, GQA, s_len, 1)).reshape(N_KV, r, 1)
    padk = lambda a: jnp.pad(a, ((0, 0), (0, pad_len - s_len), (0, 0)))
    heads = pl.BlockSpec((bn, N_KV, r, HEAD_DIM), lambda i: (i, 0, 0, 0))
    win = pl.BlockSpec((bn, WINDOW, KV_W), lambda i: (i, 0, 0))
    new = pl.BlockSpec((bn, pad_len, KV_W), lambda i: (i, 0, 0))
    small = pl.BlockSpec((N_KV, r, 1), lambda i: (0, 0, 0))
    o = pl.pallas_call(
        functools.partial(_sample_attn_kernel, bn=bn, s_len=s_len, pad_len=pad_len),
        grid=(n // bn,),
        in_specs=[small, small, heads, win, new, win, new, heads],
        out_specs=heads,
        out_shape=jax.ShapeDtypeStruct((n, N_KV, r, HEAD_DIM), F32),
        compiler_params=_params(("parallel",)),
        name="sample_attention",
    )(slope_rows, sink_rows, _to_head_rows(q, n, s_len), kc, padk(kn3), vc, padk(vn3),
      _to_head_rows(ga, n, s_len))
    return _from_head_rows(o, n, s_len)


def _ln_pw_gate(y, lng_ref, lnb_ref, wpw_ref, cg_ref, o_ref):
    mu = jnp.mean(y, axis=-1, keepdims=True)
    yc = y - mu
    var = jnp.mean(yc * yc, axis=-1, keepdims=True)
    yn = yc * lax.rsqrt(var + LN_EPS) * lng_ref[...] + lnb_ref[...]
    a = _silu(yn).astype(BF16)
    o = jnp.dot(a, wpw_ref[...], preferred_element_type=F32)
    o_ref[...] = (o * cg_ref[...].astype(F32)).astype(o_ref.dtype)


def _prompt_conv_kernel(uh_ref, uc_ref, cw_ref, cb_ref, lng_ref, lnb_ref, wpw_ref, cg_ref, o_ref,
                        ext, ybuf, *, tt, rc, lc):
    @pl.when(pl.program_id(1) == 0)
    def _():
        ext[pl.ds(0, HALO), :] = jnp.zeros((HALO, D_CONV), F32)

    @pl.when(pl.program_id(1) != 0)
    def _():
        ext[pl.ds(0, HALO), :] = uh_ref[...]

    ext[pl.ds(HALO, tt), :] = uc_ref[...]
    base = HALO - (CONV_W - 1)
    for c0 in range(0, D_CONV, lc):
        for r0 in range(0, tt, rc):
            acc = jnp.broadcast_to(cb_ref[:, pl.ds(c0, lc)], (rc, lc))
            for k in range(CONV_W):
                acc = acc + cw_ref[pl.ds(k, 1), pl.ds(c0, lc)] * ext[pl.ds(r0 + base + k, rc), pl.ds(c0, lc)]
            ybuf[pl.ds(r0, rc), pl.ds(c0, lc)] = acc
    _ln_pw_gate(ybuf[...], lng_ref, lnb_ref, wpw_ref, cg_ref, o_ref)


def _prompt_conv(u, cg, conv_w, conv_b, ln_g, ln_b, w_pw, n_batch, seq, tt=256):
    m = u.shape[0]
    nb = seq // tt
    ratio = tt // HALO
    cur = lambda n, i: (n * nb + i, 0)
    prev = lambda n, i: (jnp.maximum((n * nb + i) * ratio - 1, 0), 0)
    const = lambda n, i: (0, 0)
    return pl.pallas_call(
        functools.partial(_prompt_conv_kernel, tt=tt, rc=32, lc=256),
        grid=(n_batch, nb),
        in_specs=[
            pl.BlockSpec((HALO, D_CONV), prev),
            pl.BlockSpec((tt, D_CONV), cur),
            pl.BlockSpec((CONV_W, D_CONV), const),
            pl.BlockSpec((1, D_CONV), const),
            pl.BlockSpec((1, D_CONV), const),
            pl.BlockSpec((1, D_CONV), const),
            pl.BlockSpec((D_CONV, D_CONV), const),
            pl.BlockSpec((tt, D_CONV), cur),
        ],
        out_specs=pl.BlockSpec((tt, D_CONV), cur),
        out_shape=jax.ShapeDtypeStruct((m, D_CONV), BF16),
        scratch_shapes=[pltpu.VMEM((HALO + tt, D_CONV), F32), pltpu.VMEM((tt, D_CONV), F32)],
        compiler_params=_params(("parallel", "arbitrary")),
        name="prompt_conv",
    )(u, u, conv_w, conv_b.reshape(1, -1), ln_g.reshape(1, -1), ln_b.reshape(1, -1), w_pw, cg)


def _sample_dwconv_kernel(st_ref, u_ref, ws_ref, wu_ref, cb_ref, y_ref, *, s_len):
    st = st_ref[...]
    u = u_ref[...]
    for t in range(s_len):
        y = jnp.sum(st * ws_ref[t][None], axis=1) + jnp.sum(u * wu_ref[t][None], axis=1)
        y_ref[t] = y + cb_ref[...]


def _sample_dwconv(state, u3, conv_w, conv_b, bn=8):
    n, s_len, c = u3.shape
    hist = CONV_W - 1
    ws = jnp.stack([jnp.concatenate([jnp.zeros((t, c), F32), conv_w[:hist - t]], axis=0)
                    for t in range(s_len)])
    wu = jnp.stack([jnp.concatenate([conv_w[hist - t:], jnp.zeros((s_len - 1 - t, c), F32)], axis=0)
                    for t in range(s_len)])
    y = pl.pallas_call(
        functools.partial(_sample_dwconv_kernel, s_len=s_len),
        grid=(n // bn,),
        in_specs=[
            pl.BlockSpec((bn, hist, c), lambda i: (i, 0, 0)),
            pl.BlockSpec((bn, s_len, c), lambda i: (i, 0, 0)),
            pl.BlockSpec((s_len, hist, c), lambda i: (0, 0, 0)),
            pl.BlockSpec((s_len, s_len, c), lambda i: (0, 0, 0)),
            pl.BlockSpec((1, c), lambda i: (0, 0)),
        ],
        out_specs=pl.BlockSpec((s_len, bn, c), lambda i: (0, i, 0)),
        out_shape=jax.ShapeDtypeStruct((s_len, n, c), F32),
        compiler_params=_params(("parallel",)),
        name="sample_dwconv",
    )(state, u3, ws, wu, conv_b.reshape(1, c))
    return y


def _ln_pw_kernel(y_ref, lng_ref, lnb_ref, wpw_ref, cg_ref, o_ref):
    _ln_pw_gate(y_ref[...], lng_ref, lnb_ref, wpw_ref, cg_ref, o_ref)


def _ln_pw(y, cg, ln_g, ln_b, w_pw, tm):
    m, c = y.shape
    const = lambda i: (0, 0)
    return pl.pallas_call(
        _ln_pw_kernel,
        grid=(m // tm,),
        in_specs=[
            pl.BlockSpec((tm, c), lambda i: (i, 0)),
            pl.BlockSpec((1, c), const),
            pl.BlockSpec((1, c), const),
            pl.BlockSpec((c, c), const),
            pl.BlockSpec((tm, c), lambda i: (i, 0)),
        ],
        out_specs=pl.BlockSpec((tm, c), lambda i: (i, 0)),
        out_shape=jax.ShapeDtypeStruct((m, c), BF16),
        compiler_params=_params(("parallel",)),
        name="sample_ln_pw",
    )(y, ln_g.reshape(1, -1), ln_b.reshape(1, -1), w_pw, cg)


def _outproj_kernel(a_ref, c_ref, wa_ref, wc_ref, x_ref, gate_ref, o_ref):
    o = jnp.dot(a_ref[...], wa_ref[...], preferred_element_type=F32)
    o = o + jnp.dot(c_ref[...], wc_ref[...], preferred_element_type=F32)
    o_ref[...] = x_ref[...] + gate_ref[...] * o


def _outproj(att, conv, w_out, x2d, gate, rows_per_gate, tm, tn):
    m, d = x2d.shape
    if rows_per_gate == 1:
        gate_spec = pl.BlockSpec((tm, tn), lambda i, j: (i, j))
    else:
        assert rows_per_gate % tm == 0
        gate = gate.reshape(-1, 1, d)
        gate_spec = pl.BlockSpec((None, 1, tn), lambda i, j: (i * tm // rows_per_gate, 0, j))
    return pl.pallas_call(
        _outproj_kernel,
        grid=(m // tm, d // tn),
        in_specs=[
            pl.BlockSpec((tm, D_ATT), lambda i, j: (i, 0)),
            pl.BlockSpec((tm, D_CONV), lambda i, j: (i, 0)),
            pl.BlockSpec((D_ATT, tn), lambda i, j: (0, j)),
            pl.BlockSpec((D_CONV, tn), lambda i, j: (1, j)),
            pl.BlockSpec((tm, tn), lambda i, j: (i, j)),
            gate_spec,
        ],
        out_specs=pl.BlockSpec((tm, tn), lambda i, j: (i, j)),
        out_shape=jax.ShapeDtypeStruct((m, d), F32),
        compiler_params=_params(("parallel", "arbitrary")),
        name="outproj",
    )(att, conv, w_out, w_out, x2d, gate)


def _in_projection(h, w_in, q_g, k_g, tm, small_dtype):
    qg = q_g.reshape(1, HEAD_DIM)
    kg = k_g.reshape(1, HEAD_DIM)
    tn = 512
    (q,) = _proj_call(_proj_qnorm_kernel, h, w_in, [OFF_Q], D_ATT, [qg], [small_dtype], tm, tn, "proj_q")
    k, v = _proj_call(_proj_kv_kernel, h, w_in, [OFF_K, OFF_V], KV_W, [kg], [F32, F32], tm, tn, "proj_kv")
    (ga,) = _proj_call(_proj_silu_kernel, h, w_in, [OFF_GA], D_ATT, [], [small_dtype], tm, tn, "proj_ga")
    (u,) = _proj_call(_proj_glu_kernel, h, w_in, [OFF_CA, OFF_CB], D_CONV, [], [F32], tm, tn, "proj_glu")
    (cg,) = _proj_call(_proj_silu_kernel, h, w_in, [OFF_CG], D_CONV, [], [BF16], tm, tn, "proj_cg")
    return q, k, v, ga, u, cg


def kernel(x_prompt, x_sample, c_prompt, c_sample, cache_k_win, cache_v_win, state_conv, w_ada, b_ada,
           norm_g, w_in, q_norm_g, k_norm_g, sinks, conv_w, conv_b, ln_g, ln_b, w_pw2, w_out):
    depth = w_in.shape[0]
    nb, seq, d = x_prompt.shape
    ns, s_len, _ = x_sample.shape
    mp, ms = nb * seq, ns * s_len
    hist = CONV_W - 1

    xp = x_prompt.reshape(mp, d)
    xs = x_sample.reshape(ms, d)
    pad = (-(ns + nb)) % 16
    c_all = jnp.concatenate([c_sample, c_prompt, jnp.zeros((pad, d), F32)], axis=0)

    kp_l, vp_l, cp_l, ks_l, vs_l, cs_l = [], [], [], [], [], []
    for l in range(depth):
        w_in_b = w_in[l].astype(BF16)
        w_pw_b = w_pw2[l].astype(BF16)
        w_out_b = w_out[l].astype(BF16)

        mod = _modulation(c_all, w_ada[l], b_ada[l])
        shift, scale, gate = mod[:, :d], mod[:, d:2 * d], mod[:, 2 * d:]

        hp = _prenorm(xp, norm_g[l], scale[ns:ns + nb], shift[ns:ns + nb], seq, 512)
        q, k, v, ga, u, cg = _in_projection(hp, w_in_b, q_norm_g[l], k_norm_g[l], 1024, BF16)
        att = _prompt_attention(q, k, v, ga, sinks[l], nb, seq)
        conv = _prompt_conv(u, cg, conv_w[l], conv_b[l], ln_g[l], ln_b[l], w_pw_b, nb, seq)
        kp_l.append(k.reshape(nb, seq, N_KV, HEAD_DIM)[:, -WINDOW:])
        vp_l.append(v.reshape(nb, seq, N_KV, HEAD_DIM)[:, -WINDOW:])
        cp_l.append(u.reshape(nb, seq, D_CONV)[:, -hist:])
        xp = _outproj(att, conv, w_out_b, xp, gate[ns:ns + nb], seq, 1024, 512)

        rep = lambda a: jnp.repeat(a[:ns], s_len, axis=0)
        hs = _prenorm(xs, norm_g[l], rep(scale), rep(shift), 1, 256)
        q, k, v, ga, u, cg = _in_projection(hs, w_in_b, q_norm_g[l], k_norm_g[l], ms, F32)
        kc = cache_k_win[l].reshape(ns, WINDOW, KV_W)
        vc = cache_v_win[l].reshape(ns, WINDOW, KV_W)
        k3 = k.reshape(ns, s_len, KV_W)
        v3 = v.reshape(ns, s_len, KV_W)
        att = _sample_attention(q, kc, k3, vc, v3, ga, sinks[l], ns, s_len).astype(BF16)
        u3 = u.reshape(ns, s_len, D_CONV)
        y = _sample_dwconv(state_conv[l], u3, conv_w[l], conv_b[l])
        y = jnp.transpose(y, (1, 0, 2)).reshape(ms, D_CONV)
        conv = _ln_pw(y, cg, ln_g[l], ln_b[l], w_pw_b, 256)
        ks_l.append(jnp.concatenate([kc[:, s_len:], k3], axis=1).reshape(ns, WINDOW, N_KV, HEAD_DIM))
        vs_l.append(jnp.concatenate([vc[:, s_len:], v3], axis=1).reshape(ns, WINDOW, N_KV, HEAD_DIM))
        cs_l.append(jnp.concatenate([state_conv[l][:, s_len:], u3], axis=1))
        xs = _outproj(att, conv, w_out_b, xs, rep(gate), 1, ms, 512)

    return (xp.reshape(nb, seq, d), xs.reshape(ns, s_len, d),
            jnp.stack(kp_l), jnp.stack(vp_l), jnp.stack(cp_l),
            jnp.stack(ks_l), jnp.stack(vs_l), jnp.stack(cs_l))
```

```python
import functools

import numpy as np
import jax
import jax.numpy as jnp
from jax import lax
from jax.experimental import pallas as pl
from jax.experimental.pallas import tpu as pltpu

F32 = jnp.float32
BF16 = jnp.bfloat16

D_MODEL = 4096
D_ATT = 2048
D_CONV = 2048
HEAD_DIM = 128
N_HEADS = 16
N_KV = 4
GQA = 4
KV_W = N_KV * HEAD_DIM
WINDOW = 128
CONV_W = 31
HIST = CONV_W - 1
LANES = 128
SUBLANES = 8
HALO = 32
N_SLAB = D_CONV // LANES
RMS_EPS = 1e-6
LN_EPS = 1e-5
NEG_INF = -1e30
ATT_SCALE = HEAD_DIM ** -0.5
OFF_Q, OFF_K, OFF_V, OFF_GA, OFF_CA, OFF_CB, OFF_CG = 0, 2048, 2560, 3072, 5120, 7168, 9216
SLOPES = [float(np.float32(2.0 ** (-8.0 * (h + 1) / N_HEADS))) for h in range(N_HEADS)]

VMEM_LIMIT = 58 * 1024 * 1024
TM = 1024


def _params(n_axes):
    return pltpu.CompilerParams(dimension_semantics=("arbitrary",) * n_axes,
                                vmem_limit_bytes=VMEM_LIMIT)


def _silu(x):
    return x * jax.nn.sigmoid(x)


def _mod_kernel(c_ref, w_ref, b_ref, o_ref, act):
    @pl.when(pl.program_id(0) == 0)
    def _():
        act[...] = _silu(c_ref[...]).astype(BF16)

    o_ref[...] = jnp.dot(act[...], w_ref[...].astype(BF16), preferred_element_type=F32) + b_ref[...]


def _modulation(c_all, w_ada, b_ada, l, tn=512):
    r, d = c_all.shape
    depth, _, n = w_ada.shape
    return pl.pallas_call(
        _mod_kernel,
        grid=(n // tn,),
        in_specs=[
            pl.BlockSpec((r, d), lambda j: (0, 0)),
            pl.BlockSpec((d, tn), lambda j: (l, j)),
            pl.BlockSpec((None, 1, tn), lambda j: (l, 0, j)),
        ],
        out_specs=pl.BlockSpec((r, tn), lambda j: (0, j)),
        out_shape=jax.ShapeDtypeStruct((r, n), F32),
        scratch_shapes=[pltpu.VMEM((r, d), BF16)],
        compiler_params=_params(1),
        name="modulation",
    )(c_all, w_ada.reshape(depth * d, n), b_ada.reshape(depth, 1, n))


def _prenorm_kernel(x_ref, g_ref, scale_ref, shift_ref, *rest):
    o_ref = rest[-1]
    x = x_ref[...]
    y = x * lax.rsqrt(jnp.mean(x * x, axis=-1, keepdims=True) + RMS_EPS)
    y = y * g_ref[...]
    o_ref[...] = (y * (1.0 + scale_ref[...]) + shift_ref[...]).astype(o_ref.dtype)


def _prenorm(x2d, g, mod, mod_row0, rows_per_mod, tm, m_total, row0, h_all=None):
    m, d = x2d.shape
    if rows_per_mod == 1:
        assert mod_row0 % tm == 0
        mk = lambda c: pl.BlockSpec((tm, d), lambda i: (mod_row0 // tm + i, c))
    else:
        assert rows_per_mod % tm == 0
        mod = mod.reshape(mod.shape[0], 1, mod.shape[1])
        mk = lambda c: pl.BlockSpec((None, 1, d), lambda i: (mod_row0 + i * tm // rows_per_mod, 0, c))
    in_specs = [
        pl.BlockSpec((tm, d), lambda i: (i, 0)),
        pl.BlockSpec((1, d), lambda i: (0, 0)),
        mk(1),
        mk(0),
    ]
    args = [x2d, g.reshape(1, d), mod, mod]
    aliases = {}
    if h_all is not None:
        in_specs.append(pl.BlockSpec(memory_space=pl.ANY))
        args.append(h_all)
        aliases = {4: 0}
    assert row0 % tm == 0
    return pl.pallas_call(
        _prenorm_kernel,
        grid=(m // tm,),
        in_specs=in_specs,
        out_specs=pl.BlockSpec((tm, d), lambda i: (row0 // tm + i, 0)),
        out_shape=jax.ShapeDtypeStruct((m_total, d), BF16),
        input_output_aliases=aliases,
        compiler_params=_params(1),
        name="prenorm",
    )(*args)


def _headnorm(a, g):
    return a * lax.rsqrt(jnp.mean(a * a, axis=-1, keepdims=True) + RMS_EPS) * g


def _cast_weights(w_refs, wb_refs):
    @pl.when(pl.program_id(1) == 0)
    def _():
        for w_ref, wb_ref in zip(w_refs, wb_refs):
            wb_ref[...] = w_ref[...].astype(BF16)


def _proj_qnorm_kernel(h_ref, w_ref, g_ref, o_ref, wb):
    _cast_weights([w_ref], [wb])
    acc = jnp.dot(h_ref[...], wb[...], preferred_element_type=F32)
    g = g_ref[...]
    for c in range(acc.shape[1] // HEAD_DIM):
        sl = slice(c * HEAD_DIM, (c + 1) * HEAD_DIM)
        o_ref[:, sl] = _headnorm(acc[:, sl], g).astype(o_ref.dtype)


def _proj_kv_kernel(h_ref, wk_ref, wv_ref, g_ref, k_ref, v_ref, wkb, wvb):
    _cast_weights([wk_ref, wv_ref], [wkb, wvb])
    h = h_ref[...]
    acc = jnp.dot(h, wkb[...], preferred_element_type=F32)
    g = g_ref[...]
    for c in range(acc.shape[1] // HEAD_DIM):
        sl = slice(c * HEAD_DIM, (c + 1) * HEAD_DIM)
        k_ref[:, sl] = _headnorm(acc[:, sl], g)
    v_ref[...] = jnp.dot(h, wvb[...], preferred_element_type=F32)


def _proj_silu_kernel(h_ref, w_ref, o_ref, wb):
    _cast_weights([w_ref], [wb])
    acc = jnp.dot(h_ref[...], wb[...], preferred_element_type=F32)
    o_ref[...] = _silu(acc).astype(o_ref.dtype)


def _proj_glu_kernel(h_ref, wa_ref, wb_ref, o_ref, wab, wbb):
    _cast_weights([wa_ref, wb_ref], [wab, wbb])
    h = h_ref[...]
    a = jnp.dot(h, wab[...], preferred_element_type=F32)
    b = jnp.dot(h, wbb[...], preferred_element_type=F32)
    u = a * jax.nn.sigmoid(b)
    for c in range(u.shape[1] // LANES):
        o_ref[c] = u[:, c * LANES:(c + 1) * LANES]


def _proj_call(kernel, h, w2d, l, col_offs, width, extra, out_dtypes, tn, name, slab_out=False):
    m, d = h.shape
    w_specs = [
        pl.BlockSpec((d, tn), functools.partial(lambda j, i, o: (l, o + j), o=off // tn))
        for off in col_offs
    ]
    extra_specs = [pl.BlockSpec(e.shape, lambda j, i: (0, 0)) for e in extra]
    if slab_out:
        out_specs = [pl.BlockSpec((tn // LANES, TM, LANES), lambda j, i: (j, i, 0))]
        out_shape = [jax.ShapeDtypeStruct((width // LANES, m, LANES), out_dtypes[0])]
    else:
        out_specs = [pl.BlockSpec((TM, tn), lambda j, i: (i, j))] * len(out_dtypes)
        out_shape = [jax.ShapeDtypeStruct((m, width), dt) for dt in out_dtypes]
    return pl.pallas_call(
        kernel,
        grid=(width // tn, pl.cdiv(m, TM)),
        in_specs=[pl.BlockSpec((TM, d), lambda j, i: (i, 0))] + w_specs + extra_specs,
        out_specs=out_specs,
        out_shape=out_shape,
        scratch_shapes=[pltpu.VMEM((d, tn), BF16) for _ in col_offs],
        compiler_params=_params(2),
        name=name,
    )(h, *([w2d] * len(col_offs)), *extra)


def _in_projection(h, w_in, l, q_g, k_g):
    depth, d, n = w_in.shape
    w2d = w_in.reshape(depth * d, n)
    qg = q_g.reshape(1, HEAD_DIM)
    kg = k_g.reshape(1, HEAD_DIM)
    (q,) = _proj_call(_proj_qnorm_kernel, h, w2d, l, [OFF_Q], D_ATT, [qg], [BF16], 512, "proj_q")
    k, v = _proj_call(_proj_kv_kernel, h, w2d, l, [OFF_K, OFF_V], KV_W, [kg], [F32, F32], 256, "proj_kv")
    (ga,) = _proj_call(_proj_silu_kernel, h, w2d, l, [OFF_GA], D_ATT, [], [BF16], 512, "proj_ga")
    (u,) = _proj_call(_proj_glu_kernel, h, w2d, l, [OFF_CA, OFF_CB], D_CONV, [], [F32], 256, "proj_glu",
                      slab_out=True)
    (cg,) = _proj_call(_proj_silu_kernel, h, w2d, l, [OFF_CG], D_CONV, [], [BF16], 512, "proj_cg")
    return q, k, v, ga, u, cg


def _softmax_pv(sc, valid, slope_dist, sink, vb):
    sc = sc * ATT_SCALE - slope_dist
    sc = jnp.where(valid, sc, NEG_INF)
    m = jnp.maximum(jnp.max(sc, axis=-1, keepdims=True), sink)
    p = jnp.exp(sc - m)
    denom = jnp.sum(p, axis=-1, keepdims=True) + jnp.exp(sink - m)
    return jnp.dot(p.astype(BF16), vb, preferred_element_type=F32) / denom


def _prompt_attn_kernel(sinks_ref, q_ref, kp_ref, kc_ref, vp_ref, vc_ref, ga_ref, o_ref, *, bq):
    first = (pl.program_id(1) == 0).astype(jnp.int32)
    nsub = bq // WINDOW
    k_all = jnp.concatenate([kp_ref[...], kc_ref[...]], axis=0).astype(BF16)
    v_all = jnp.concatenate([vp_ref[...], vc_ref[...]], axis=0).astype(BF16)
    qi = lax.broadcasted_iota(jnp.int32, (WINDOW, 2 * WINDOW), 0)
    kj = lax.broadcasted_iota(jnp.int32, (WINDOW, 2 * WINDOW), 1)
    dist = WINDOW + qi - kj
    in_band = (dist >= 0) & (dist < WINDOW)
    valid_first = in_band & (kj >= WINDOW * first)
    distf = dist.astype(F32)
    for s in range(nsub):
        valid = valid_first if s == 0 else in_band
        rows = slice(s * WINDOW, (s + 1) * WINDOW)
        for kv in range(N_KV):
            cols = slice(kv * HEAD_DIM, (kv + 1) * HEAD_DIM)
            kb = k_all[s * WINDOW:(s + 2) * WINDOW, cols]
            vb = v_all[s * WINDOW:(s + 2) * WINDOW, cols]
            for g in range(GQA):
                h = kv * GQA + g
                hc = slice(h * HEAD_DIM, (h + 1) * HEAD_DIM)
                sc = lax.dot_general(q_ref[rows, hc], kb, (((1,), (1,)), ((), ())),
                                     preferred_element_type=F32)
                o = _softmax_pv(sc, valid, SLOPES[h] * distf, sinks_ref[h], vb)
                o_ref[rows, hc] = (o * ga_ref[rows, hc].astype(F32)).astype(o_ref.dtype)


def _prompt_attention(q, k, v, ga, sinks, n_batch, seq, bq=256):
    m = q.shape[0]
    nb = seq // bq
    ratio = bq // WINDOW
    cur = lambda n, i, s: (n * nb + i, 0)
    prev = lambda n, i, s: (jnp.maximum((n * nb + i) * ratio - 1, 0), 0)
    return pl.pallas_call(
        functools.partial(_prompt_attn_kernel, bq=bq),
        grid_spec=pltpu.PrefetchScalarGridSpec(
            num_scalar_prefetch=1,
            grid=(n_batch, nb),
            in_specs=[
                pl.BlockSpec((bq, D_ATT), cur),
                pl.BlockSpec((WINDOW, KV_W), prev),
                pl.BlockSpec((bq, KV_W), cur),
                pl.BlockSpec((WINDOW, KV_W), prev),
                pl.BlockSpec((bq, KV_W), cur),
                pl.BlockSpec((bq, D_ATT), cur),
            ],
            out_specs=pl.BlockSpec((bq, D_ATT), cur),
        ),
        out_shape=jax.ShapeDtypeStruct((m, D_ATT), BF16),
        compiler_params=_params(2),
        name="prompt_attention",
    )(sinks, q, k, k, v, v, ga)


def _sample_attn_kernel(slope_ref, sink_ref, q_ref, kc_ref, kn_ref, vc_ref, vn_ref, ga_ref,
                        o_ref, ko_ref, vo_ref, *, bn, s_len):
    nk = WINDOW + SUBLANES
    r = GQA * s_len
    rows = bn * N_KV * r
    ri = lax.broadcasted_iota(jnp.int32, (rows, nk), 0)
    kj = lax.broadcasted_iota(jnp.int32, (rows, nk), 1)
    dist = lax.rem(ri, s_len) + WINDOW - kj
    valid = (dist >= 0) & (dist < WINDOW)
    shift = s_len * N_KV
    keep = WINDOW * N_KV - shift

    def head_rows(win_ref, new_ref, b, kv):
        win = win_ref.at[b][pl.ds(kv, WINDOW, stride=N_KV), :]
        new = new_ref.at[b][pl.ds(kv, SUBLANES, stride=N_KV), :]
        return jnp.concatenate([win, new], axis=0).astype(BF16)

    chains = [(b, kv) for b in range(bn) for kv in range(N_KV)]
    sc = jnp.concatenate(
        [lax.dot_general(q_ref[pl.ds(c * r, r), :], head_rows(kc_ref, kn_ref, b, kv),
                         (((1,), (1,)), ((), ())), preferred_element_type=F32)
         for c, (b, kv) in enumerate(chains)], axis=0)
    sc = sc * ATT_SCALE - slope_ref[...] * dist.astype(F32)
    sc = jnp.where(valid, sc, NEG_INF)
    sink = sink_ref[...]
    m = jnp.maximum(jnp.max(sc, axis=-1, keepdims=True), sink)
    p = jnp.exp(sc - m)
    denom = jnp.sum(p, axis=-1, keepdims=True) + jnp.exp(sink - m)
    p = p.astype(BF16)
    o = jnp.concatenate(
        [jnp.dot(p[c * r:(c + 1) * r], head_rows(vc_ref, vn_ref, b, kv), preferred_element_type=F32)
         for c, (b, kv) in enumerate(chains)], axis=0)
    o_ref[...] = (o / denom * ga_ref[...].astype(F32)).astype(o_ref.dtype)

    for b in range(bn):
        for win_ref, new_ref, out_ref in ((kc_ref, kn_ref, ko_ref), (vc_ref, vn_ref, vo_ref)):
            out_ref[b, pl.ds(0, keep), :] = win_ref[b, pl.ds(shift, keep), :]
            out_ref[b, pl.ds(keep, shift), :] = new_ref[b, pl.ds(0, shift), :]


def _to_head_rows(a, n, s_len):
    a = a.reshape(n, s_len, N_KV, GQA, HEAD_DIM)
    return jnp.transpose(a, (0, 2, 3, 1, 4)).reshape(n * N_KV * GQA * s_len, HEAD_DIM)


def _from_head_rows(a, n, s_len):
    a = a.reshape(n, N_KV, GQA, s_len, HEAD_DIM)
    return jnp.transpose(a, (0, 3, 1, 2, 4)).reshape(n * s_len, D_ATT)


def _sample_attention(q, k_new, v_new, ga, cache_k, cache_v, l, sinks, n, s_len, bn=8):
    assert s_len <= SUBLANES and n % bn == 0
    r = GQA * s_len
    depth = cache_k.shape[0]
    rows = bn * N_KV * r
    per_row = lambda a: jnp.broadcast_to(a.astype(F32).reshape(1, N_HEADS, 1, 1),
                                         (bn, N_HEADS, s_len, 1)).reshape(rows, 1)
    slope_rows = per_row(jnp.asarray(SLOPES, F32))
    sink_rows = per_row(sinks)
    rows_new = SUBLANES * N_KV
    new_rows = lambda a: jnp.pad(a.reshape(n, s_len * N_KV, HEAD_DIM),
                                 ((0, 0), (0, rows_new - s_len * N_KV), (0, 0)))
    win_rows = lambda c: c.reshape(depth * n, WINDOW * N_KV, HEAD_DIM)
    heads = pl.BlockSpec((rows, HEAD_DIM), lambda i: (i, 0))
    win_in = pl.BlockSpec((bn, WINDOW * N_KV, HEAD_DIM), lambda i: (l * (n // bn) + i, 0, 0))
    win_out = pl.BlockSpec((bn, WINDOW * N_KV, HEAD_DIM), lambda i: (i, 0, 0))
    new = pl.BlockSpec((bn, rows_new, HEAD_DIM), lambda i: (i, 0, 0))
    small = pl.BlockSpec((rows, 1), lambda i: (0, 0))
    win_shape = jax.ShapeDtypeStruct((n, WINDOW * N_KV, HEAD_DIM), F32)
    o, k_win, v_win = pl.pallas_call(
        functools.partial(_sample_attn_kernel, bn=bn, s_len=s_len),
        grid=(n // bn,),
        in_specs=[small, small, heads, win_in, new, win_in, new, heads],
        out_specs=[heads, win_out, win_out],
        out_shape=[jax.ShapeDtypeStruct((n * N_KV * r, HEAD_DIM), BF16), win_shape, win_shape],
        compiler_params=_params(1),
        name="sample_attention",
    )(slope_rows, sink_rows, _to_head_rows(q, n, s_len), win_rows(cache_k), new_rows(k_new),
      win_rows(cache_v), new_rows(v_new), _to_head_rows(ga, n, s_len))
    win5 = lambda a: a.reshape(n, WINDOW, N_KV, HEAD_DIM)
    return _from_head_rows(o, n, s_len), win5(k_win), win5(v_win)


def _ln_pw_gate(y, lng_ref, lnb_ref, wpw, cg_ref, o_ref):
    mu = jnp.mean(y, axis=-1, keepdims=True)
    yc = y - mu
    var = jnp.mean(yc * yc, axis=-1, keepdims=True)
    yn = yc * lax.rsqrt(var + LN_EPS) * lng_ref[...] + lnb_ref[...]
    a = _silu(yn).astype(BF16)
    o = jnp.dot(a, wpw, preferred_element_type=F32)
    o_ref[...] = (o * cg_ref[...].astype(F32)).astype(o_ref.dtype)


def _prompt_conv_kernel(uh_ref, uc_ref, cw_ref, cb_ref, lng_ref, lnb_ref, wpw_ref, cg_ref, o_ref,
                        ext, ybuf, wpb, *, tt, rc):
    @pl.when((pl.program_id(0) == 0) & (pl.program_id(1) == 0))
    def _():
        wpb[...] = wpw_ref[...].astype(BF16)

    @pl.when(pl.program_id(1) == 0)
    def _():
        ext[:, pl.ds(0, HALO), :] = jnp.zeros((N_SLAB, HALO, LANES), F32)

    @pl.when(pl.program_id(1) != 0)
    def _():
        ext[:, pl.ds(0, HALO), :] = uh_ref[...]

    ext[:, pl.ds(HALO, tt), :] = uc_ref[...]
    base = HALO - HIST

    def rows(i, carry):
        r0 = pl.multiple_of(i * rc, rc)
        for s in range(N_SLAB):
            acc = jnp.broadcast_to(cb_ref[s], (rc, LANES))
            for k in range(CONV_W):
                win = ext.at[s][pl.ds(r0 + (base + k), rc, stride=1), :]
                acc = acc + cw_ref[s, pl.ds(k, 1), :] * win
            ybuf[pl.ds(r0, rc), pl.ds(s * LANES, LANES)] = acc
        return carry

    lax.fori_loop(0, tt // rc, rows, 0)
    _ln_pw_gate(ybuf[...], lng_ref, lnb_ref, wpb[...], cg_ref, o_ref)


def _slab_weights(conv_w, conv_b):
    cw = jnp.pad(conv_w, ((0, HALO - CONV_W), (0, 0)))
    cw = jnp.transpose(cw.reshape(HALO, N_SLAB, LANES), (1, 0, 2))
    return cw, conv_b.reshape(N_SLAB, 1, LANES)


def _prompt_conv(u_slab, cg, conv_w, conv_b, ln_g, ln_b, w_pw2, l, n_batch, seq, tt=512):
    nb = seq // tt
    ratio = tt // HALO
    depth, c, _ = w_pw2.shape
    cw, cb = _slab_weights(conv_w, conv_b)
    cur = lambda n, i: (n * nb + i, 0)
    const2 = lambda n, i: (0, 0)
    const3 = lambda n, i: (0, 0, 0)
    return pl.pallas_call(
        functools.partial(_prompt_conv_kernel, tt=tt, rc=32),
        grid=(n_batch, nb),
        in_specs=[
            pl.BlockSpec((N_SLAB, HALO, LANES), lambda n, i: (0, jnp.maximum((n * nb + i) * ratio - 1, 0), 0)),
            pl.BlockSpec((N_SLAB, tt, LANES), lambda n, i: (0, n * nb + i, 0)),
            pl.BlockSpec((N_SLAB, HALO, LANES), const3),
            pl.BlockSpec((N_SLAB, 1, LANES), const3),
            pl.BlockSpec((1, c), const2),
            pl.BlockSpec((1, c), const2),
            pl.BlockSpec((c, c), lambda n, i: (l, 0), pipeline_mode=pl.Buffered(1)),
            pl.BlockSpec((tt, c), cur),
        ],
        out_specs=pl.BlockSpec((tt, c), cur),
        out_shape=jax.ShapeDtypeStruct((cg.shape[0], c), BF16),
        scratch_shapes=[pltpu.VMEM((N_SLAB, HALO + tt, LANES), F32), pltpu.VMEM((tt, c), F32),
                        pltpu.VMEM((c, c), BF16)],
        compiler_params=_params(2),
        name="prompt_conv",
    )(u_slab, u_slab, cw, cb, ln_g.reshape(1, -1), ln_b.reshape(1, -1), w_pw2.reshape(depth * c, c), cg)


def _sample_dwconv_kernel(st_ref, u_ref, ws_ref, wu_ref, cb_ref, y_ref, so_ref, *, s_len):
    st = st_ref[...]
    u = u_ref[...]
    for t in range(s_len):
        y = jnp.sum(st * ws_ref[t][None], axis=1) + jnp.sum(u * wu_ref[t][None], axis=1)
        y_ref[t] = y + cb_ref[...]
    so_ref[:, pl.ds(0, HIST - s_len), :] = st_ref[:, pl.ds(s_len, HIST - s_len), :]
    so_ref[:, pl.ds(HIST - s_len, s_len), :] = u


def _sample_dwconv(state, l, u3, conv_w, conv_b, bn=8):
    n, s_len, c = u3.shape
    depth = state.shape[0]
    ws = jnp.stack([jnp.concatenate([jnp.zeros((t, c), F32), conv_w[:HIST - t]], axis=0)
                    for t in range(s_len)])
    wu = jnp.stack([jnp.concatenate([conv_w[HIST - t:], jnp.zeros((s_len - 1 - t, c), F32)], axis=0)
                    for t in range(s_len)])
    return pl.pallas_call(
        functools.partial(_sample_dwconv_kernel, s_len=s_len),
        grid=(n // bn,),
        in_specs=[
            pl.BlockSpec((bn, HIST, c), lambda i: (l * (n // bn) + i, 0, 0)),
            pl.BlockSpec((bn, s_len, c), lambda i: (i, 0, 0)),
            pl.BlockSpec((s_len, HIST, c), lambda i: (0, 0, 0)),
            pl.BlockSpec((s_len, s_len, c), lambda i: (0, 0, 0)),
            pl.BlockSpec((1, c), lambda i: (0, 0)),
        ],
        out_specs=[pl.BlockSpec((s_len, bn, c), lambda i: (0, i, 0)),
                   pl.BlockSpec((bn, HIST, c), lambda i: (i, 0, 0))],
        out_shape=[jax.ShapeDtypeStruct((s_len, n, c), F32), jax.ShapeDtypeStruct((n, HIST, c), F32)],
        compiler_params=_params(1),
        name="sample_dwconv",
    )(state.reshape(depth * n, HIST, c), u3, ws, wu, conv_b.reshape(1, c))


def _ln_pw_kernel(y_ref, lng_ref, lnb_ref, wpw_ref, cg_ref, o_ref):
    _ln_pw_gate(y_ref[...], lng_ref, lnb_ref, wpw_ref[...].astype(BF16), cg_ref, o_ref)


def _ln_pw(y, cg, cg_row0, ln_g, ln_b, w_pw2, l):
    m, c = y.shape
    depth = w_pw2.shape[0]
    assert cg_row0 % m == 0
    const = lambda i: (0, 0)
    return pl.pallas_call(
        _ln_pw_kernel,
        grid=(1,),
        in_specs=[
            pl.BlockSpec((m, c), const),
            pl.BlockSpec((1, c), const),
            pl.BlockSpec((1, c), const),
            pl.BlockSpec((c, c), lambda i: (l, 0), pipeline_mode=pl.Buffered(1)),
            pl.BlockSpec((m, c), lambda i: (cg_row0 // m, 0)),
        ],
        out_specs=pl.BlockSpec((m, c), const),
        out_shape=jax.ShapeDtypeStruct((m, c), BF16),
        compiler_params=_params(1),
        name="sample_ln_pw",
    )(y, ln_g.reshape(1, -1), ln_b.reshape(1, -1), w_pw2.reshape(depth * c, c), cg)


def _outproj_kernel(a_ref, c_ref, w_ref, xp_ref, gp_ref, xs_ref, gs_ref, yp_ref, ys_ref, wb,
                    *, n_prompt_tiles, ms):
    i = pl.program_id(1)

    @pl.when(i == 0)
    def _():
        wb[...] = w_ref[...].astype(BF16)

    o = jnp.dot(a_ref[...], wb[pl.ds(0, D_ATT), :], preferred_element_type=F32)
    o = o + jnp.dot(c_ref[...], wb[pl.ds(D_ATT, D_CONV), :], preferred_element_type=F32)

    @pl.when(i < n_prompt_tiles)
    def _():
        yp_ref[...] = xp_ref[...] + gp_ref[...] * o

    @pl.when(i == n_prompt_tiles)
    def _():
        ys_ref[...] = xs_ref[...] + gs_ref[...] * o[:ms]


def _outproj(att, conv, w_out, l, xp, xs, mod, n_batch, seq, tn=512):
    mp, d = xp.shape
    ms = xs.shape[0]
    depth = w_out.shape[0]
    assert mp % TM == 0 and seq % TM == 0 and ms <= TM
    npt = mp // TM
    gate_col0 = 2 * d // tn
    mod3 = mod.reshape(mod.shape[0], 1, mod.shape[1])
    ptile = lambda j, i: (jnp.minimum(i, npt - 1), j)
    return pl.pallas_call(
        functools.partial(_outproj_kernel, n_prompt_tiles=npt, ms=ms),
        grid=(d // tn, npt + 1),
        in_specs=[
            pl.BlockSpec((TM, D_ATT), lambda j, i: (i, 0)),
            pl.BlockSpec((TM, D_CONV), lambda j, i: (i, 0)),
            pl.BlockSpec((D_ATT + D_CONV, tn), lambda j, i: (l, j)),
            pl.BlockSpec((TM, tn), ptile),
            pl.BlockSpec((None, 1, tn),
                         lambda j, i: (ms + jnp.minimum(i, npt - 1) * TM // seq, 0, gate_col0 + j)),
            pl.BlockSpec((ms, tn), lambda j, i: (0, j)),
            pl.BlockSpec((ms, tn), lambda j, i: (0, gate_col0 + j)),
        ],
        out_specs=[pl.BlockSpec((TM, tn), ptile), pl.BlockSpec((ms, tn), lambda j, i: (0, j))],
        out_shape=[jax.ShapeDtypeStruct((mp, d), F32), jax.ShapeDtypeStruct((ms, d), F32)],
        scratch_shapes=[pltpu.VMEM((D_ATT + D_CONV, tn), BF16)],
        compiler_params=_params(2),
        name="outproj",
    )(att, conv, w_out.reshape(depth * (D_ATT + D_CONV), d), xp, mod3, xs, mod)


def kernel(x_prompt, x_sample, c_prompt, c_sample, cache_k_win, cache_v_win, state_conv, w_ada, b_ada,
           norm_g, w_in, q_norm_g, k_norm_g, sinks, conv_w, conv_b, ln_g, ln_b, w_pw2, w_out):
    depth = w_in.shape[0]
    nb, seq, d = x_prompt.shape
    ns, s_len, _ = x_sample.shape
    mp, ms = nb * seq, ns * s_len
    m_all = mp + ms

    xp = x_prompt.reshape(mp, d)
    xs = x_sample.reshape(ms, d)
    pad = (-(ms + nb)) % 16
    c_all = jnp.concatenate([jnp.repeat(c_sample, s_len, axis=0), c_prompt, jnp.zeros((pad, d), F32)], axis=0)

    kp_l, vp_l, cp_l, ks_l, vs_l, cs_l = [], [], [], [], [], []
    for l in range(depth):
        mod = _modulation(c_all, w_ada, b_ada, l)

        h = _prenorm(xp, norm_g[l], mod, ms, seq, 512, m_all, 0)
        h = _prenorm(xs, norm_g[l], mod, 0, 1, ms, m_all, mp, h_all=h)
        q, k, v, ga, u, cg = _in_projection(h, w_in, l, q_norm_g[l], k_norm_g[l])

        att_p = _prompt_attention(q, k, v, ga, sinks[l], nb, seq)
        att_s, k_win, v_win = _sample_attention(q[mp:], k[mp:], v[mp:], ga[mp:], cache_k_win, cache_v_win,
                                                l, sinks[l], ns, s_len)
        att = lax.dynamic_update_slice(att_p, att_s, (mp, 0))

        conv_p = _prompt_conv(u, cg, conv_w[l], conv_b[l], ln_g[l], ln_b[l], w_pw2, l, nb, seq)
        u_s = jnp.transpose(u[:, mp:, :], (1, 0, 2)).reshape(ns, s_len, D_CONV)
        y_s, conv_state = _sample_dwconv(state_conv, l, u_s, conv_w[l], conv_b[l])
        y_s = jnp.transpose(y_s, (1, 0, 2)).reshape(ms, D_CONV)
        conv_s = _ln_pw(y_s, cg, mp, ln_g[l], ln_b[l], w_pw2, l)
        conv = lax.dynamic_update_slice(conv_p, conv_s, (mp, 0))

        tail = lambda a, rows, axis: jnp.stack(
            [lax.slice_in_dim(a, (b + 1) * seq - rows, (b + 1) * seq, axis=axis) for b in range(nb)])
        kp_l.append(tail(k, WINDOW, 0).reshape(nb, WINDOW, N_KV, HEAD_DIM))
        vp_l.append(tail(v, WINDOW, 0).reshape(nb, WINDOW, N_KV, HEAD_DIM))
        cp_l.append(jnp.transpose(tail(u, HIST, 1), (0, 2, 1, 3)).reshape(nb, HIST, D_CONV))
        ks_l.append(k_win)
        vs_l.append(v_win)
        cs_l.append(conv_state)

        xp, xs = _outproj(att, conv, w_out, l, xp, xs, mod, nb, seq)

    return (xp.reshape(nb, seq, d), xs.reshape(ns, s_len, d),
            jnp.stack(kp_l), jnp.stack(vp_l), jnp.stack(cp_l),
            jnp.stack(ks_l), jnp.stack(vs_l), jnp.stack(cs_l))
```

```python
import functools

import numpy as np
import jax
import jax.numpy as jnp
from jax import lax
from jax.experimental import pallas as pl
from jax.experimental.pallas import tpu as pltpu

F32 = jnp.float32
BF16 = jnp.bfloat16

D_MODEL = 4096
D_ATT = 2048
D_CONV = 2048
HEAD_DIM = 128
N_HEADS = 16
N_KV = 4
GQA = 4
KV_W = N_KV * HEAD_DIM
WINDOW = 128
CONV_W = 31
HIST = CONV_W - 1
LANES = 128
SUBLANES = 8
HALO = 32
N_SLAB = D_CONV // LANES
RMS_EPS = 1e-6
LN_EPS = 1e-5
NEG_INF = -1e30
ATT_SCALE = HEAD_DIM ** -0.5
OFF_Q, OFF_K, OFF_V, OFF_GA, OFF_CA, OFF_CB, OFF_CG = 0, 2048, 2560, 3072, 5120, 7168, 9216
SLOPES = [float(np.float32(2.0 ** (-8.0 * (h + 1) / N_HEADS))) for h in range(N_HEADS)]

VMEM_LIMIT = 58 * 1024 * 1024
TM = 1024
MOD_ROWS = 8


def _params(n_axes):
    return pltpu.CompilerParams(dimension_semantics=("arbitrary",) * n_axes,
                                vmem_limit_bytes=VMEM_LIMIT)


def _silu(x):
    return x * jax.nn.sigmoid(x)


def _select_row(ref, idx, count):
    row = ref[0:1, :]
    for n in range(1, count):
        row = jnp.where(idx == n, ref[n:n + 1, :], row)
    return row


def _mod_kernel(c_ref, w_ref, b_ref, o_ref, act):
    @pl.when(pl.program_id(0) == 0)
    def _():
        act[...] = _silu(c_ref[...]).astype(BF16)

    o_ref[...] = jnp.dot(act[...], w_ref[...].astype(BF16), preferred_element_type=F32) + b_ref[...]


def _modulation(c_all, w_ada, b_ada, l, tn=512):
    r, d = c_all.shape
    depth, _, n = w_ada.shape
    return pl.pallas_call(
        _mod_kernel,
        grid=(n // tn,),
        in_specs=[
            pl.BlockSpec((r, d), lambda j: (0, 0)),
            pl.BlockSpec((d, tn), lambda j: (l, j)),
            pl.BlockSpec((None, 1, tn), lambda j: (l, 0, j)),
        ],
        out_specs=pl.BlockSpec((r, tn), lambda j: (0, j)),
        out_shape=jax.ShapeDtypeStruct((r, n), F32),
        scratch_shapes=[pltpu.VMEM((r, d), BF16)],
        compiler_params=_params(1),
        name="modulation",
    )(c_all, w_ada.reshape(depth * d, n), b_ada.reshape(depth, 1, n))


def _prenorm_kernel(x_ref, g_ref, scale_ref, shift_ref, *rest, tiles_per_batch, n_batch):
    o_ref = rest[-1]
    x = x_ref[...]
    y = x * lax.rsqrt(jnp.mean(x * x, axis=-1, keepdims=True) + RMS_EPS)
    y = y * g_ref[...]
    if tiles_per_batch is None:
        scale, shift = scale_ref[...], shift_ref[...]
    else:
        batch = pl.program_id(0) // tiles_per_batch
        scale = _select_row(scale_ref, batch, n_batch)
        shift = _select_row(shift_ref, batch, n_batch)
    o_ref[...] = (y * (1.0 + scale) + shift).astype(o_ref.dtype)


def _prenorm(x2d, g, mod, mod_row0, n_batch, tm, m_total, row0, h_all=None):
    m, d = x2d.shape
    if n_batch is None:
        assert mod_row0 % tm == 0
        mk = lambda c: pl.BlockSpec((tm, d), lambda i: (mod_row0 // tm, c))
        tiles_per_batch = None
    else:
        assert mod_row0 % MOD_ROWS == 0 and n_batch <= MOD_ROWS and (m // n_batch) % tm == 0
        mk = lambda c: pl.BlockSpec((MOD_ROWS, d), lambda i: (mod_row0 // MOD_ROWS, c))
        tiles_per_batch = m // n_batch // tm
    in_specs = [pl.BlockSpec((tm, d), lambda i: (i, 0)), pl.BlockSpec((1, d), lambda i: (0, 0)), mk(1), mk(0)]
    args = [x2d, g.reshape(1, d), mod, mod]
    aliases = {}
    if h_all is not None:
        in_specs.append(pl.BlockSpec(memory_space=pl.ANY))
        args.append(h_all)
        aliases = {4: 0}
    assert row0 % tm == 0
    return pl.pallas_call(
        functools.partial(_prenorm_kernel, tiles_per_batch=tiles_per_batch, n_batch=n_batch),
        grid=(m // tm,),
        in_specs=in_specs,
        out_specs=pl.BlockSpec((tm, d), lambda i: (row0 // tm + i, 0)),
        out_shape=jax.ShapeDtypeStruct((m_total, d), BF16),
        input_output_aliases=aliases,
        compiler_params=_params(1),
        name="prenorm",
    )(*args)


def _headnorm(a, g):
    return a * lax.rsqrt(jnp.mean(a * a, axis=-1, keepdims=True) + RMS_EPS) * g


def _cast_weights(w_refs, wb_refs):
    @pl.when(pl.program_id(1) == 0)
    def _():
        for w_ref, wb_ref in zip(w_refs, wb_refs):
            wb_ref[...] = w_ref[...].astype(BF16)


def _proj_qnorm_kernel(h_ref, w_ref, g_ref, o_ref, wb):
    _cast_weights([w_ref], [wb])
    acc = jnp.dot(h_ref[...], wb[...], preferred_element_type=F32)
    g = g_ref[...]
    for c in range(acc.shape[1] // HEAD_DIM):
        sl = slice(c * HEAD_DIM, (c + 1) * HEAD_DIM)
        o_ref[:, sl] = _headnorm(acc[:, sl], g).astype(o_ref.dtype)


def _proj_kv_kernel(h_ref, wk_ref, wv_ref, g_ref, k_ref, v_ref, wkb, wvb):
    _cast_weights([wk_ref, wv_ref], [wkb, wvb])
    h = h_ref[...]
    acc = jnp.dot(h, wkb[...], preferred_element_type=F32)
    g = g_ref[...]
    for c in range(acc.shape[1] // HEAD_DIM):
        sl = slice(c * HEAD_DIM, (c + 1) * HEAD_DIM)
        k_ref[:, sl] = _headnorm(acc[:, sl], g)
    v_ref[...] = jnp.dot(h, wvb[...], preferred_element_type=F32)


def _proj_silu_kernel(h_ref, w_ref, o_ref, wb):
    _cast_weights([w_ref], [wb])
    acc = jnp.dot(h_ref[...], wb[...], preferred_element_type=F32)
    o_ref[...] = _silu(acc).astype(o_ref.dtype)


def _proj_call(kernel, h, w2d, l, col_offs, width, extra, out_dtypes, tm, tn, name):
    m, d = h.shape
    assert m % tm == 0
    w_specs = [
        pl.BlockSpec((d, tn), functools.partial(lambda j, i, o: (l, o + j), o=off // tn))
        for off in col_offs
    ]
    extra_specs = [pl.BlockSpec(e.shape, lambda j, i: (0, 0)) for e in extra]
    return pl.pallas_call(
        kernel,
        grid=(width // tn, m // tm),
        in_specs=[pl.BlockSpec((tm, d), lambda j, i: (i, 0))] + w_specs + extra_specs,
        out_specs=[pl.BlockSpec((tm, tn), lambda j, i: (i, j))] * len(out_dtypes),
        out_shape=[jax.ShapeDtypeStruct((m, width), dt) for dt in out_dtypes],
        scratch_shapes=[pltpu.VMEM((d, tn), BF16) for _ in col_offs],
        compiler_params=_params(2),
        name=name,
    )(h, *([w2d] * len(col_offs)), *extra)


def _proj_glu_conv_kernel(h_ref, wa_ref, wb_ref, cw_ref, cb_ref, y_ref, us_ref, ut_ref, wab, wbb, ubuf0, ubuf1,
                          *, tm, rc, tiles_per_seq, n_prompt_tiles, ms):
    j, i = pl.program_id(0), pl.program_id(1)
    nsl = y_ref.shape[1] // LANES
    base = HALO - HIST

    @pl.when(i == 0)
    def _():
        wab[...] = wa_ref[...].astype(BF16)
        wbb[...] = wb_ref[...].astype(BF16)

    @pl.when((i == 0) & (j == 0))
    def _():
        ubuf1[...] = jnp.zeros(ubuf1.shape, F32)

    def step(cur, prv):
        for c in range(nsl):
            bias = jnp.broadcast_to(cb_ref[c], (rc, LANES))
            for r0 in range(0, tm, rc):
                acc = bias
                for k in range(CONV_W):
                    acc = acc + cw_ref[c, pl.ds(k, 1), :] * prv[c, pl.ds(r0 + base + k, rc), :]
                y_ref[pl.ds(r0, rc), pl.ds(c * LANES, LANES)] = acc
        h = h_ref[...]
        a = jnp.dot(h, wab[...], preferred_element_type=F32)
        b = jnp.dot(h, wbb[...], preferred_element_type=F32)
        u = a * jax.nn.sigmoid(b)
        seq_start = lax.rem(i, tiles_per_seq) == 0
        for c in range(nsl):
            cur[c, pl.ds(0, HALO), :] = jnp.where(seq_start, 0.0, prv[c, pl.ds(tm, HALO), :])
            cur[c, pl.ds(HALO, tm), :] = u[:, c * LANES:(c + 1) * LANES]

        @pl.when(i == n_prompt_tiles)
        def _():
            for c in range(nsl):
                us_ref[c] = cur[c, pl.ds(HALO, ms), :]

        @pl.when((lax.rem(i, tiles_per_seq) == tiles_per_seq - 1) & (i < n_prompt_tiles))
        def _():
            for c in range(nsl):
                ut_ref[c] = cur[c, pl.ds(tm, HALO), :]

    pl.when(lax.rem(i, 2) == 0)(lambda: step(ubuf0, ubuf1))
    pl.when(lax.rem(i, 2) == 1)(lambda: step(ubuf1, ubuf0))


def _slab_weights(conv_w, conv_b):
    cw = jnp.pad(conv_w, ((0, HALO - CONV_W), (0, 0)))
    cw = jnp.transpose(cw.reshape(HALO, N_SLAB, LANES), (1, 0, 2))
    return cw, conv_b.reshape(N_SLAB, 1, LANES)


def _proj_glu_conv(h, w2d, l, conv_w, conv_b, n_batch, seq, ms, tn=256):
    m, d = h.shape
    mp = n_batch * seq
    assert seq % TM == 0 and ms <= TM and m == mp + ms
    npt = mp // TM
    assert npt % 2 == 0
    nsl = tn // LANES
    cw, cb = _slab_weights(conv_w, conv_b)
    wspec = lambda off: pl.BlockSpec((d, tn), lambda j, i: (l, off // tn + j))
    return pl.pallas_call(
        functools.partial(_proj_glu_conv_kernel, tm=TM, rc=32, tiles_per_seq=seq // TM,
                          n_prompt_tiles=npt, ms=ms),
        grid=(D_CONV // tn, npt + 1),
        in_specs=[
            pl.BlockSpec((TM, d), lambda j, i: (i, 0)),
            wspec(OFF_CA),
            wspec(OFF_CB),
            pl.BlockSpec((nsl, HALO, LANES), lambda j, i: (j, 0, 0)),
            pl.BlockSpec((nsl, 1, LANES), lambda j, i: (j, 0, 0)),
        ],
        out_specs=[
            pl.BlockSpec((TM, tn), lambda j, i: (jnp.maximum(i - 1, 0), j)),
            pl.BlockSpec((nsl, ms, LANES), lambda j, i: (j, 0, 0)),
            pl.BlockSpec((None, nsl, HALO, LANES),
                         lambda j, i: (jnp.minimum(i // (seq // TM), n_batch - 1), j, 0, 0)),
        ],
        out_shape=[
            jax.ShapeDtypeStruct((m, D_CONV), F32),
            jax.ShapeDtypeStruct((N_SLAB, ms, LANES), F32),
            jax.ShapeDtypeStruct((n_batch, N_SLAB, HALO, LANES), F32),
        ],
        scratch_shapes=[pltpu.VMEM((d, tn), BF16), pltpu.VMEM((d, tn), BF16),
                        pltpu.VMEM((nsl, HALO + TM, LANES), F32), pltpu.VMEM((nsl, HALO + TM, LANES), F32)],
        compiler_params=_params(2),
        name="proj_glu_conv",
    )(h, w2d, w2d, cw, cb)


def _softmax_pv(sc, valid, slope_dist, sink, vb):
    sc = sc * ATT_SCALE - slope_dist
    sc = jnp.where(valid, sc, NEG_INF)
    m = jnp.maximum(jnp.max(sc, axis=-1, keepdims=True), sink)
    p = jnp.exp(sc - m)
    denom = jnp.sum(p, axis=-1, keepdims=True) + jnp.exp(sink - m)
    return jnp.dot(p.astype(BF16), vb, preferred_element_type=F32) / denom


def _prompt_attn_kernel(sinks_ref, q_ref, kp_ref, kc_ref, vp_ref, vc_ref, ga_ref, o_ref, *, bq):
    first = (pl.program_id(1) == 0).astype(jnp.int32)
    nsub = bq // WINDOW
    k_all = jnp.concatenate([kp_ref[...], kc_ref[...]], axis=0).astype(BF16)
    v_all = jnp.concatenate([vp_ref[...], vc_ref[...]], axis=0).astype(BF16)
    qi = lax.broadcasted_iota(jnp.int32, (WINDOW, 2 * WINDOW), 0)
    kj = lax.broadcasted_iota(jnp.int32, (WINDOW, 2 * WINDOW), 1)
    dist = WINDOW + qi - kj
    in_band = (dist >= 0) & (dist < WINDOW)
    valid_first = in_band & (kj >= WINDOW * first)
    distf = dist.astype(F32)
    for s in range(nsub):
        valid = valid_first if s == 0 else in_band
        rows = slice(s * WINDOW, (s + 1) * WINDOW)
        for kv in range(N_KV):
            cols = slice(kv * HEAD_DIM, (kv + 1) * HEAD_DIM)
            kb = k_all[s * WINDOW:(s + 2) * WINDOW, cols]
            vb = v_all[s * WINDOW:(s + 2) * WINDOW, cols]
            for g in range(GQA):
                h = kv * GQA + g
                hc = slice(h * HEAD_DIM, (h + 1) * HEAD_DIM)
                sc = lax.dot_general(q_ref[rows, hc], kb, (((1,), (1,)), ((), ())),
                                     preferred_element_type=F32)
                o = _softmax_pv(sc, valid, SLOPES[h] * distf, sinks_ref[h], vb)
                o_ref[rows, hc] = (o * ga_ref[rows, hc].astype(F32)).astype(o_ref.dtype)


def _prompt_attention(q, k, v, ga, sinks, n_batch, seq, bq=256):
    m = q.shape[0]
    nb = seq // bq
    ratio = bq // WINDOW
    cur = lambda n, i, s: (n * nb + i, 0)
    prev = lambda n, i, s: (jnp.maximum((n * nb + i) * ratio - 1, 0), 0)
    return pl.pallas_call(
        functools.partial(_prompt_attn_kernel, bq=bq),
        grid_spec=pltpu.PrefetchScalarGridSpec(
            num_scalar_prefetch=1,
            grid=(n_batch, nb),
            in_specs=[
                pl.BlockSpec((bq, D_ATT), cur),
                pl.BlockSpec((WINDOW, KV_W), prev),
                pl.BlockSpec((bq, KV_W), cur),
                pl.BlockSpec((WINDOW, KV_W), prev),
                pl.BlockSpec((bq, KV_W), cur),
                pl.BlockSpec((bq, D_ATT), cur),
            ],
            out_specs=pl.BlockSpec((bq, D_ATT), cur),
        ),
        out_shape=jax.ShapeDtypeStruct((m, D_ATT), BF16),
        compiler_params=_params(2),
        name="prompt_attention",
    )(sinks, q, k, k, v, v, ga)


def _sample_attn_kernel(slope_ref, sink_ref, q_ref, kc_ref, kn_ref, vc_ref, vn_ref, ga_ref,
                        o_ref, ko_ref, vo_ref, *, bn, s_len):
    nk = WINDOW + SUBLANES
    r = GQA * s_len
    rows = bn * N_KV * r
    ri = lax.broadcasted_iota(jnp.int32, (rows, nk), 0)
    kj = lax.broadcasted_iota(jnp.int32, (rows, nk), 1)
    dist = lax.rem(ri, s_len) + WINDOW - kj
    valid = (dist >= 0) & (dist < WINDOW)
    shift = s_len * N_KV
    keep = WINDOW * N_KV - shift

    def head_rows(win_ref, new_ref, b, kv):
        win = win_ref.at[b][pl.ds(kv, WINDOW, stride=N_KV), :]
        new = new_ref.at[b][pl.ds(kv, SUBLANES, stride=N_KV), :]
        return jnp.concatenate([win, new], axis=0).astype(BF16)

    chains = [(b, kv) for b in range(bn) for kv in range(N_KV)]
    sc = jnp.concatenate(
        [lax.dot_general(q_ref[pl.ds(c * r, r), :], head_rows(kc_ref, kn_ref, b, kv),
                         (((1,), (1,)), ((), ())), preferred_element_type=F32)
         for c, (b, kv) in enumerate(chains)], axis=0)
    sc = sc * ATT_SCALE - slope_ref[...] * dist.astype(F32)
    sc = jnp.where(valid, sc, NEG_INF)
    sink = sink_ref[...]
    m = jnp.maximum(jnp.max(sc, axis=-1, keepdims=True), sink)
    p = jnp.exp(sc - m)
    denom = jnp.sum(p, axis=-1, keepdims=True) + jnp.exp(sink - m)
    p = p.astype(BF16)
    o = jnp.concatenate(
        [jnp.dot(p[c * r:(c + 1) * r], head_rows(vc_ref, vn_ref, b, kv), preferred_element_type=F32)
         for c, (b, kv) in enumerate(chains)], axis=0)
    o_ref[...] = (o / denom * ga_ref[...].astype(F32)).astype(o_ref.dtype)

    for b in range(bn):
        for win_ref, new_ref, out_ref in ((kc_ref, kn_ref, ko_ref), (vc_ref, vn_ref, vo_ref)):
            out_ref[b, pl.ds(0, keep), :] = win_ref[b, pl.ds(shift, keep), :]
            out_ref[b, pl.ds(keep, shift), :] = new_ref[b, pl.ds(0, shift), :]


def _to_head_rows(a, n, s_len):
    a = a.reshape(s_len, n, N_KV, GQA, HEAD_DIM)
    return jnp.transpose(a, (1, 2, 3, 0, 4)).reshape(n * N_KV * GQA * s_len, HEAD_DIM)


def _from_head_rows(a, n, s_len):
    a = a.reshape(n, N_KV, GQA, s_len, HEAD_DIM)
    return jnp.transpose(a, (3, 0, 1, 2, 4)).reshape(s_len * n, D_ATT)


def _sample_attention(q, k_new, v_new, ga, cache_k, cache_v, l, sinks, n, s_len, bn=8):
    assert s_len <= SUBLANES and n % bn == 0
    r = GQA * s_len
    depth = cache_k.shape[0]
    rows = bn * N_KV * r
    per_row = lambda a: jnp.broadcast_to(a.astype(F32).reshape(1, N_HEADS, 1, 1),
                                         (bn, N_HEADS, s_len, 1)).reshape(rows, 1)
    slope_rows = per_row(jnp.asarray(SLOPES, F32))
    sink_rows = per_row(sinks)
    rows_new = SUBLANES * N_KV

    def new_rows(a):
        a = jnp.transpose(a.reshape(s_len, n, N_KV, HEAD_DIM), (1, 0, 2, 3)).reshape(n, s_len * N_KV, HEAD_DIM)
        return jnp.pad(a, ((0, 0), (0, rows_new - s_len * N_KV), (0, 0)))

    win_rows = lambda c: c.reshape(depth * n, WINDOW * N_KV, HEAD_DIM)
    heads = pl.BlockSpec((rows, HEAD_DIM), lambda i: (i, 0))
    win_in = pl.BlockSpec((bn, WINDOW * N_KV, HEAD_DIM), lambda i: (l * (n // bn) + i, 0, 0))
    win_out = pl.BlockSpec((bn, WINDOW * N_KV, HEAD_DIM), lambda i: (i, 0, 0))
    new = pl.BlockSpec((bn, rows_new, HEAD_DIM), lambda i: (i, 0, 0))
    small = pl.BlockSpec((rows, 1), lambda i: (0, 0))
    win_shape = jax.ShapeDtypeStruct((n, WINDOW * N_KV, HEAD_DIM), F32)
    o, k_win, v_win = pl.pallas_call(
        functools.partial(_sample_attn_kernel, bn=bn, s_len=s_len),
        grid=(n // bn,),
        in_specs=[small, small, heads, win_in, new, win_in, new, heads],
        out_specs=[heads, win_out, win_out],
        out_shape=[jax.ShapeDtypeStruct((n * N_KV * r, HEAD_DIM), BF16), win_shape, win_shape],
        compiler_params=_params(1),
        name="sample_attention",
    )(slope_rows, sink_rows, _to_head_rows(q, n, s_len), win_rows(cache_k), new_rows(k_new),
      win_rows(cache_v), new_rows(v_new), _to_head_rows(ga, n, s_len))
    win5 = lambda a: a.reshape(n, WINDOW, N_KV, HEAD_DIM)
    return _from_head_rows(o, n, s_len), win5(k_win), win5(v_win)


def _sample_dwconv_kernel(st_ref, u_ref, cw_ref, cb_ref, y_in_ref, y_ref, so_ref, *, s_len, n):
    del y_in_ref
    for s in range(u_ref.shape[0]):
        lanes = pl.ds(s * LANES, LANES)
        ext = lambda j: st_ref[j, :, lanes] if j < HIST else u_ref[s, pl.ds((j - HIST) * n, n), :]
        for t in range(s_len):
            acc = jnp.broadcast_to(cb_ref[:, lanes], (n, LANES))
            for k in range(CONV_W):
                acc = acc + cw_ref[pl.ds(k, 1), lanes] * ext(t + k)
            y_ref[pl.ds(t * n, n), lanes] = acc
        for j in range(HIST):
            so_ref[j, :, lanes] = ext(j + s_len)


def _sample_dwconv(state_tm, l, u_slab, conv_w, conv_b, y_all, row0, tc=512):
    depth, _, n, c = state_tm.shape
    ms = u_slab.shape[1]
    s_len = ms // n
    assert row0 % ms == 0 and n % SUBLANES == 0
    y, new_state = pl.pallas_call(
        functools.partial(_sample_dwconv_kernel, s_len=s_len, n=n),
        grid=(c // tc,),
        in_specs=[
            pl.BlockSpec((None, HIST, n, tc), lambda j: (l, 0, 0, j)),
            pl.BlockSpec((tc // LANES, ms, LANES), lambda j: (j, 0, 0)),
            pl.BlockSpec((CONV_W, tc), lambda j: (0, j)),
            pl.BlockSpec((1, tc), lambda j: (0, j)),
            pl.BlockSpec(memory_space=pl.ANY),
        ],
        out_specs=[pl.BlockSpec((ms, tc), lambda j: (row0 // ms, j)),
                   pl.BlockSpec((HIST, n, tc), lambda j: (0, 0, j))],
        out_shape=[jax.ShapeDtypeStruct(y_all.shape, F32), jax.ShapeDtypeStruct((HIST, n, c), F32)],
        input_output_aliases={4: 0},
        compiler_params=_params(1),
        name="sample_dwconv",
    )(state_tm, u_slab, conv_w, conv_b.reshape(1, c), y_all)
    return y, new_state


def _ln_pw_kernel(y_ref, lng_ref, lnb_ref, wpw_ref, cg_ref, o_ref, wpb):
    @pl.when(pl.program_id(0) == 0)
    def _():
        wpb[...] = wpw_ref[...].astype(BF16)

    y = y_ref[...]
    mu = jnp.mean(y, axis=-1, keepdims=True)
    yc = y - mu
    var = jnp.mean(yc * yc, axis=-1, keepdims=True)
    yn = yc * lax.rsqrt(var + LN_EPS) * lng_ref[...] + lnb_ref[...]
    a = _silu(yn).astype(BF16)
    o = jnp.dot(a, wpb[...], preferred_element_type=F32)
    o_ref[...] = (o * cg_ref[...].astype(F32)).astype(o_ref.dtype)


def _ln_pw(y, cg, ln_g, ln_b, w_pw2, l, tm=512):
    m, c = y.shape
    depth = w_pw2.shape[0]
    assert m % tm == 0
    const = lambda i: (0, 0)
    return pl.pallas_call(
        _ln_pw_kernel,
        grid=(m // tm,),
        in_specs=[
            pl.BlockSpec((tm, c), lambda i: (i, 0)),
            pl.BlockSpec((1, c), const),
            pl.BlockSpec((1, c), const),
            pl.BlockSpec((c, c), lambda i: (l, 0), pipeline_mode=pl.Buffered(1)),
            pl.BlockSpec((tm, c), lambda i: (i, 0)),
        ],
        out_specs=pl.BlockSpec((tm, c), lambda i: (i, 0)),
        out_shape=jax.ShapeDtypeStruct((m, c), BF16),
        scratch_shapes=[pltpu.VMEM((c, c), BF16)],
        compiler_params=_params(1),
        name="ln_pw",
    )(y, ln_g.reshape(1, -1), ln_b.reshape(1, -1), w_pw2.reshape(depth * c, c), cg)


def _outproj_kernel(a_ref, c_ref, as_ref, cs_ref, w_ref, xp_ref, gp_ref, xs_ref, gs_ref, yp_ref, ys_ref, wb,
                    *, n_prompt_tiles, tiles_per_seq, n_batch, s_len):
    i = pl.program_id(1)

    @pl.when(i == 0)
    def _():
        wb[...] = w_ref[...].astype(BF16)

    def project(att_ref, conv_ref):
        o = jnp.dot(att_ref[...], wb[pl.ds(0, D_ATT), :], preferred_element_type=F32)
        return o + jnp.dot(conv_ref[...], wb[pl.ds(D_ATT, D_CONV), :], preferred_element_type=F32)

    @pl.when(i < n_prompt_tiles)
    def _():
        gate = _select_row(gp_ref, i // tiles_per_seq, n_batch)
        yp_ref[...] = xp_ref[...] + gate * project(a_ref, c_ref)

    @pl.when(i == n_prompt_tiles)
    def _():
        gate = jnp.concatenate([gs_ref[...]] * s_len, axis=0)
        ys_ref[...] = xs_ref[...] + gate * project(as_ref, cs_ref)


def _outproj(att, conv, w_out, l, xp, xs, mod, n_batch, seq, s_len, tn=512):
    mp, d = xp.shape
    ms = xs.shape[0]
    ns = ms // s_len
    depth = w_out.shape[0]
    assert mp % TM == 0 and seq % TM == 0 and mp % ms == 0 and ns % MOD_ROWS == 0
    npt = mp // TM
    gate_col0 = 2 * d // tn
    ptile = lambda j, i: (jnp.minimum(i, npt - 1), j)
    ptile_rows = lambda j, i: (jnp.minimum(i, npt - 1), 0)
    stile = lambda j, i: (mp // ms, 0)
    return pl.pallas_call(
        functools.partial(_outproj_kernel, n_prompt_tiles=npt, tiles_per_seq=seq // TM, n_batch=n_batch,
                          s_len=s_len),
        grid=(d // tn, npt + 1),
        in_specs=[
            pl.BlockSpec((TM, D_ATT), ptile_rows),
            pl.BlockSpec((TM, D_CONV), ptile_rows),
            pl.BlockSpec((ms, D_ATT), stile),
            pl.BlockSpec((ms, D_CONV), stile),
            pl.BlockSpec((D_ATT + D_CONV, tn), lambda j, i: (l, j)),
            pl.BlockSpec((TM, tn), ptile),
            pl.BlockSpec((MOD_ROWS, tn), lambda j, i: (ns // MOD_ROWS, gate_col0 + j)),
            pl.BlockSpec((ms, tn), lambda j, i: (0, j)),
            pl.BlockSpec((ns, tn), lambda j, i: (0, gate_col0 + j)),
        ],
        out_specs=[pl.BlockSpec((TM, tn), ptile), pl.BlockSpec((ms, tn), lambda j, i: (0, j))],
        out_shape=[jax.ShapeDtypeStruct((mp, d), F32), jax.ShapeDtypeStruct((ms, d), F32)],
        scratch_shapes=[pltpu.VMEM((D_ATT + D_CONV, tn), BF16)],
        compiler_params=_params(2),
        name="outproj",
    )(att, conv, att, conv, w_out.reshape(depth * (D_ATT + D_CONV), d), xp, mod, xs, mod)


def kernel(x_prompt, x_sample, c_prompt, c_sample, cache_k_win, cache_v_win, state_conv, w_ada, b_ada,
           norm_g, w_in, q_norm_g, k_norm_g, sinks, conv_w, conv_b, ln_g, ln_b, w_pw2, w_out):
    depth, d, n_in = w_in.shape
    nb, seq, _ = x_prompt.shape
    ns, s_len, _ = x_sample.shape
    mp, ms = nb * seq, ns * s_len
    m_all = mp + ms
    tm_in = m_all // 8
    assert tm_in * 8 == m_all and tm_in % 16 == 0

    xp = x_prompt.reshape(mp, d)
    xs = jnp.transpose(x_sample, (1, 0, 2)).reshape(ms, d)
    pad = (-(ns + nb)) % 16
    c_all = jnp.concatenate([c_sample, c_prompt, jnp.zeros((pad, d), F32)], axis=0)
    w_in2d = w_in.reshape(depth * d, n_in)
    state_tm = jnp.transpose(state_conv, (0, 2, 1, 3))

    kp_l, vp_l, cp_l, ks_l, vs_l, cs_l = [], [], [], [], [], []
    for l in range(depth):
        mod = _modulation(c_all, w_ada, b_ada, l)

        h = _prenorm(xp, norm_g[l], mod, ns, nb, 512, m_all, 0)
        h = _prenorm(xs, norm_g[l], mod, 0, None, ns, m_all, mp, h_all=h)

        qg = q_norm_g[l].reshape(1, HEAD_DIM)
        kg = k_norm_g[l].reshape(1, HEAD_DIM)
        (q,) = _proj_call(_proj_qnorm_kernel, h, w_in2d, l, [OFF_Q], D_ATT, [qg], [BF16], tm_in, 512, "proj_q")
        k, v = _proj_call(_proj_kv_kernel, h, w_in2d, l, [OFF_K, OFF_V], KV_W, [kg], [F32, F32], tm_in, 256,
                          "proj_kv")
        (ga,) = _proj_call(_proj_silu_kernel, h, w_in2d, l, [OFF_GA], D_ATT, [], [BF16], tm_in, 512, "proj_ga")
        (cg,) = _proj_call(_proj_silu_kernel, h, w_in2d, l, [OFF_CG], D_CONV, [], [BF16], tm_in, 512, "proj_cg")
        y, u_s, u_tail = _proj_glu_conv(h, w_in2d, l, conv_w[l], conv_b[l], nb, seq, ms)

        att = _prompt_attention(q, k, v, ga, sinks[l], nb, seq)
        att_s, k_win, v_win = _sample_attention(q[mp:], k[mp:], v[mp:], ga[mp:], cache_k_win, cache_v_win,
                                                l, sinks[l], ns, s_len)
        att = lax.dynamic_update_slice(att, att_s, (mp, 0))

        y, conv_state = _sample_dwconv(state_tm, l, u_s, conv_w[l], conv_b[l], y, mp)
        conv = _ln_pw(y, cg, ln_g[l], ln_b[l], w_pw2, l)

        tail = lambda a, rows: jnp.stack(
            [lax.slice_in_dim(a, (b + 1) * seq - rows, (b + 1) * seq, axis=0) for b in range(nb)])
        kp_l.append(tail(k, WINDOW).reshape(nb, WINDOW, N_KV, HEAD_DIM))
        vp_l.append(tail(v, WINDOW).reshape(nb, WINDOW, N_KV, HEAD_DIM))
        cp_l.append(jnp.transpose(u_tail[:, :, HALO - HIST:, :], (0, 2, 1, 3)).reshape(nb, HIST, D_CONV))
        ks_l.append(k_win)
        vs_l.append(v_win)
        cs_l.append(jnp.transpose(conv_state, (1, 0, 2)))

        xp, xs = _outproj(att, conv, w_out, l, xp, xs, mod, nb, seq, s_len)

    ys = jnp.transpose(xs.reshape(s_len, ns, d), (1, 0, 2))
    return (xp.reshape(nb, seq, d), ys,
            jnp.stack(kp_l), jnp.stack(vp_l), jnp.stack(cp_l),
            jnp.stack(ks_l), jnp.stack(vs_l), jnp.stack(cs_l))
```

```python
import functools

import numpy as np
import jax
import jax.numpy as jnp
from jax import lax
from jax.experimental import pallas as pl
from jax.experimental.pallas import tpu as pltpu

F32 = jnp.float32
BF16 = jnp.bfloat16

D_MODEL = 4096
D_ATT = 2048
D_CONV = 2048
HEAD_DIM = 128
N_HEADS = 16
N_KV = 4
GQA = 4
KV_W = N_KV * HEAD_DIM
WINDOW = 128
CONV_W = 31
HIST = CONV_W - 1
LANES = 128
SUBLANES = 8
HALO = 32
N_SLAB = D_CONV // LANES
RMS_EPS = 1e-6
LN_EPS = 1e-5
NEG_INF = -1e30
ATT_SCALE = HEAD_DIM ** -0.5
LOG2E = float(np.log2(np.e))
OFF_Q, OFF_K, OFF_V, OFF_GA, OFF_CA, OFF_CB, OFF_CG = 0, 2048, 2560, 3072, 5120, 7168, 9216
SLOPES = [float(np.float32(2.0 ** (-8.0 * (h + 1) / N_HEADS))) for h in range(N_HEADS)]

VMEM_LIMIT = 58 * 1024 * 1024
TM = 1024
MOD_ROWS = 8


def _params(n_axes):
    return pltpu.CompilerParams(dimension_semantics=("arbitrary",) * n_axes,
                                vmem_limit_bytes=VMEM_LIMIT)


def _silu(x):
    return x * jax.nn.sigmoid(x)


def _select_row(ref, idx, count):
    row = ref[0:1, :]
    for n in range(1, count):
        row = jnp.where(idx == n, ref[n:n + 1, :], row)
    return row


def _mod_kernel(c_ref, w_ref, b_ref, o_ref, act):
    @pl.when(pl.program_id(0) == 0)
    def _():
        act[...] = _silu(c_ref[...]).astype(BF16)

    o_ref[...] = jnp.dot(act[...], w_ref[...].astype(BF16), preferred_element_type=F32) + b_ref[...]


def _modulation(c_all, w_ada, b_ada, l, tn=512):
    r, d = c_all.shape
    depth, _, n = w_ada.shape
    return pl.pallas_call(
        _mod_kernel,
        grid=(n // tn,),
        in_specs=[
            pl.BlockSpec((r, d), lambda j: (0, 0)),
            pl.BlockSpec((d, tn), lambda j: (l, j)),
            pl.BlockSpec((None, 1, tn), lambda j: (l, 0, j)),
        ],
        out_specs=pl.BlockSpec((r, tn), lambda j: (0, j)),
        out_shape=jax.ShapeDtypeStruct((r, n), F32),
        scratch_shapes=[pltpu.VMEM((r, d), BF16)],
        compiler_params=_params(1),
        name="modulation",
    )(c_all, w_ada.reshape(depth * d, n), b_ada.reshape(depth, 1, n))


def _headnorm(a, g):
    return a * lax.rsqrt(jnp.mean(a * a, axis=-1, keepdims=True) + RMS_EPS) * g


def _prenorm_kv_kernel(x_ref, g_ref, scale_ref, shift_ref, wk_ref, wv_ref, kg_ref, *rest,
                       tiles_per_batch, n_batch, rows_per_mod):
    h_ref, k_ref, v_ref, wkb, wvb = rest[-5:]

    @pl.when(pl.program_id(0) == 0)
    def _():
        wkb[...] = wk_ref[...].astype(BF16)
        wvb[...] = wv_ref[...].astype(BF16)

    x = x_ref[...]
    y = x * lax.rsqrt(jnp.mean(x * x, axis=-1, keepdims=True) + RMS_EPS)
    y = y * g_ref[...]
    if tiles_per_batch is None:
        reps = x.shape[0] // rows_per_mod
        scale = jnp.concatenate([scale_ref[...]] * reps, axis=0)
        shift = jnp.concatenate([shift_ref[...]] * reps, axis=0)
    else:
        batch = pl.program_id(0) // tiles_per_batch
        scale = _select_row(scale_ref, batch, n_batch)
        shift = _select_row(shift_ref, batch, n_batch)
    h = (y * (1.0 + scale) + shift).astype(BF16)
    h_ref[...] = h
    acc = jnp.dot(h, wkb[...], preferred_element_type=F32)
    kg = kg_ref[...]
    for c in range(N_KV):
        sl = slice(c * HEAD_DIM, (c + 1) * HEAD_DIM)
        k_ref[:, sl] = _headnorm(acc[:, sl], kg)
    v_ref[...] = jnp.dot(h, wvb[...], preferred_element_type=F32)


def _prenorm_kv(x2d, g, mod, mod_row0, n_batch, rows_per_mod, w2d, l, kg, tm, m_total, row0, prev=None):
    m, d = x2d.shape
    if n_batch is None:
        assert mod_row0 % rows_per_mod == 0 and tm % rows_per_mod == 0
        mk = lambda c: pl.BlockSpec((rows_per_mod, d), lambda i: (mod_row0 // rows_per_mod, c))
        tiles_per_batch = None
    else:
        assert mod_row0 % MOD_ROWS == 0 and n_batch <= MOD_ROWS and (m // n_batch) % tm == 0
        mk = lambda c: pl.BlockSpec((MOD_ROWS, d), lambda i: (mod_row0 // MOD_ROWS, c))
        tiles_per_batch = m // n_batch // tm
    wspec = lambda off: pl.BlockSpec((d, KV_W), lambda i: (l, off // KV_W), pipeline_mode=pl.Buffered(1))
    in_specs = [pl.BlockSpec((tm, d), lambda i: (i, 0)), pl.BlockSpec((1, d), lambda i: (0, 0)), mk(1), mk(0),
                wspec(OFF_K), wspec(OFF_V), pl.BlockSpec((1, HEAD_DIM), lambda i: (0, 0))]
    args = [x2d, g.reshape(1, d), mod, mod, w2d, w2d, kg]
    aliases = {}
    if prev is not None:
        aliases = {len(args) + n: n for n in range(3)}
        in_specs += [pl.BlockSpec(memory_space=pl.ANY)] * 3
        args += list(prev)
    assert row0 % tm == 0 and m % tm == 0
    rows = lambda i: (row0 // tm + i, 0)
    return pl.pallas_call(
        functools.partial(_prenorm_kv_kernel, tiles_per_batch=tiles_per_batch, n_batch=n_batch,
                          rows_per_mod=rows_per_mod),
        grid=(m // tm,),
        in_specs=in_specs,
        out_specs=[pl.BlockSpec((tm, d), rows), pl.BlockSpec((tm, KV_W), rows), pl.BlockSpec((tm, KV_W), rows)],
        out_shape=[jax.ShapeDtypeStruct((m_total, d), BF16), jax.ShapeDtypeStruct((m_total, KV_W), F32),
                   jax.ShapeDtypeStruct((m_total, KV_W), F32)],
        scratch_shapes=[pltpu.VMEM((d, KV_W), BF16), pltpu.VMEM((d, KV_W), BF16)],
        input_output_aliases=aliases,
        compiler_params=_params(1),
        name="prenorm_kv",
    )(*args)


def _cast_weights(w_refs, wb_refs):
    @pl.when(pl.program_id(1) == 0)
    def _():
        for w_ref, wb_ref in zip(w_refs, wb_refs):
            wb_ref[...] = w_ref[...].astype(BF16)


def _proj_qnorm_kernel(h_ref, w_ref, g_ref, o_ref, wb):
    _cast_weights([w_ref], [wb])
    acc = jnp.dot(h_ref[...], wb[...], preferred_element_type=F32)
    g = g_ref[...]
    for c in range(acc.shape[1] // HEAD_DIM):
        sl = slice(c * HEAD_DIM, (c + 1) * HEAD_DIM)
        o_ref[:, sl] = _headnorm(acc[:, sl], g).astype(o_ref.dtype)


def _proj_silu_kernel(h_ref, w_ref, o_ref, wb):
    _cast_weights([w_ref], [wb])
    acc = jnp.dot(h_ref[...], wb[...], preferred_element_type=F32)
    o_ref[...] = _silu(acc).astype(o_ref.dtype)


def _proj_call(kernel, h, w2d, l, col_offs, width, extra, out_dtypes, tm, tn, name):
    m, d = h.shape
    assert m % tm == 0
    w_specs = [
        pl.BlockSpec((d, tn), functools.partial(lambda j, i, o: (l, o + j), o=off // tn))
        for off in col_offs
    ]
    extra_specs = [pl.BlockSpec(e.shape, lambda j, i: (0, 0)) for e in extra]
    return pl.pallas_call(
        kernel,
        grid=(width // tn, m // tm),
        in_specs=[pl.BlockSpec((tm, d), lambda j, i: (i, 0))] + w_specs + extra_specs,
        out_specs=[pl.BlockSpec((tm, tn), lambda j, i: (i, j))] * len(out_dtypes),
        out_shape=[jax.ShapeDtypeStruct((m, width), dt) for dt in out_dtypes],
        scratch_shapes=[pltpu.VMEM((d, tn), BF16) for _ in col_offs],
        compiler_params=_params(2),
        name=name,
    )(h, *([w2d] * len(col_offs)), *extra)


def _proj_glu_conv_kernel(h_ref, hs_ref, wa_ref, wb_ref, cw_ref, cb_ref, y_ref, us_ref, ut_ref,
                          wab, wbb, ubuf0, ubuf1, *, tm, rc, tiles_per_seq, n_prompt_tiles):
    j, i = pl.program_id(0), pl.program_id(1)
    nsl = y_ref.shape[1] // LANES
    base = HALO - HIST

    @pl.when(i == 0)
    def _():
        wab[...] = wa_ref[...].astype(BF16)
        wbb[...] = wb_ref[...].astype(BF16)

    @pl.when((i == 0) & (j == 0))
    def _():
        ubuf1[...] = jnp.zeros(ubuf1.shape, F32)

    def conv(prv):
        for c in range(nsl):
            bias = jnp.broadcast_to(cb_ref[c], (rc, LANES))
            for r0 in range(0, tm, rc):
                acc = bias
                for k in range(CONV_W):
                    acc = acc + cw_ref[c, pl.ds(k, 1), :] * prv[c, pl.ds(r0 + base + k, rc), :]
                y_ref[pl.ds(r0, rc), pl.ds(c * LANES, LANES)] = acc

    def glu(h):
        a = jnp.dot(h, wab[...], preferred_element_type=F32)
        b = jnp.dot(h, wbb[...], preferred_element_type=F32)
        return a * jax.nn.sigmoid(b)

    def prompt_step(cur, prv):
        conv(prv)
        u = glu(h_ref[...])
        seq_start = lax.rem(i, tiles_per_seq) == 0
        for c in range(nsl):
            cur[c, pl.ds(0, HALO), :] = jnp.where(seq_start, 0.0, prv[c, pl.ds(tm, HALO), :])
            cur[c, pl.ds(HALO, tm), :] = u[:, c * LANES:(c + 1) * LANES]

        @pl.when(lax.rem(i, tiles_per_seq) == tiles_per_seq - 1)
        def _():
            for c in range(nsl):
                ut_ref[c] = cur[c, pl.ds(tm, HALO), :]

    def sample_step(prv):
        conv(prv)
        u = glu(hs_ref[...])
        for c in range(nsl):
            us_ref[c] = u[:, c * LANES:(c + 1) * LANES]

    is_prompt = i < n_prompt_tiles
    pl.when(is_prompt & (lax.rem(i, 2) == 0))(lambda: prompt_step(ubuf0, ubuf1))
    pl.when(is_prompt & (lax.rem(i, 2) == 1))(lambda: prompt_step(ubuf1, ubuf0))
    pl.when(i == n_prompt_tiles)(lambda: sample_step(ubuf1 if n_prompt_tiles % 2 == 0 else ubuf0))


def _slab_weights(conv_w, conv_b):
    cw = jnp.pad(conv_w, ((0, HALO - CONV_W), (0, 0)))
    cw = jnp.transpose(cw.reshape(HALO, N_SLAB, LANES), (1, 0, 2))
    return cw, conv_b.reshape(N_SLAB, 1, LANES)


def _proj_glu_conv(h, w2d, l, conv_w, conv_b, n_batch, seq, ms, tn=256):
    m, d = h.shape
    mp = n_batch * seq
    assert seq % TM == 0 and mp % ms == 0 and m == mp + ms
    npt = mp // TM
    assert npt % 2 == 0
    nsl = tn // LANES
    cw, cb = _slab_weights(conv_w, conv_b)
    wspec = lambda off: pl.BlockSpec((d, tn), lambda j, i: (l, off // tn + j))
    return pl.pallas_call(
        functools.partial(_proj_glu_conv_kernel, tm=TM, rc=32, tiles_per_seq=seq // TM, n_prompt_tiles=npt),
        grid=(D_CONV // tn, npt + 1),
        in_specs=[
            pl.BlockSpec((TM, d), lambda j, i: (jnp.minimum(i, npt - 1), 0)),
            pl.BlockSpec((ms, d), lambda j, i: (mp // ms, 0)),
            wspec(OFF_CA),
            wspec(OFF_CB),
            pl.BlockSpec((nsl, HALO, LANES), lambda j, i: (j, 0, 0)),
            pl.BlockSpec((nsl, 1, LANES), lambda j, i: (j, 0, 0)),
        ],
        out_specs=[
            pl.BlockSpec((TM, tn), lambda j, i: (jnp.maximum(i - 1, 0), j)),
            pl.BlockSpec((nsl, ms, LANES), lambda j, i: (j, 0, 0)),
            pl.BlockSpec((None, nsl, HALO, LANES),
                         lambda j, i: (jnp.minimum(i // (seq // TM), n_batch - 1), j, 0, 0)),
        ],
        out_shape=[
            jax.ShapeDtypeStruct((m, D_CONV), F32),
            jax.ShapeDtypeStruct((N_SLAB, ms, LANES), F32),
            jax.ShapeDtypeStruct((n_batch, N_SLAB, HALO, LANES), F32),
        ],
        scratch_shapes=[pltpu.VMEM((d, tn), BF16), pltpu.VMEM((d, tn), BF16),
                        pltpu.VMEM((nsl, HALO + TM, LANES), F32), pltpu.VMEM((nsl, HALO + TM, LANES), F32)],
        compiler_params=_params(2),
        name="proj_glu_conv",
    )(h, h, w2d, w2d, cw, cb)


def _prompt_attn_kernel(sinks_ref, q_ref, kp_ref, kc_ref, vp_ref, vc_ref, ga_ref, o_ref, bias, *, bq):
    qi = lax.broadcasted_iota(jnp.int32, (WINDOW, 2 * WINDOW), 0)
    kj = lax.broadcasted_iota(jnp.int32, (WINDOW, 2 * WINDOW), 1)

    @pl.when((pl.program_id(0) == 0) & (pl.program_id(1) == 0))
    def _():
        dist = WINDOW + qi - kj
        in_band = (dist >= 0) & (dist < WINDOW)
        distf = dist.astype(F32)
        for h in range(N_HEADS):
            bias[h] = jnp.where(in_band, (-SLOPES[h] * LOG2E) * distf, NEG_INF)

    has_prev = kj >= WINDOW * (pl.program_id(1) == 0).astype(jnp.int32)
    nsub = bq // WINDOW
    k_all = jnp.concatenate([kp_ref[...], kc_ref[...]], axis=0).astype(BF16)
    v_all = jnp.concatenate([vp_ref[...], vc_ref[...]], axis=0).astype(BF16)
    for s in range(nsub):
        rows = slice(s * WINDOW, (s + 1) * WINDOW)
        for kv in range(N_KV):
            cols = slice(kv * HEAD_DIM, (kv + 1) * HEAD_DIM)
            kb = k_all[s * WINDOW:(s + 2) * WINDOW, cols]
            vb = v_all[s * WINDOW:(s + 2) * WINDOW, cols]
            for g in range(GQA):
                h = kv * GQA + g
                hc = slice(h * HEAD_DIM, (h + 1) * HEAD_DIM)
                sc = lax.dot_general(q_ref[rows, hc], kb, (((1,), (1,)), ((), ())),
                                     preferred_element_type=F32)
                t = sc * (ATT_SCALE * LOG2E) + bias[h]
                if s == 0:
                    t = jnp.where(has_prev, t, NEG_INF)
                sink = sinks_ref[h] * LOG2E
                m = jnp.maximum(jnp.max(t, axis=-1, keepdims=True), sink)
                p = jnp.exp2(t - m)
                denom = jnp.sum(p, axis=-1, keepdims=True) + jnp.exp2(sink - m)
                o = jnp.dot(p.astype(BF16), vb, preferred_element_type=F32) / denom
                o_ref[rows, hc] = (o * ga_ref[rows, hc].astype(F32)).astype(o_ref.dtype)


def _prompt_attention(q, k, v, ga, sinks, n_batch, seq, bq=256):
    m = n_batch * seq
    nb = seq // bq
    ratio = bq // WINDOW
    cur = lambda n, i, s: (n * nb + i, 0)
    prev = lambda n, i, s: (jnp.maximum((n * nb + i) * ratio - 1, 0), 0)
    return pl.pallas_call(
        functools.partial(_prompt_attn_kernel, bq=bq),
        grid_spec=pltpu.PrefetchScalarGridSpec(
            num_scalar_prefetch=1,
            grid=(n_batch, nb),
            in_specs=[
                pl.BlockSpec((bq, D_ATT), cur),
                pl.BlockSpec((WINDOW, KV_W), prev),
                pl.BlockSpec((bq, KV_W), cur),
                pl.BlockSpec((WINDOW, KV_W), prev),
                pl.BlockSpec((bq, KV_W), cur),
                pl.BlockSpec((bq, D_ATT), cur),
            ],
            out_specs=pl.BlockSpec((bq, D_ATT), cur),
            scratch_shapes=[pltpu.VMEM((N_HEADS, WINDOW, 2 * WINDOW), F32)],
        ),
        out_shape=jax.ShapeDtypeStruct((m, D_ATT), BF16),
        compiler_params=_params(2),
        name="prompt_attention",
    )(sinks, q, k, k, v, v, ga)


def _sample_attn_kernel(slope_ref, sink_ref, q_ref, kc_ref, kn_ref, vc_ref, vn_ref, ga_ref,
                        o_ref, ko_ref, vo_ref, *, bn, s_len):
    nk = WINDOW + SUBLANES
    r = GQA * s_len
    rows = bn * N_KV * r
    ri = lax.broadcasted_iota(jnp.int32, (rows, nk), 0)
    kj = lax.broadcasted_iota(jnp.int32, (rows, nk), 1)
    dist = lax.rem(ri, s_len) + WINDOW - kj
    valid = (dist >= 0) & (dist < WINDOW)
    shift = s_len * N_KV
    keep = WINDOW * N_KV - shift

    def head_rows(win_ref, new_ref, b, kv):
        win = win_ref.at[b][pl.ds(kv, WINDOW, stride=N_KV), :]
        new = new_ref.at[b][pl.ds(kv, SUBLANES, stride=N_KV), :]
        return jnp.concatenate([win, new], axis=0).astype(BF16)

    chains = [(b, kv) for b in range(bn) for kv in range(N_KV)]
    sc = jnp.concatenate(
        [lax.dot_general(q_ref[pl.ds(c * r, r), :], head_rows(kc_ref, kn_ref, b, kv),
                         (((1,), (1,)), ((), ())), preferred_element_type=F32)
         for c, (b, kv) in enumerate(chains)], axis=0)
    sc = sc * ATT_SCALE - slope_ref[...] * dist.astype(F32)
    sc = jnp.where(valid, sc, NEG_INF)
    sink = sink_ref[...]
    m = jnp.maximum(jnp.max(sc, axis=-1, keepdims=True), sink)
    p = jnp.exp(sc - m)
    denom = jnp.sum(p, axis=-1, keepdims=True) + jnp.exp(sink - m)
    p = p.astype(BF16)
    o = jnp.concatenate(
        [jnp.dot(p[c * r:(c + 1) * r], head_rows(vc_ref, vn_ref, b, kv), preferred_element_type=F32)
         for c, (b, kv) in enumerate(chains)], axis=0)
    o_ref[...] = (o / denom * ga_ref[...].astype(F32)).astype(o_ref.dtype)

    for b in range(bn):
        for win_ref, new_ref, out_ref in ((kc_ref, kn_ref, ko_ref), (vc_ref, vn_ref, vo_ref)):
            out_ref[b, pl.ds(0, keep), :] = win_ref[b, pl.ds(shift, keep), :]
            out_ref[b, pl.ds(keep, shift), :] = new_ref[b, pl.ds(0, shift), :]


def _to_head_rows(a, n, s_len):
    a = a.reshape(s_len, n, N_KV, GQA, HEAD_DIM)
    return jnp.transpose(a, (1, 2, 3, 0, 4)).reshape(n * N_KV * GQA * s_len, HEAD_DIM)


def _from_head_rows(a, n, s_len):
    a = a.reshape(n, N_KV, GQA, s_len, HEAD_DIM)
    return jnp.transpose(a, (3, 0, 1, 2, 4)).reshape(s_len * n, D_ATT)


def _sample_attention(q, k_new, v_new, ga, cache_k, cache_v, l, sinks, n, s_len, bn=8):
    assert s_len <= SUBLANES and n % bn == 0
    r = GQA * s_len
    depth = cache_k.shape[0]
    rows = bn * N_KV * r
    per_row = lambda a: jnp.broadcast_to(a.astype(F32).reshape(1, N_HEADS, 1, 1),
                                         (bn, N_HEADS, s_len, 1)).reshape(rows, 1)
    slope_rows = per_row(jnp.asarray(SLOPES, F32))
    sink_rows = per_row(sinks)
    rows_new = SUBLANES * N_KV

    def new_rows(a):
        a = jnp.transpose(a.reshape(s_len, n, N_KV, HEAD_DIM), (1, 0, 2, 3)).reshape(n, s_len * N_KV, HEAD_DIM)
        return jnp.pad(a, ((0, 0), (0, rows_new - s_len * N_KV), (0, 0)))

    win_rows = lambda c: c.reshape(depth * n, WINDOW * N_KV, HEAD_DIM)
    heads = pl.BlockSpec((rows, HEAD_DIM), lambda i: (i, 0))
    win_in = pl.BlockSpec((bn, WINDOW * N_KV, HEAD_DIM), lambda i: (l * (n // bn) + i, 0, 0))
    win_out = pl.BlockSpec((bn, WINDOW * N_KV, HEAD_DIM), lambda i: (i, 0, 0))
    new = pl.BlockSpec((bn, rows_new, HEAD_DIM), lambda i: (i, 0, 0))
    small = pl.BlockSpec((rows, 1), lambda i: (0, 0))
    win_shape = jax.ShapeDtypeStruct((n, WINDOW * N_KV, HEAD_DIM), F32)
    o, k_win, v_win = pl.pallas_call(
        functools.partial(_sample_attn_kernel, bn=bn, s_len=s_len),
        grid=(n // bn,),
        in_specs=[small, small, heads, win_in, new, win_in, new, heads],
        out_specs=[heads, win_out, win_out],
        out_shape=[jax.ShapeDtypeStruct((n * N_KV * r, HEAD_DIM), BF16), win_shape, win_shape],
        compiler_params=_params(1),
        name="sample_attention",
    )(slope_rows, sink_rows, _to_head_rows(q, n, s_len), win_rows(cache_k), new_rows(k_new),
      win_rows(cache_v), new_rows(v_new), _to_head_rows(ga, n, s_len))
    win5 = lambda a: a.reshape(n, WINDOW, N_KV, HEAD_DIM)
    return _from_head_rows(o, n, s_len), win5(k_win), win5(v_win)


def _sample_dwconv_kernel(st_ref, u_ref, cw_ref, cb_ref, y_in_ref, y_ref, so_ref, *, s_len, n):
    del y_in_ref
    for s in range(u_ref.shape[0]):
        lanes = pl.ds(s * LANES, LANES)
        ext = lambda j: st_ref[j, :, lanes] if j < HIST else u_ref[s, pl.ds((j - HIST) * n, n), :]
        for t in range(s_len):
            acc = jnp.broadcast_to(cb_ref[:, lanes], (n, LANES))
            for k in range(CONV_W):
                acc = acc + cw_ref[pl.ds(k, 1), lanes] * ext(t + k)
            y_ref[pl.ds(t * n, n), lanes] = acc
        for j in range(HIST):
            so_ref[j, :, lanes] = ext(j + s_len)


def _sample_dwconv(state_tm, l, u_slab, conv_w, conv_b, y_all, row0, tc=512):
    depth, _, n, c = state_tm.shape
    ms = u_slab.shape[1]
    s_len = ms // n
    assert row0 % ms == 0 and n % SUBLANES == 0
    y, new_state = pl.pallas_call(
        functools.partial(_sample_dwconv_kernel, s_len=s_len, n=n),
        grid=(c // tc,),
        in_specs=[
            pl.BlockSpec((None, HIST, n, tc), lambda j: (l, 0, 0, j)),
            pl.BlockSpec((tc // LANES, ms, LANES), lambda j: (j, 0, 0)),
            pl.BlockSpec((CONV_W, tc), lambda j: (0, j)),
            pl.BlockSpec((1, tc), lambda j: (0, j)),
            pl.BlockSpec(memory_space=pl.ANY),
        ],
        out_specs=[pl.BlockSpec((ms, tc), lambda j: (row0 // ms, j)),
                   pl.BlockSpec((HIST, n, tc), lambda j: (0, 0, j))],
        out_shape=[jax.ShapeDtypeStruct(y_all.shape, F32), jax.ShapeDtypeStruct((HIST, n, c), F32)],
        input_output_aliases={4: 0},
        compiler_params=_params(1),
        name="sample_dwconv",
    )(state_tm, u_slab, conv_w, conv_b.reshape(1, c), y_all)
    return y, new_state


def _ln_pw_kernel(y_ref, lng_ref, lnb_ref, wpw_ref, cg_ref, o_ref, wpb):
    @pl.when(pl.program_id(0) == 0)
    def _():
        wpb[...] = wpw_ref[...].astype(BF16)

    y = y_ref[...]
    mu = jnp.mean(y, axis=-1, keepdims=True)
    yc = y - mu
    var = jnp.mean(yc * yc, axis=-1, keepdims=True)
    yn = yc * lax.rsqrt(var + LN_EPS) * lng_ref[...] + lnb_ref[...]
    a = _silu(yn).astype(BF16)
    o = jnp.dot(a, wpb[...], preferred_element_type=F32)
    o_ref[...] = (o * cg_ref[...].astype(F32)).astype(o_ref.dtype)


def _ln_pw(y, cg, ln_g, ln_b, w_pw2, l, tm=512):
    m, c = y.shape
    depth = w_pw2.shape[0]
    assert m % tm == 0
    const = lambda i: (0, 0)
    return pl.pallas_call(
        _ln_pw_kernel,
        grid=(m // tm,),
        in_specs=[
            pl.BlockSpec((tm, c), lambda i: (i, 0)),
            pl.BlockSpec((1, c), const),
            pl.BlockSpec((1, c), const),
            pl.BlockSpec((c, c), lambda i: (l, 0), pipeline_mode=pl.Buffered(1)),
            pl.BlockSpec((tm, c), lambda i: (i, 0)),
        ],
        out_specs=pl.BlockSpec((tm, c), lambda i: (i, 0)),
        out_shape=jax.ShapeDtypeStruct((m, c), BF16),
        scratch_shapes=[pltpu.VMEM((c, c), BF16)],
        compiler_params=_params(1),
        name="ln_pw",
    )(y, ln_g.reshape(1, -1), ln_b.reshape(1, -1), w_pw2.reshape(depth * c, c), cg)


def _outproj_kernel(a_ref, c_ref, as_ref, cs_ref, w_ref, xp_ref, gp_ref, xs_ref, gs_ref, yp_ref, ys_ref, wb,
                    *, n_prompt_tiles, tiles_per_seq, n_batch, s_len):
    i = pl.program_id(1)

    @pl.when(i == 0)
    def _():
        wb[...] = w_ref[...].astype(BF16)

    def project(att_ref, conv_ref):
        o = jnp.dot(att_ref[...], wb[pl.ds(0, D_ATT), :], preferred_element_type=F32)
        return o + jnp.dot(conv_ref[...], wb[pl.ds(D_ATT, D_CONV), :], preferred_element_type=F32)

    @pl.when(i < n_prompt_tiles)
    def _():
        gate = _select_row(gp_ref, i // tiles_per_seq, n_batch)
        yp_ref[...] = xp_ref[...] + gate * project(a_ref, c_ref)

    @pl.when(i == n_prompt_tiles)
    def _():
        gate = jnp.concatenate([gs_ref[...]] * s_len, axis=0)
        ys_ref[...] = xs_ref[...] + gate * project(as_ref, cs_ref)


def _outproj(att_p, att_s, conv, w_out, l, xp, xs, mod, n_batch, seq, s_len, tn=512):
    mp, d = xp.shape
    ms = xs.shape[0]
    ns = ms // s_len
    depth = w_out.shape[0]
    assert mp % TM == 0 and seq % TM == 0 and mp % ms == 0 and ns % MOD_ROWS == 0
    npt = mp // TM
    gate_col0 = 2 * d // tn
    ptile = lambda j, i: (jnp.minimum(i, npt - 1), j)
    ptile_rows = lambda j, i: (jnp.minimum(i, npt - 1), 0)
    stile = lambda j, i: (mp // ms, 0)
    return pl.pallas_call(
        functools.partial(_outproj_kernel, n_prompt_tiles=npt, tiles_per_seq=seq // TM, n_batch=n_batch,
                          s_len=s_len),
        grid=(d // tn, npt + 1),
        in_specs=[
            pl.BlockSpec((TM, D_ATT), ptile_rows),
            pl.BlockSpec((TM, D_CONV), ptile_rows),
            pl.BlockSpec((ms, D_ATT), lambda j, i: (0, 0)),
            pl.BlockSpec((ms, D_CONV), stile),
            pl.BlockSpec((D_ATT + D_CONV, tn), lambda j, i: (l, j)),
            pl.BlockSpec((TM, tn), ptile),
            pl.BlockSpec((MOD_ROWS, tn), lambda j, i: (ns // MOD_ROWS, gate_col0 + j)),
            pl.BlockSpec((ms, tn), lambda j, i: (0, j)),
            pl.BlockSpec((ns, tn), lambda j, i: (0, gate_col0 + j)),
        ],
        out_specs=[pl.BlockSpec((TM, tn), ptile), pl.BlockSpec((ms, tn), lambda j, i: (0, j))],
        out_shape=[jax.ShapeDtypeStruct((mp, d), F32), jax.ShapeDtypeStruct((ms, d), F32)],
        scratch_shapes=[pltpu.VMEM((D_ATT + D_CONV, tn), BF16)],
        compiler_params=_params(2),
        name="outproj",
    )(att_p, conv, att_s, conv, w_out.reshape(depth * (D_ATT + D_CONV), d), xp, mod, xs, mod)


def kernel(x_prompt, x_sample, c_prompt, c_sample, cache_k_win, cache_v_win, state_conv, w_ada, b_ada,
           norm_g, w_in, q_norm_g, k_norm_g, sinks, conv_w, conv_b, ln_g, ln_b, w_pw2, w_out):
    depth, d, n_in = w_in.shape
    nb, seq, _ = x_prompt.shape
    ns, s_len, _ = x_sample.shape
    mp, ms = nb * seq, ns * s_len
    m_all = mp + ms
    tm_in = m_all // 8
    assert tm_in * 8 == m_all and tm_in % 16 == 0

    xp = x_prompt.reshape(mp, d)
    xs = jnp.transpose(x_sample, (1, 0, 2)).reshape(ms, d)
    pad = (-(ns + nb)) % 16
    c_all = jnp.concatenate([c_sample, c_prompt, jnp.zeros((pad, d), F32)], axis=0)
    w_in2d = w_in.reshape(depth * d, n_in)
    state_tm = jnp.transpose(state_conv, (0, 2, 1, 3))

    kp_l, vp_l, cp_l, ks_l, vs_l, cs_l = [], [], [], [], [], []
    for l in range(depth):
        mod = _modulation(c_all, w_ada, b_ada, l)

        qg = q_norm_g[l].reshape(1, HEAD_DIM)
        kg = k_norm_g[l].reshape(1, HEAD_DIM)
        hkv = _prenorm_kv(xp, norm_g[l], mod, ns, nb, None, w_in2d, l, kg, 512, m_all, 0)
        h, k, v = _prenorm_kv(xs, norm_g[l], mod, 0, None, ns, w_in2d, l, kg, ms, m_all, mp, prev=hkv)
        (q,) = _proj_call(_proj_qnorm_kernel, h, w_in2d, l, [OFF_Q], D_ATT, [qg], [BF16], tm_in, 512, "proj_q")
        (ga,) = _proj_call(_proj_silu_kernel, h, w_in2d, l, [OFF_GA], D_ATT, [], [BF16], tm_in, 512, "proj_ga")
        (cg,) = _proj_call(_proj_silu_kernel, h, w_in2d, l, [OFF_CG], D_CONV, [], [BF16], tm_in, 512, "proj_cg")
        y, u_s, u_tail = _proj_glu_conv(h, w_in2d, l, conv_w[l], conv_b[l], nb, seq, ms)

        att = _prompt_attention(q, k, v, ga, sinks[l], nb, seq)
        att_s, k_win, v_win = _sample_attention(q[mp:], k[mp:], v[mp:], ga[mp:], cache_k_win, cache_v_win,
                                                l, sinks[l], ns, s_len)

        y, conv_state = _sample_dwconv(state_tm, l, u_s, conv_w[l], conv_b[l], y, mp)
        conv = _ln_pw(y, cg, ln_g[l], ln_b[l], w_pw2, l)

        tail = lambda a, rows: jnp.stack(
            [lax.slice_in_dim(a, (b + 1) * seq - rows, (b + 1) * seq, axis=0) for b in range(nb)])
        kp_l.append(tail(k, WINDOW).reshape(nb, WINDOW, N_KV, HEAD_DIM))
        vp_l.append(tail(v, WINDOW).reshape(nb, WINDOW, N_KV, HEAD_DIM))
        cp_l.append(jnp.transpose(u_tail[:, :, HALO - HIST:, :], (0, 2, 1, 3)).reshape(nb, HIST, D_CONV))
        ks_l.append(k_win)
        vs_l.append(v_win)
        cs_l.append(jnp.transpose(conv_state, (1, 0, 2)))

        xp, xs = _outproj(att, att_s, conv, w_out, l, xp, xs, mod, nb, seq, s_len)

    ys = jnp.transpose(xs.reshape(s_len, ns, d), (1, 0, 2))
    return (xp.reshape(nb, seq, d), ys,
            jnp.stack(kp_l), jnp.stack(vp_l), jnp.stack(cp_l),
            jnp.stack(ks_l), jnp.stack(vs_l), jnp.stack(cs_l))
```

```python
import functools

import numpy as np
import jax
import jax.numpy as jnp
from jax import lax
from jax.experimental import pallas as pl
from jax.experimental.pallas import tpu as pltpu

F32 = jnp.float32
BF16 = jnp.bfloat16

D_MODEL = 4096
D_ATT = 2048
D_CONV = 2048
HEAD_DIM = 128
N_HEADS = 16
N_KV = 4
GQA = 4
KV_W = N_KV * HEAD_DIM
WINDOW = 128
CONV_W = 31
HIST = CONV_W - 1
LANES = 128
SUBLANES = 8
HALO = 32
N_SLAB = D_CONV // LANES
RMS_EPS = 1e-6
LN_EPS = 1e-5
NEG_INF = -1e30
ATT_SCALE = HEAD_DIM ** -0.5
LOG2E = float(np.log2(np.e))
OFF_Q, OFF_K, OFF_V, OFF_GA, OFF_CA, OFF_CB, OFF_CG = 0, 2048, 2560, 3072, 5120, 7168, 9216
SLOPES = [float(np.float32(2.0 ** (-8.0 * (h + 1) / N_HEADS))) for h in range(N_HEADS)]

VMEM_LIMIT = 58 * 1024 * 1024
TM = 1024
MOD_ROWS = 8


def _params(n_axes):
    return pltpu.CompilerParams(dimension_semantics=("arbitrary",) * n_axes,
                                vmem_limit_bytes=VMEM_LIMIT)


def _silu(x):
    return x * jax.nn.sigmoid(x)


def _select_row(ref, idx, count):
    row = ref[0:1, :]
    for n in range(1, count):
        row = jnp.where(idx == n, ref[n:n + 1, :], row)
    return row


def _mod_kernel(c_ref, w_ref, b_ref, o_ref, act, *, tk):
    k = pl.program_id(0)

    @pl.when(k == 0)
    def _():
        act[...] = _silu(c_ref[...]).astype(BF16)
        o_ref[...] = jnp.broadcast_to(b_ref[...], o_ref.shape)

    a = act[:, pl.ds(pl.multiple_of(k * tk, tk), tk)]
    o_ref[...] += jnp.dot(a, w_ref[...].astype(BF16), preferred_element_type=F32)


def _modulation(c_all, w_ada, b_ada, l, tk=256):
    r, d = c_all.shape
    depth, _, n = w_ada.shape
    return pl.pallas_call(
        functools.partial(_mod_kernel, tk=tk),
        grid=(d // tk,),
        in_specs=[
            pl.BlockSpec((r, d), lambda k: (0, 0)),
            pl.BlockSpec((tk, n), lambda k: (l * (d // tk) + k, 0)),
            pl.BlockSpec((None, 1, n), lambda k: (l, 0, 0)),
        ],
        out_specs=pl.BlockSpec((r, n), lambda k: (0, 0)),
        out_shape=jax.ShapeDtypeStruct((r, n), F32),
        scratch_shapes=[pltpu.VMEM((r, d), BF16)],
        compiler_params=_params(1),
        name="modulation",
    )(c_all, w_ada.reshape(depth * d, n), b_ada.reshape(depth, 1, n))


def _headnorm(a, g):
    return a * lax.rsqrt(jnp.mean(a * a, axis=-1, keepdims=True) + RMS_EPS) * g


def _prenorm_kv_kernel(x_ref, g_ref, scale_ref, shift_ref, wk_ref, wv_ref, kg_ref, *rest,
                       tiles_per_batch, n_batch, rows_per_mod):
    h_ref, k_ref, v_ref, wkb, wvb = rest[-5:]

    @pl.when(pl.program_id(0) == 0)
    def _():
        wkb[...] = wk_ref[...].astype(BF16)
        wvb[...] = wv_ref[...].astype(BF16)

    x = x_ref[...]
    y = x * lax.rsqrt(jnp.mean(x * x, axis=-1, keepdims=True) + RMS_EPS)
    y = y * g_ref[...]
    if tiles_per_batch is None:
        reps = x.shape[0] // rows_per_mod
        scale = jnp.concatenate([scale_ref[...]] * reps, axis=0)
        shift = jnp.concatenate([shift_ref[...]] * reps, axis=0)
    else:
        batch = pl.program_id(0) // tiles_per_batch
        scale = _select_row(scale_ref, batch, n_batch)
        shift = _select_row(shift_ref, batch, n_batch)
    h = (y * (1.0 + scale) + shift).astype(BF16)
    h_ref[...] = h
    acc = jnp.dot(h, wkb[...], preferred_element_type=F32)
    kg = kg_ref[...]
    for c in range(N_KV):
        sl = slice(c * HEAD_DIM, (c + 1) * HEAD_DIM)
        k_ref[:, sl] = _headnorm(acc[:, sl], kg)
    v_ref[...] = jnp.dot(h, wvb[...], preferred_element_type=F32)


def _prenorm_kv(x2d, g, mod, mod_row0, n_batch, rows_per_mod, w2d, l, kg, tm, m_total, row0, prev=None):
    m, d = x2d.shape
    if n_batch is None:
        assert mod_row0 % rows_per_mod == 0 and tm % rows_per_mod == 0
        mk = lambda c: pl.BlockSpec((rows_per_mod, d), lambda i: (mod_row0 // rows_per_mod, c))
        tiles_per_batch = None
    else:
        assert mod_row0 % MOD_ROWS == 0 and n_batch <= MOD_ROWS and (m // n_batch) % tm == 0
        mk = lambda c: pl.BlockSpec((MOD_ROWS, d), lambda i: (mod_row0 // MOD_ROWS, c))
        tiles_per_batch = m // n_batch // tm
    wspec = lambda off: pl.BlockSpec((d, KV_W), lambda i: (l, off // KV_W), pipeline_mode=pl.Buffered(1))
    in_specs = [pl.BlockSpec((tm, d), lambda i: (i, 0)), pl.BlockSpec((1, d), lambda i: (0, 0)), mk(1), mk(0),
                wspec(OFF_K), wspec(OFF_V), pl.BlockSpec((1, HEAD_DIM), lambda i: (0, 0))]
    args = [x2d, g.reshape(1, d), mod, mod, w2d, w2d, kg]
    aliases = {}
    if prev is not None:
        aliases = {len(args) + n: n for n in range(3)}
        in_specs += [pl.BlockSpec(memory_space=pl.ANY)] * 3
        args += list(prev)
    assert row0 % tm == 0 and m % tm == 0
    rows = lambda i: (row0 // tm + i, 0)
    return pl.pallas_call(
        functools.partial(_prenorm_kv_kernel, tiles_per_batch=tiles_per_batch, n_batch=n_batch,
                          rows_per_mod=rows_per_mod),
        grid=(m // tm,),
        in_specs=in_specs,
        out_specs=[pl.BlockSpec((tm, d), rows), pl.BlockSpec((tm, KV_W), rows), pl.BlockSpec((tm, KV_W), rows)],
        out_shape=[jax.ShapeDtypeStruct((m_total, d), BF16), jax.ShapeDtypeStruct((m_total, KV_W), F32),
                   jax.ShapeDtypeStruct((m_total, KV_W), F32)],
        scratch_shapes=[pltpu.VMEM((d, KV_W), BF16), pltpu.VMEM((d, KV_W), BF16)],
        input_output_aliases=aliases,
        compiler_params=_params(1),
        name="prenorm_kv",
    )(*args)


def _proj_qgc_kernel(h_ref, w_ref, qg_ref, q_ref, ga_ref, cg_ref, wb, *, tiles):
    j = pl.program_id(0)

    @pl.when(pl.program_id(1) == 0)
    def _():
        wb[...] = w_ref[...].astype(BF16)

    def project():
        return jnp.dot(h_ref[...], wb[...], preferred_element_type=F32)

    @pl.when(j < tiles)
    def _():
        acc = project()
        g = qg_ref[...]
        for c in range(acc.shape[1] // HEAD_DIM):
            sl = slice(c * HEAD_DIM, (c + 1) * HEAD_DIM)
            q_ref[:, sl] = _headnorm(acc[:, sl], g).astype(q_ref.dtype)

    @pl.when((j >= tiles) & (j < 2 * tiles))
    def _():
        ga_ref[...] = _silu(project()).astype(ga_ref.dtype)

    @pl.when(j >= 2 * tiles)
    def _():
        cg_ref[...] = _silu(project()).astype(cg_ref.dtype)


def _proj_qgc(h, w2d, l, qg, tm, tn=512):
    m, d = h.shape
    assert m % tm == 0 and D_ATT == D_CONV
    tiles, nrt = D_ATT // tn, m // tm
    offs = [OFF_Q // tn, OFF_GA // tn, OFF_CG // tn]

    def w_map(j, i):
        region = jnp.minimum(j // tiles, 2)
        start = jnp.where(region == 0, offs[0], jnp.where(region == 1, offs[1], offs[2]))
        return (l, start + j - region * tiles)

    def out_map(region):
        def index(j, i):
            before, after = j < region * tiles, j >= (region + 1) * tiles
            row = jnp.where(before, 0, jnp.where(after, nrt - 1, i))
            return (row, jnp.clip(j - region * tiles, 0, tiles - 1))
        return index

    out = jax.ShapeDtypeStruct((m, D_ATT), BF16)
    return pl.pallas_call(
        functools.partial(_proj_qgc_kernel, tiles=tiles),
        grid=(3 * tiles, nrt),
        in_specs=[pl.BlockSpec((tm, d), lambda j, i: (i, 0)), pl.BlockSpec((d, tn), w_map),
                  pl.BlockSpec((1, HEAD_DIM), lambda j, i: (0, 0))],
        out_specs=[pl.BlockSpec((tm, tn), out_map(r)) for r in range(3)],
        out_shape=[out, out, out],
        scratch_shapes=[pltpu.VMEM((d, tn), BF16)],
        compiler_params=_params(2),
        name="proj_qgc",
    )(h, w2d, qg)


def _proj_glu_conv_kernel(h_ref, hs_ref, wa_ref, wb_ref, cw_ref, cb_ref, y_ref, us_ref, ut_ref,
                          wab, wbb, ubuf0, ubuf1, *, tm, rc, tiles_per_seq, n_prompt_tiles):
    j, i = pl.program_id(0), pl.program_id(1)
    nsl = y_ref.shape[1] // LANES
    base = HALO - HIST

    @pl.when(i == 0)
    def _():
        wab[...] = wa_ref[...].astype(BF16)
        wbb[...] = wb_ref[...].astype(BF16)

    @pl.when((i == 0) & (j == 0))
    def _():
        ubuf1[...] = jnp.zeros(ubuf1.shape, F32)

    def conv(prv):
        for c in range(nsl):
            bias = jnp.broadcast_to(cb_ref[c], (rc, LANES))
            for r0 in range(0, tm, rc):
                acc = bias
                for k in range(CONV_W):
                    acc = acc + cw_ref[c, pl.ds(k, 1), :] * prv[c, pl.ds(r0 + base + k, rc), :]
                y_ref[pl.ds(r0, rc), pl.ds(c * LANES, LANES)] = acc

    def glu(h):
        a = jnp.dot(h, wab[...], preferred_element_type=F32)
        b = jnp.dot(h, wbb[...], preferred_element_type=F32)
        return a * jax.nn.sigmoid(b)

    def prompt_step(cur, prv):
        conv(prv)
        u = glu(h_ref[...])
        seq_start = lax.rem(i, tiles_per_seq) == 0
        for c in range(nsl):
            cur[c, pl.ds(0, HALO), :] = jnp.where(seq_start, 0.0, prv[c, pl.ds(tm, HALO), :])
            cur[c, pl.ds(HALO, tm), :] = u[:, c * LANES:(c + 1) * LANES]

        @pl.when(lax.rem(i, tiles_per_seq) == tiles_per_seq - 1)
        def _():
            for c in range(nsl):
                ut_ref[c] = cur[c, pl.ds(tm, HALO), :]

    def sample_step(prv):
        conv(prv)
        u = glu(hs_ref[...])
        for c in range(nsl):
            us_ref[c] = u[:, c * LANES:(c + 1) * LANES]

    is_prompt = i < n_prompt_tiles
    pl.when(is_prompt & (lax.rem(i, 2) == 0))(lambda: prompt_step(ubuf0, ubuf1))
    pl.when(is_prompt & (lax.rem(i, 2) == 1))(lambda: prompt_step(ubuf1, ubuf0))
    pl.when(i == n_prompt_tiles)(lambda: sample_step(ubuf1 if n_prompt_tiles % 2 == 0 else ubuf0))


def _slab_weights(conv_w, conv_b):
    cw = jnp.pad(conv_w, ((0, HALO - CONV_W), (0, 0)))
    cw = jnp.transpose(cw.reshape(HALO, N_SLAB, LANES), (1, 0, 2))
    return cw, conv_b.reshape(N_SLAB, 1, LANES)


def _proj_glu_conv(h, w2d, l, conv_w, conv_b, n_batch, seq, ms, tn=256):
    m, d = h.shape
    mp = n_batch * seq
    assert seq % TM == 0 and mp % ms == 0 and m == mp + ms
    npt = mp // TM
    assert npt % 2 == 0
    nsl = tn // LANES
    cw, cb = _slab_weights(conv_w, conv_b)
    wspec = lambda off: pl.BlockSpec((d, tn), lambda j, i: (l, off // tn + j))
    return pl.pallas_call(
        functools.partial(_proj_glu_conv_kernel, tm=TM, rc=32, tiles_per_seq=seq // TM, n_prompt_tiles=npt),
        grid=(D_CONV // tn, npt + 1),
        in_specs=[
            pl.BlockSpec((TM, d), lambda j, i: (jnp.minimum(i, npt - 1), 0)),
            pl.BlockSpec((ms, d), lambda j, i: (mp // ms, 0)),
            wspec(OFF_CA),
            wspec(OFF_CB),
            pl.BlockSpec((nsl, HALO, LANES), lambda j, i: (j, 0, 0)),
            pl.BlockSpec((nsl, 1, LANES), lambda j, i: (j, 0, 0)),
        ],
        out_specs=[
            pl.BlockSpec((TM, tn), lambda j, i: (jnp.maximum(i - 1, 0), j)),
            pl.BlockSpec((nsl, ms, LANES), lambda j, i: (j, 0, 0)),
            pl.BlockSpec((None, nsl, HALO, LANES),
                         lambda j, i: (jnp.minimum(i // (seq // TM), n_batch - 1), j, 0, 0)),
        ],
        out_shape=[
            jax.ShapeDtypeStruct((m, D_CONV), F32),
            jax.ShapeDtypeStruct((N_SLAB, ms, LANES), F32),
            jax.ShapeDtypeStruct((n_batch, N_SLAB, HALO, LANES), F32),
        ],
        scratch_shapes=[pltpu.VMEM((d, tn), BF16), pltpu.VMEM((d, tn), BF16),
                        pltpu.VMEM((nsl, HALO + TM, LANES), F32), pltpu.VMEM((nsl, HALO + TM, LANES), F32)],
        compiler_params=_params(2),
        name="proj_glu_conv",
    )(h, h, w2d, w2d, cw, cb)


def _prompt_attn_kernel(sinks_ref, q_ref, kp_ref, kc_ref, vp_ref, vc_ref, ga_ref, o_ref, bias, *, bq):
    qi = lax.broadcasted_iota(jnp.int32, (WINDOW, 2 * WINDOW), 0)
    kj = lax.broadcasted_iota(jnp.int32, (WINDOW, 2 * WINDOW), 1)

    @pl.when((pl.program_id(0) == 0) & (pl.program_id(1) == 0))
    def _():
        dist = WINDOW + qi - kj
        in_band = (dist >= 0) & (dist < WINDOW)
        distf = dist.astype(F32)
        for h in range(N_HEADS):
            bias[h] = jnp.where(in_band, (-SLOPES[h] * LOG2E) * distf, NEG_INF)

    has_prev = kj >= WINDOW * (pl.program_id(1) == 0).astype(jnp.int32)
    nsub = bq // WINDOW
    k_all = jnp.concatenate([kp_ref[...], kc_ref[...]], axis=0).astype(BF16)
    v_all = jnp.concatenate([vp_ref[...], vc_ref[...]], axis=0).astype(BF16)
    for s in range(nsub):
        rows = slice(s * WINDOW, (s + 1) * WINDOW)
        for kv in range(N_KV):
            cols = slice(kv * HEAD_DIM, (kv + 1) * HEAD_DIM)
            kb = k_all[s * WINDOW:(s + 2) * WINDOW, cols]
            vb = v_all[s * WINDOW:(s + 2) * WINDOW, cols]
            for g in range(GQA):
                h = kv * GQA + g
                hc = slice(h * HEAD_DIM, (h + 1) * HEAD_DIM)
                sc = lax.dot_general(q_ref[rows, hc], kb, (((1,), (1,)), ((), ())),
                                     preferred_element_type=F32)
                t = sc * (ATT_SCALE * LOG2E) + bias[h]
                if s == 0:
                    t = jnp.where(has_prev, t, NEG_INF)
                sink = sinks_ref[h] * LOG2E
                m = jnp.maximum(jnp.max(t, axis=-1, keepdims=True), sink)
                p = jnp.exp2(t - m)
                denom = jnp.sum(p, axis=-1, keepdims=True) + jnp.exp2(sink - m)
                o = jnp.dot(p.astype(BF16), vb, preferred_element_type=F32) / denom
                o_ref[rows, hc] = (o * ga_ref[rows, hc].astype(F32)).astype(o_ref.dtype)


def _prompt_attention(q, k, v, ga, sinks, n_batch, seq, bq=512):
    m = n_batch * seq
    nb = seq // bq
    ratio = bq // WINDOW
    cur = lambda n, i, s: (n * nb + i, 0)
    prev = lambda n, i, s: (jnp.maximum((n * nb + i) * ratio - 1, 0), 0)
    return pl.pallas_call(
        functools.partial(_prompt_attn_kernel, bq=bq),
        grid_spec=pltpu.PrefetchScalarGridSpec(
            num_scalar_prefetch=1,
            grid=(n_batch, nb),
            in_specs=[
                pl.BlockSpec((bq, D_ATT), cur),
                pl.BlockSpec((WINDOW, KV_W), prev),
                pl.BlockSpec((bq, KV_W), cur),
                pl.BlockSpec((WINDOW, KV_W), prev),
                pl.BlockSpec((bq, KV_W), cur),
                pl.BlockSpec((bq, D_ATT), cur),
            ],
            out_specs=pl.BlockSpec((bq, D_ATT), cur),
            scratch_shapes=[pltpu.VMEM((N_HEADS, WINDOW, 2 * WINDOW), F32)],
        ),
        out_shape=jax.ShapeDtypeStruct((m, D_ATT), BF16),
        compiler_params=_params(2),
        name="prompt_attention",
    )(sinks, q, k, k, v, v, ga)


def _sample_attn_kernel(slope_ref, sink_ref, q_ref, kc_ref, kn_ref, vc_ref, vn_ref, ga_ref,
                        o_ref, ko_ref, vo_ref, *, bn, s_len):
    nk = WINDOW + SUBLANES
    r = GQA * s_len
    rows = bn * N_KV * r
    ri = lax.broadcasted_iota(jnp.int32, (rows, nk), 0)
    kj = lax.broadcasted_iota(jnp.int32, (rows, nk), 1)
    dist = lax.rem(ri, s_len) + WINDOW - kj
    valid = (dist >= 0) & (dist < WINDOW)
    shift = s_len * N_KV
    keep = WINDOW * N_KV - shift

    def head_rows(win_ref, new_ref, b, kv):
        win = win_ref.at[b][pl.ds(kv, WINDOW, stride=N_KV), :]
        new = new_ref.at[b][pl.ds(kv, SUBLANES, stride=N_KV), :]
        return jnp.concatenate([win, new], axis=0).astype(BF16)

    chains = [(b, kv) for b in range(bn) for kv in range(N_KV)]
    sc = jnp.concatenate(
        [lax.dot_general(q_ref[pl.ds(c * r, r), :], head_rows(kc_ref, kn_ref, b, kv),
                         (((1,), (1,)), ((), ())), preferred_element_type=F32)
         for c, (b, kv) in enumerate(chains)], axis=0)
    sc = sc * ATT_SCALE - slope_ref[...] * dist.astype(F32)
    sc = jnp.where(valid, sc, NEG_INF)
    sink = sink_ref[...]
    m = jnp.maximum(jnp.max(sc, axis=-1, keepdims=True), sink)
    p = jnp.exp(sc - m)
    denom = jnp.sum(p, axis=-1, keepdims=True) + jnp.exp(sink - m)
    p = p.astype(BF16)
    o = jnp.concatenate(
        [jnp.dot(p[c * r:(c + 1) * r], head_rows(vc_ref, vn_ref, b, kv), preferred_element_type=F32)
         for c, (b, kv) in enumerate(chains)], axis=0)
    o_ref[...] = (o / denom * ga_ref[...].astype(F32)).astype(o_ref.dtype)

    for b in range(bn):
        for win_ref, new_ref, out_ref in ((kc_ref, kn_ref, ko_ref), (vc_ref, vn_ref, vo_ref)):
            out_ref[b, pl.ds(0, keep), :] = win_ref[b, pl.ds(shift, keep), :]
            out_ref[b, pl.ds(keep, shift), :] = new_ref[b, pl.ds(0, shift), :]


def _to_head_rows(a, n, s_len):
    a = a.reshape(s_len, n, N_KV, GQA, HEAD_DIM)
    return jnp.transpose(a, (1, 2, 3, 0, 4)).reshape(n * N_KV * GQA * s_len, HEAD_DIM)


def _from_head_rows(a, n, s_len):
    a = a.reshape(n, N_KV, GQA, s_len, HEAD_DIM)
    return jnp.transpose(a, (3, 0, 1, 2, 4)).reshape(s_len * n, D_ATT)


def _sample_attention(q, k_new, v_new, ga, cache_k, cache_v, l, sinks, n, s_len, bn=8):
    assert s_len <= SUBLANES and n % bn == 0
    r = GQA * s_len
    depth = cache_k.shape[0]
    rows = bn * N_KV * r
    per_row = lambda a: jnp.broadcast_to(a.astype(F32).reshape(1, N_HEADS, 1, 1),
                                         (bn, N_HEADS, s_len, 1)).reshape(rows, 1)
    slope_rows = per_row(jnp.asarray(SLOPES, F32))
    sink_rows = per_row(sinks)
    rows_new = SUBLANES * N_KV

    def new_rows(a):
        a = jnp.transpose(a.reshape(s_len, n, N_KV, HEAD_DIM), (1, 0, 2, 3)).reshape(n, s_len * N_KV, HEAD_DIM)
        return jnp.pad(a, ((0, 0), (0, rows_new - s_len * N_KV), (0, 0)))

    win_rows = lambda c: c.reshape(depth * n, WINDOW * N_KV, HEAD_DIM)
    heads = pl.BlockSpec((rows, HEAD_DIM), lambda i: (i, 0))
    win_in = pl.BlockSpec((bn, WINDOW * N_KV, HEAD_DIM), lambda i: (l * (n // bn) + i, 0, 0))
    win_out = pl.BlockSpec((bn, WINDOW * N_KV, HEAD_DIM), lambda i: (i, 0, 0))
    new = pl.BlockSpec((bn, rows_new, HEAD_DIM), lambda i: (i, 0, 0))
    small = pl.BlockSpec((rows, 1), lambda i: (0, 0))
    win_shape = jax.ShapeDtypeStruct((n, WINDOW * N_KV, HEAD_DIM), F32)
    o, k_win, v_win = pl.pallas_call(
        functools.partial(_sample_attn_kernel, bn=bn, s_len=s_len),
        grid=(n // bn,),
        in_specs=[small, small, heads, win_in, new, win_in, new, heads],
        out_specs=[heads, win_out, win_out],
        out_shape=[jax.ShapeDtypeStruct((n * N_KV * r, HEAD_DIM), BF16), win_shape, win_shape],
        compiler_params=_params(1),
        name="sample_attention",
    )(slope_rows, sink_rows, _to_head_rows(q, n, s_len), win_rows(cache_k), new_rows(k_new),
      win_rows(cache_v), new_rows(v_new), _to_head_rows(ga, n, s_len))
    win5 = lambda a: a.reshape(n, WINDOW, N_KV, HEAD_DIM)
    return _from_head_rows(o, n, s_len), win5(k_win), win5(v_win)


def _sample_dwconv_kernel(st_ref, u_ref, cw_ref, cb_ref, y_in_ref, y_ref, so_ref, *, s_len, n):
    del y_in_ref
    for s in range(u_ref.shape[0]):
        lanes = pl.ds(s * LANES, LANES)
        ext = lambda j: st_ref[j, :, lanes] if j < HIST else u_ref[s, pl.ds((j - HIST) * n, n), :]
        for t in range(s_len):
            acc = jnp.broadcast_to(cb_ref[:, lanes], (n, LANES))
            for k in range(CONV_W):
                acc = acc + cw_ref[pl.ds(k, 1), lanes] * ext(t + k)
            y_ref[pl.ds(t * n, n), lanes] = acc
        for j in range(HIST):
            so_ref[j, :, lanes] = ext(j + s_len)


def _sample_dwconv(state_tm, l, u_slab, conv_w, conv_b, y_all, row0, tc=512):
    depth, _, n, c = state_tm.shape
    ms = u_slab.shape[1]
    s_len = ms // n
    assert row0 % ms == 0 and n % SUBLANES == 0
    y, new_state = pl.pallas_call(
        functools.partial(_sample_dwconv_kernel, s_len=s_len, n=n),
        grid=(c // tc,),
        in_specs=[
            pl.BlockSpec((None, HIST, n, tc), lambda j: (l, 0, 0, j)),
            pl.BlockSpec((tc // LANES, ms, LANES), lambda j: (j, 0, 0)),
            pl.BlockSpec((CONV_W, tc), lambda j: (0, j)),
            pl.BlockSpec((1, tc), lambda j: (0, j)),
            pl.BlockSpec(memory_space=pl.ANY),
        ],
        out_specs=[pl.BlockSpec((ms, tc), lambda j: (row0 // ms, j)),
                   pl.BlockSpec((HIST, n, tc), lambda j: (0, 0, j))],
        out_shape=[jax.ShapeDtypeStruct(y_all.shape, F32), jax.ShapeDtypeStruct((HIST, n, c), F32)],
        input_output_aliases={4: 0},
        compiler_params=_params(1),
        name="sample_dwconv",
    )(state_tm, u_slab, conv_w, conv_b.reshape(1, c), y_all)
    return y, new_state


def _ln_pw_kernel(y_ref, lng_ref, lnb_ref, wpw_ref, cg_ref, o_ref, wpb):
    @pl.when(pl.program_id(0) == 0)
    def _():
        wpb[...] = wpw_ref[...].astype(BF16)

    y = y_ref[...]
    mu = jnp.mean(y, axis=-1, keepdims=True)
    yc = y - mu
    var = jnp.mean(yc * yc, axis=-1, keepdims=True)
    yn = yc * lax.rsqrt(var + LN_EPS) * lng_ref[...] + lnb_ref[...]
    a = _silu(yn).astype(BF16)
    o = jnp.dot(a, wpb[...], preferred_element_type=F32)
    o_ref[...] = (o * cg_ref[...].astype(F32)).astype(o_ref.dtype)


def _ln_pw(y, cg, ln_g, ln_b, w_pw2, l, tm=512):
    m, c = y.shape
    depth = w_pw2.shape[0]
    assert m % tm == 0
    const = lambda i: (0, 0)
    return pl.pallas_call(
        _ln_pw_kernel,
        grid=(m // tm,),
        in_specs=[
            pl.BlockSpec((tm, c), lambda i: (i, 0)),
            pl.BlockSpec((1, c), const),
            pl.BlockSpec((1, c), const),
            pl.BlockSpec((c, c), lambda i: (l, 0), pipeline_mode=pl.Buffered(1)),
            pl.BlockSpec((tm, c), lambda i: (i, 0)),
        ],
        out_specs=pl.BlockSpec((tm, c), lambda i: (i, 0)),
        out_shape=jax.ShapeDtypeStruct((m, c), BF16),
        scratch_shapes=[pltpu.VMEM((c, c), BF16)],
        compiler_params=_params(1),
        name="ln_pw",
    )(y, ln_g.reshape(1, -1), ln_b.reshape(1, -1), w_pw2.reshape(depth * c, c), cg)


def _outproj_kernel(a_ref, c_ref, as_ref, cs_ref, w_ref, xp_ref, gp_ref, xs_ref, gs_ref, yp_ref, ys_ref, wb,
                    *, n_prompt_tiles, tiles_per_seq, n_batch, s_len):
    i = pl.program_id(1)

    @pl.when(i == 0)
    def _():
        wb[...] = w_ref[...].astype(BF16)

    def project(att_ref, conv_ref):
        o = jnp.dot(att_ref[...], wb[pl.ds(0, D_ATT), :], preferred_element_type=F32)
        return o + jnp.dot(conv_ref[...], wb[pl.ds(D_ATT, D_CONV), :], preferred_element_type=F32)

    @pl.when(i < n_prompt_tiles)
    def _():
        gate = _select_row(gp_ref, i // tiles_per_seq, n_batch)
        yp_ref[...] = xp_ref[...] + gate * project(a_ref, c_ref)

    @pl.when(i == n_prompt_tiles)
    def _():
        gate = jnp.concatenate([gs_ref[...]] * s_len, axis=0)
        ys_ref[...] = xs_ref[...] + gate * project(as_ref, cs_ref)


def _outproj(att_p, att_s, conv, w_out, l, xp, xs, mod, n_batch, seq, s_len, tn=512):
    mp, d = xp.shape
    ms = xs.shape[0]
    ns = ms // s_len
    depth = w_out.shape[0]
    assert mp % TM == 0 and seq % TM == 0 and mp % ms == 0 and ns % MOD_ROWS == 0
    npt = mp // TM
    gate_col0 = 2 * d // tn
    ptile = lambda j, i: (jnp.minimum(i, npt - 1), j)
    ptile_rows = lambda j, i: (jnp.minimum(i, npt - 1), 0)
    stile = lambda j, i: (mp // ms, 0)
    return pl.pallas_call(
        functools.partial(_outproj_kernel, n_prompt_tiles=npt, tiles_per_seq=seq // TM, n_batch=n_batch,
                          s_len=s_len),
        grid=(d // tn, npt + 1),
        in_specs=[
            pl.BlockSpec((TM, D_ATT), ptile_rows),
            pl.BlockSpec((TM, D_CONV), ptile_rows),
            pl.BlockSpec((ms, D_ATT), lambda j, i: (0, 0)),
            pl.BlockSpec((ms, D_CONV), stile),
            pl.BlockSpec((D_ATT + D_CONV, tn), lambda j, i: (l, j)),
            pl.BlockSpec((TM, tn), ptile),
            pl.BlockSpec((MOD_ROWS, tn), lambda j, i: (ns // MOD_ROWS, gate_col0 + j)),
            pl.BlockSpec((ms, tn), lambda j, i: (0, j)),
            pl.BlockSpec((ns, tn), lambda j, i: (0, gate_col0 + j)),
        ],
        out_specs=[pl.BlockSpec((TM, tn), ptile), pl.BlockSpec((ms, tn), lambda j, i: (0, j))],
        out_shape=[jax.ShapeDtypeStruct((mp, d), F32), jax.ShapeDtypeStruct((ms, d), F32)],
        scratch_shapes=[pltpu.VMEM((D_ATT + D_CONV, tn), BF16)],
        compiler_params=_params(2),
        name="outproj",
    )(att_p, conv, att_s, conv, w_out.reshape(depth * (D_ATT + D_CONV), d), xp, mod, xs, mod)


def kernel(x_prompt, x_sample, c_prompt, c_sample, cache_k_win, cache_v_win, state_conv, w_ada, b_ada,
           norm_g, w_in, q_norm_g, k_norm_g, sinks, conv_w, conv_b, ln_g, ln_b, w_pw2, w_out):
    depth, d, n_in = w_in.shape
    nb, seq, _ = x_prompt.shape
    ns, s_len, _ = x_sample.shape
    mp, ms = nb * seq, ns * s_len
    m_all = mp + ms
    tm_in = m_all // 8
    assert tm_in * 8 == m_all and tm_in % 16 == 0

    xp = x_prompt.reshape(mp, d)
    xs = jnp.transpose(x_sample, (1, 0, 2)).reshape(ms, d)
    pad = (-(ns + nb)) % 16
    c_all = jnp.concatenate([c_sample, c_prompt, jnp.zeros((pad, d), F32)], axis=0)
    w_in2d = w_in.reshape(depth * d, n_in)
    state_tm = jnp.transpose(state_conv, (0, 2, 1, 3))

    kp_l, vp_l, cp_l, ks_l, vs_l, cs_l = [], [], [], [], [], []
    for l in range(depth):
        mod = _modulation(c_all, w_ada, b_ada, l)

        qg = q_norm_g[l].reshape(1, HEAD_DIM)
        kg = k_norm_g[l].reshape(1, HEAD_DIM)
        hkv = _prenorm_kv(xp, norm_g[l], mod, ns, nb, None, w_in2d, l, kg, 512, m_all, 0)
        h, k, v = _prenorm_kv(xs, norm_g[l], mod, 0, None, ns, w_in2d, l, kg, ms, m_all, mp, prev=hkv)
        q, ga, cg = _proj_qgc(h, w_in2d, l, qg, tm_in)
        y, u_s, u_tail = _proj_glu_conv(h, w_in2d, l, conv_w[l], conv_b[l], nb, seq, ms)

        att = _prompt_attention(q, k, v, ga, sinks[l], nb, seq)
        att_s, k_win, v_win = _sample_attention(q[mp:], k[mp:], v[mp:], ga[mp:], cache_k_win, cache_v_win,
                                                l, sinks[l], ns, s_len)

        y, conv_state = _sample_dwconv(state_tm, l, u_s, conv_w[l], conv_b[l], y, mp)
        conv = _ln_pw(y, cg, ln_g[l], ln_b[l], w_pw2, l)

        tail = lambda a, rows: jnp.stack(
            [lax.slice_in_dim(a, (b + 1) * seq - rows, (b + 1) * seq, axis=0) for b in range(nb)])
        kp_l.append(tail(k, WINDOW).reshape(nb, WINDOW, N_KV, HEAD_DIM))
        vp_l.append(tail(v, WINDOW).reshape(nb, WINDOW, N_KV, HEAD_DIM))
        cp_l.append(jnp.transpose(u_tail[:, :, HALO - HIST:, :], (0, 2, 1, 3)).reshape(nb, HIST, D_CONV))
        ks_l.append(k_win)
        vs_l.append(v_win)
        cs_l.append(jnp.transpose(conv_state, (1, 0, 2)))

        xp, xs = _outproj(att, att_s, conv, w_out, l, xp, xs, mod, nb, seq, s_len)

    ys = jnp.transpose(xs.reshape(s_len, ns, d), (1, 0, 2))
    return (xp.reshape(nb, seq, d), ys,
            jnp.stack(kp_l), jnp.stack(vp_l), jnp.stack(cp_l),
            jnp.stack(ks_l), jnp.stack(vs_l), jnp.stack(cs_l))
```

```python
import functools

import numpy as np
import jax
import jax.numpy as jnp
from jax import lax
from jax.experimental import pallas as pl
from jax.experimental.pallas import tpu as pltpu

F32 = jnp.float32
BF16 = jnp.bfloat16

D_MODEL = 4096
D_ATT = 2048
D_CONV = 2048
HEAD_DIM = 128
N_HEADS = 16
N_KV = 4
GQA = 4
KV_W = N_KV * HEAD_DIM
WINDOW = 128
CONV_W = 31
HIST = CONV_W - 1
LANES = 128
SUBLANES = 8
HALO = 32
N_SLAB = D_CONV // LANES
RMS_EPS = 1e-6
LN_EPS = 1e-5
NEG_INF = -1e30
ATT_SCALE = HEAD_DIM ** -0.5
LOG2E = float(np.log2(np.e))
OFF_Q, OFF_K, OFF_V, OFF_GA, OFF_CA, OFF_CB, OFF_CG = 0, 2048, 2560, 3072, 5120, 7168, 9216
SLOPES = [float(np.float32(2.0 ** (-8.0 * (h + 1) / N_HEADS))) for h in range(N_HEADS)]

VMEM_LIMIT = 58 * 1024 * 1024
TM = 1024
MOD_ROWS = 8


def _params(n_axes):
    return pltpu.CompilerParams(dimension_semantics=("arbitrary",) * n_axes,
                                vmem_limit_bytes=VMEM_LIMIT)


def _silu(x):
    return x * jax.nn.sigmoid(x)


def _select_row(ref, idx, count):
    row = ref[0:1, :]
    for n in range(1, count):
        row = jnp.where(idx == n, ref[n:n + 1, :], row)
    return row


def _mod_kernel(c_ref, w_ref, b_ref, o_ref, act, *, tk):
    k = pl.program_id(0)

    @pl.when(k == 0)
    def _():
        act[...] = _silu(c_ref[...]).astype(BF16)
        o_ref[...] = jnp.broadcast_to(b_ref[...], o_ref.shape)

    a = act[:, pl.ds(pl.multiple_of(k * tk, tk), tk)]
    o_ref[...] += jnp.dot(a, w_ref[...].astype(BF16), preferred_element_type=F32)


def _modulation(c_all, w_ada, b_ada, l, tk=256):
    r, d = c_all.shape
    depth, _, n = w_ada.shape
    return pl.pallas_call(
        functools.partial(_mod_kernel, tk=tk),
        grid=(d // tk,),
        in_specs=[
            pl.BlockSpec((r, d), lambda k: (0, 0)),
            pl.BlockSpec((tk, n), lambda k: (l * (d // tk) + k, 0)),
            pl.BlockSpec((None, 1, n), lambda k: (l, 0, 0)),
        ],
        out_specs=pl.BlockSpec((r, n), lambda k: (0, 0)),
        out_shape=jax.ShapeDtypeStruct((r, n), F32),
        scratch_shapes=[pltpu.VMEM((r, d), BF16)],
        compiler_params=_params(1),
        name="modulation",
    )(c_all, w_ada.reshape(depth * d, n), b_ada.reshape(depth, 1, n))


def _headnorm(a, g):
    return a * lax.rsqrt(jnp.mean(a * a, axis=-1, keepdims=True) + RMS_EPS) * g


def _prenorm_kv_kernel(x_ref, g_ref, scale_ref, shift_ref, wk_ref, wv_ref, kg_ref, *rest,
                       tiles_per_batch, n_batch, rows_per_mod, n_tiles):
    h_ref, k_ref, v_ref, wkb, wvb = rest[-5:]

    @pl.when(pl.program_id(0) == 0)
    def _():
        wkb[...] = wk_ref[...].astype(BF16)
        wvb[...] = wv_ref[...].astype(BF16)

    @pl.when(pl.program_id(0) >= n_tiles)
    def _():
        for ref in (h_ref, k_ref, v_ref):
            ref[...] = jnp.zeros(ref.shape, ref.dtype)

    pl.when(pl.program_id(0) < n_tiles)(functools.partial(
        _prenorm_kv_tile, x_ref, g_ref, scale_ref, shift_ref, kg_ref, h_ref, k_ref, v_ref, wkb, wvb,
        tiles_per_batch, n_batch, rows_per_mod))


def _prenorm_kv_tile(x_ref, g_ref, scale_ref, shift_ref, kg_ref, h_ref, k_ref, v_ref, wkb, wvb,
                     tiles_per_batch, n_batch, rows_per_mod):
    x = x_ref[...]
    y = x * lax.rsqrt(jnp.mean(x * x, axis=-1, keepdims=True) + RMS_EPS)
    y = y * g_ref[...]
    if tiles_per_batch is None:
        reps = x.shape[0] // rows_per_mod
        scale = jnp.concatenate([scale_ref[...]] * reps, axis=0)
        shift = jnp.concatenate([shift_ref[...]] * reps, axis=0)
    else:
        batch = pl.program_id(0) // tiles_per_batch
        scale = _select_row(scale_ref, batch, n_batch)
        shift = _select_row(shift_ref, batch, n_batch)
    h = (y * (1.0 + scale) + shift).astype(BF16)
    h_ref[...] = h
    acc = jnp.dot(h, wkb[...], preferred_element_type=F32)
    kg = kg_ref[...]
    for c in range(N_KV):
        sl = slice(c * HEAD_DIM, (c + 1) * HEAD_DIM)
        k_ref[:, sl] = _headnorm(acc[:, sl], kg)
    v_ref[...] = jnp.dot(h, wvb[...], preferred_element_type=F32)


def _prenorm_kv(x2d, g, mod, mod_row0, n_batch, rows_per_mod, w2d, l, kg, tm, m_total, row0, prev=None,
                zero_tiles=0):
    m, d = x2d.shape
    if n_batch is None:
        assert mod_row0 % rows_per_mod == 0 and tm % rows_per_mod == 0
        mk = lambda c: pl.BlockSpec((rows_per_mod, d), lambda i: (mod_row0 // rows_per_mod, c))
        tiles_per_batch = None
    else:
        assert mod_row0 % MOD_ROWS == 0 and n_batch <= MOD_ROWS and (m // n_batch) % tm == 0
        mk = lambda c: pl.BlockSpec((MOD_ROWS, d), lambda i: (mod_row0 // MOD_ROWS, c))
        tiles_per_batch = m // n_batch // tm
    wspec = lambda off: pl.BlockSpec((d, KV_W), lambda i: (l, off // KV_W), pipeline_mode=pl.Buffered(1))
    n_tiles = m // tm
    in_specs = [pl.BlockSpec((tm, d), lambda i: (jnp.minimum(i, n_tiles - 1), 0)),
                pl.BlockSpec((1, d), lambda i: (0, 0)), mk(1), mk(0),
                wspec(OFF_K), wspec(OFF_V), pl.BlockSpec((1, HEAD_DIM), lambda i: (0, 0))]
    args = [x2d, g.reshape(1, d), mod, mod, w2d, w2d, kg]
    aliases = {}
    if prev is not None:
        aliases = {len(args) + n: n for n in range(3)}
        in_specs += [pl.BlockSpec(memory_space=pl.ANY)] * 3
        args += list(prev)
    assert row0 % tm == 0 and m % tm == 0
    rows = lambda i: (row0 // tm + i, 0)
    return pl.pallas_call(
        functools.partial(_prenorm_kv_kernel, tiles_per_batch=tiles_per_batch, n_batch=n_batch,
                          rows_per_mod=rows_per_mod, n_tiles=n_tiles),
        grid=(n_tiles + zero_tiles,),
        in_specs=in_specs,
        out_specs=[pl.BlockSpec((tm, d), rows), pl.BlockSpec((tm, KV_W), rows), pl.BlockSpec((tm, KV_W), rows)],
        out_shape=[jax.ShapeDtypeStruct((m_total, d), BF16), jax.ShapeDtypeStruct((m_total, KV_W), F32),
                   jax.ShapeDtypeStruct((m_total, KV_W), F32)],
        scratch_shapes=[pltpu.VMEM((d, KV_W), BF16), pltpu.VMEM((d, KV_W), BF16)],
        input_output_aliases=aliases,
        compiler_params=_params(1),
        name="prenorm_kv",
    )(*args)


def _proj_qgc_kernel(h_ref, w_hbm, qg_ref, z_ref, stage, wb, sem, *, row0, col_starts, q_tiles):
    j, i = pl.program_id(0), pl.program_id(1)
    d, tn = stage.shape
    n_tiles = len(col_starts)

    def weight_copy(jj):
        col = col_starts[0]
        for n in range(1, n_tiles):
            col = jnp.where(jj == n, col_starts[n], col)
        src = w_hbm.at[pl.ds(row0, d), pl.ds(pl.multiple_of(col, tn), tn)]
        return pltpu.make_async_copy(src, stage, sem)

    @pl.when(i == 0)
    def _():
        @pl.when(j == 0)
        def _():
            weight_copy(j).start()

        weight_copy(j).wait()
        wb[...] = stage[...].astype(BF16)

        @pl.when(j + 1 < n_tiles)
        def _():
            weight_copy(j + 1).start()

    def project():
        return jnp.dot(h_ref[...], wb[...], preferred_element_type=F32)

    @pl.when(j < q_tiles)
    def _():
        acc = project()
        g = qg_ref[...]
        for c in range(tn // HEAD_DIM):
            sl = slice(c * HEAD_DIM, (c + 1) * HEAD_DIM)
            z_ref[:, sl] = _headnorm(acc[:, sl], g).astype(z_ref.dtype)

    @pl.when(j >= q_tiles)
    def _():
        z_ref[...] = _silu(project()).astype(z_ref.dtype)


def _proj_qgc(h, w2d, l, qg, tm, tn=1024):
    m, d = h.shape
    assert m % tm == 0 and D_ATT % tn == 0 and D_CONV % tn == 0
    col_starts = [off + c for off, width in ((OFF_Q, D_ATT), (OFF_GA, D_ATT), (OFF_CG, D_CONV))
                  for c in range(0, width, tn)]
    return pl.pallas_call(
        functools.partial(_proj_qgc_kernel, row0=l * d, col_starts=col_starts, q_tiles=D_ATT // tn),
        grid=(len(col_starts), m // tm),
        in_specs=[pl.BlockSpec((tm, d), lambda j, i: (i, 0)), pl.BlockSpec(memory_space=pl.ANY),
                  pl.BlockSpec((1, HEAD_DIM), lambda j, i: (0, 0))],
        out_specs=pl.BlockSpec((tm, tn), lambda j, i: (i, j)),
        out_shape=jax.ShapeDtypeStruct((m, len(col_starts) * tn), BF16),
        scratch_shapes=[pltpu.VMEM((d, tn), F32), pltpu.VMEM((d, tn), BF16), pltpu.SemaphoreType.DMA],
        compiler_params=_params(2),
        name="proj_qgc",
    )(h, w2d, qg)


def _proj_glu_conv_kernel(h_ref, hs_ref, wa_ref, wb_ref, cw_ref, cb_ref, y_ref, us_ref, ut_ref,
                          wab, wbb, ubuf0, ubuf1, *, tm, rc, tiles_per_seq, n_prompt_tiles):
    j, i = pl.program_id(0), pl.program_id(1)
    nsl = y_ref.shape[1] // LANES
    base = HALO - HIST

    @pl.when(i == 0)
    def _():
        wab[...] = wa_ref[...].astype(BF16)
        wbb[...] = wb_ref[...].astype(BF16)

    @pl.when((i == 0) & (j == 0))
    def _():
        ubuf1[...] = jnp.zeros(ubuf1.shape, F32)

    def conv(prv):
        for c in range(nsl):
            bias = jnp.broadcast_to(cb_ref[c], (rc, LANES))
            for r0 in range(0, tm, rc):
                acc = bias
                for k in range(CONV_W):
                    acc = acc + cw_ref[c, pl.ds(k, 1), :] * prv[c, pl.ds(r0 + base + k, rc), :]
                y_ref[pl.ds(r0, rc), pl.ds(c * LANES, LANES)] = acc

    def glu(h):
        a = jnp.dot(h, wab[...], preferred_element_type=F32)
        b = jnp.dot(h, wbb[...], preferred_element_type=F32)
        return a * jax.nn.sigmoid(b)

    def prompt_step(cur, prv):
        conv(prv)
        u = glu(h_ref[...])
        seq_start = lax.rem(i, tiles_per_seq) == 0
        for c in range(nsl):
            cur[c, pl.ds(0, HALO), :] = jnp.where(seq_start, 0.0, prv[c, pl.ds(tm, HALO), :])
            cur[c, pl.ds(HALO, tm), :] = u[:, c * LANES:(c + 1) * LANES]

        @pl.when(lax.rem(i, tiles_per_seq) == tiles_per_seq - 1)
        def _():
            for c in range(nsl):
                ut_ref[c] = cur[c, pl.ds(tm, HALO), :]

    def sample_step(prv):
        conv(prv)
        u = glu(hs_ref[...])
        for c in range(nsl):
            us_ref[c] = u[:, c * LANES:(c + 1) * LANES]

    is_prompt = i < n_prompt_tiles
    pl.when(is_prompt & (lax.rem(i, 2) == 0))(lambda: prompt_step(ubuf0, ubuf1))
    pl.when(is_prompt & (lax.rem(i, 2) == 1))(lambda: prompt_step(ubuf1, ubuf0))
    pl.when(i == n_prompt_tiles)(lambda: sample_step(ubuf1 if n_prompt_tiles % 2 == 0 else ubuf0))


def _slab_weights(conv_w, conv_b):
    cw = jnp.pad(conv_w, ((0, HALO - CONV_W), (0, 0)))
    cw = jnp.transpose(cw.reshape(HALO, N_SLAB, LANES), (1, 0, 2))
    return cw, conv_b.reshape(N_SLAB, 1, LANES)


def _proj_glu_conv(h, w2d, l, conv_w, conv_b, n_batch, seq, ms, tn=256):
    m, d = h.shape
    mp = n_batch * seq
    assert seq % TM == 0 and mp % ms == 0 and m == mp + ms
    npt = mp // TM
    assert npt % 2 == 0
    nsl = tn // LANES
    cw, cb = _slab_weights(conv_w, conv_b)
    wspec = lambda off: pl.BlockSpec((d, tn), lambda j, i: (l, off // tn + j))
    return pl.pallas_call(
        functools.partial(_proj_glu_conv_kernel, tm=TM, rc=32, tiles_per_seq=seq // TM, n_prompt_tiles=npt),
        grid=(D_CONV // tn, npt + 1),
        in_specs=[
            pl.BlockSpec((TM, d), lambda j, i: (jnp.minimum(i, npt - 1), 0)),
            pl.BlockSpec((ms, d), lambda j, i: (mp // ms, 0)),
            wspec(OFF_CA),
            wspec(OFF_CB),
            pl.BlockSpec((nsl, HALO, LANES), lambda j, i: (j, 0, 0)),
            pl.BlockSpec((nsl, 1, LANES), lambda j, i: (j, 0, 0)),
        ],
        out_specs=[
            pl.BlockSpec((TM, tn), lambda j, i: (jnp.maximum(i - 1, 0), j)),
            pl.BlockSpec((nsl, ms, LANES), lambda j, i: (j, 0, 0)),
            pl.BlockSpec((None, nsl, HALO, LANES),
                         lambda j, i: (jnp.minimum(i // (seq // TM), n_batch - 1), j, 0, 0)),
        ],
        out_shape=[
            jax.ShapeDtypeStruct((mp, D_CONV), F32),
            jax.ShapeDtypeStruct((N_SLAB, ms, LANES), F32),
            jax.ShapeDtypeStruct((n_batch, N_SLAB, HALO, LANES), F32),
        ],
        scratch_shapes=[pltpu.VMEM((d, tn), BF16), pltpu.VMEM((d, tn), BF16),
                        pltpu.VMEM((nsl, HALO + TM, LANES), F32), pltpu.VMEM((nsl, HALO + TM, LANES), F32)],
        compiler_params=_params(2),
        name="proj_glu_conv",
    )(h, h, w2d, w2d, cw, cb)


def _prompt_attn_kernel(sinks_ref, q_ref, kp_ref, kc_ref, vp_ref, vc_ref, ga_ref, o_ref, bias, *, bq):
    qi = lax.broadcasted_iota(jnp.int32, (WINDOW, 2 * WINDOW), 0)
    kj = lax.broadcasted_iota(jnp.int32, (WINDOW, 2 * WINDOW), 1)

    @pl.when((pl.program_id(0) == 0) & (pl.program_id(1) == 0))
    def _():
        dist = WINDOW + qi - kj
        in_band = (dist >= 0) & (dist < WINDOW)
        distf = dist.astype(F32)
        for h in range(N_HEADS):
            bias[h] = jnp.where(in_band, (-SLOPES[h] * LOG2E) * distf, NEG_INF)

    has_prev = kj >= WINDOW * (pl.program_id(1) == 0).astype(jnp.int32)
    nsub = bq // WINDOW
    k_all = jnp.concatenate([kp_ref[...], kc_ref[...]], axis=0).astype(BF16)
    v_all = jnp.concatenate([vp_ref[...], vc_ref[...]], axis=0).astype(BF16)
    for s in range(nsub):
        rows = slice(s * WINDOW, (s + 1) * WINDOW)
        for kv in range(N_KV):
            cols = slice(kv * HEAD_DIM, (kv + 1) * HEAD_DIM)
            kb = k_all[s * WINDOW:(s + 2) * WINDOW, cols]
            vb = v_all[s * WINDOW:(s + 2) * WINDOW, cols]
            for g in range(GQA):
                h = kv * GQA + g
                hc = slice(h * HEAD_DIM, (h + 1) * HEAD_DIM)
                sc = lax.dot_general(q_ref[rows, hc], kb, (((1,), (1,)), ((), ())),
                                     preferred_element_type=F32)
                t = sc * (ATT_SCALE * LOG2E) + bias[h]
                if s == 0:
                    t = jnp.where(has_prev, t, NEG_INF)
                sink = sinks_ref[h] * LOG2E
                m = jnp.maximum(jnp.max(t, axis=-1, keepdims=True), sink)
                p = jnp.exp2(t - m)
                denom = jnp.sum(p, axis=-1, keepdims=True) + jnp.exp2(sink - m)
                o = jnp.dot(p.astype(BF16), vb, preferred_element_type=F32) / denom
                o_ref[rows, hc] = (o * ga_ref[rows, hc].astype(F32)).astype(o_ref.dtype)


def _prompt_attention(z, k, v, sinks, n_batch, seq, bq=512):
    m = n_batch * seq
    nb = seq // bq
    ratio = bq // WINDOW
    cur = lambda n, i, s: (n * nb + i, 0)
    prev = lambda n, i, s: (jnp.maximum((n * nb + i) * ratio - 1, 0), 0)
    return pl.pallas_call(
        functools.partial(_prompt_attn_kernel, bq=bq),
        grid_spec=pltpu.PrefetchScalarGridSpec(
            num_scalar_prefetch=1,
            grid=(n_batch, nb),
            in_specs=[
                pl.BlockSpec((bq, D_ATT), cur),
                pl.BlockSpec((WINDOW, KV_W), prev),
                pl.BlockSpec((bq, KV_W), cur),
                pl.BlockSpec((WINDOW, KV_W), prev),
                pl.BlockSpec((bq, KV_W), cur),
                pl.BlockSpec((bq, D_ATT), lambda n, i, s: (n * nb + i, 1)),
            ],
            out_specs=pl.BlockSpec((bq, D_ATT), cur),
            scratch_shapes=[pltpu.VMEM((N_HEADS, WINDOW, 2 * WINDOW), F32)],
        ),
        out_shape=jax.ShapeDtypeStruct((m, D_ATT), BF16),
        compiler_params=_params(2),
        name="prompt_attention",
    )(sinks, z, k, k, v, v, z)


def _sample_attn_kernel(slope_ref, sink_ref, q_ref, kc_ref, kn_ref, vc_ref, vn_ref, ga_ref,
                        o_ref, ko_ref, vo_ref, *, bn, s_len):
    nk = WINDOW + SUBLANES
    r = GQA * s_len
    rows = bn * N_KV * r
    ri = lax.broadcasted_iota(jnp.int32, (rows, nk), 0)
    kj = lax.broadcasted_iota(jnp.int32, (rows, nk), 1)
    dist = lax.rem(ri, s_len) + WINDOW - kj
    valid = (dist >= 0) & (dist < WINDOW)
    shift = s_len * N_KV
    keep = WINDOW * N_KV - shift

    def head_rows(win_ref, new_ref, b, kv):
        win = win_ref.at[b][pl.ds(kv, WINDOW, stride=N_KV), :]
        new = new_ref.at[b][pl.ds(kv, SUBLANES, stride=N_KV), :]
        return jnp.concatenate([win, new], axis=0).astype(BF16)

    chains = [(b, kv) for b in range(bn) for kv in range(N_KV)]
    sc = jnp.concatenate(
        [lax.dot_general(q_ref[pl.ds(c * r, r), :], head_rows(kc_ref, kn_ref, b, kv),
                         (((1,), (1,)), ((), ())), preferred_element_type=F32)
         for c, (b, kv) in enumerate(chains)], axis=0)
    sc = sc * ATT_SCALE - slope_ref[...] * dist.astype(F32)
    sc = jnp.where(valid, sc, NEG_INF)
    sink = sink_ref[...]
    m = jnp.maximum(jnp.max(sc, axis=-1, keepdims=True), sink)
    p = jnp.exp(sc - m)
    denom = jnp.sum(p, axis=-1, keepdims=True) + jnp.exp(sink - m)
    p = p.astype(BF16)
    o = jnp.concatenate(
        [jnp.dot(p[c * r:(c + 1) * r], head_rows(vc_ref, vn_ref, b, kv), preferred_element_type=F32)
         for c, (b, kv) in enumerate(chains)], axis=0)
    o_ref[...] = (o / denom * ga_ref[...].astype(F32)).astype(o_ref.dtype)

    for b in range(bn):
        for win_ref, new_ref, out_ref in ((kc_ref, kn_ref, ko_ref), (vc_ref, vn_ref, vo_ref)):
            out_ref[b, pl.ds(0, keep), :] = win_ref[b, pl.ds(shift, keep), :]
            out_ref[b, pl.ds(keep, shift), :] = new_ref[b, pl.ds(0, shift), :]


def _to_head_rows(a, n, s_len):
    a = a.reshape(s_len, n, N_KV, GQA, HEAD_DIM)
    return jnp.transpose(a, (1, 2, 3, 0, 4)).reshape(n * N_KV * GQA * s_len, HEAD_DIM)


def _from_head_rows(a, n, s_len):
    a = a.reshape(n, N_KV, GQA, s_len, HEAD_DIM)
    return jnp.transpose(a, (3, 0, 1, 2, 4)).reshape(s_len * n, D_ATT)


def _sample_attention(q, k_new, v_new, ga, cache_k, cache_v, l, sinks, n, s_len, bn=8):
    assert s_len <= SUBLANES and n % bn == 0
    r = GQA * s_len
    depth = cache_k.shape[0]
    rows = bn * N_KV * r
    per_row = lambda a: jnp.broadcast_to(a.astype(F32).reshape(1, N_HEADS, 1, 1),
                                         (bn, N_HEADS, s_len, 1)).reshape(rows, 1)
    slope_rows = per_row(jnp.asarray(SLOPES, F32))
    sink_rows = per_row(sinks)
    rows_new = SUBLANES * N_KV

    def new_rows(a):
        a = jnp.transpose(a.reshape(s_len, n, N_KV, HEAD_DIM), (1, 0, 2, 3)).reshape(n, s_len * N_KV, HEAD_DIM)
        return jnp.pad(a, ((0, 0), (0, rows_new - s_len * N_KV), (0, 0)))

    win_rows = lambda c: c.reshape(depth * n, WINDOW * N_KV, HEAD_DIM)
    heads = pl.BlockSpec((rows, HEAD_DIM), lambda i: (i, 0))
    win_in = pl.BlockSpec((bn, WINDOW * N_KV, HEAD_DIM), lambda i: (l * (n // bn) + i, 0, 0))
    win_out = pl.BlockSpec((bn, WINDOW * N_KV, HEAD_DIM), lambda i: (i, 0, 0))
    new = pl.BlockSpec((bn, rows_new, HEAD_DIM), lambda i: (i, 0, 0))
    small = pl.BlockSpec((rows, 1), lambda i: (0, 0))
    win_shape = jax.ShapeDtypeStruct((n, WINDOW * N_KV, HEAD_DIM), F32)
    o, k_win, v_win = pl.pallas_call(
        functools.partial(_sample_attn_kernel, bn=bn, s_len=s_len),
        grid=(n // bn,),
        in_specs=[small, small, heads, win_in, new, win_in, new, heads],
        out_specs=[heads, win_out, win_out],
        out_shape=[jax.ShapeDtypeStruct((n * N_KV * r, HEAD_DIM), BF16), win_shape, win_shape],
        compiler_params=_params(1),
        name="sample_attention",
    )(slope_rows, sink_rows, _to_head_rows(q, n, s_len), win_rows(cache_k), new_rows(k_new),
      win_rows(cache_v), new_rows(v_new), _to_head_rows(ga, n, s_len))
    win5 = lambda a: a.reshape(n, WINDOW, N_KV, HEAD_DIM)
    return _from_head_rows(o, n, s_len), win5(k_win), win5(v_win)


def _sample_dwconv_kernel(st_ref, u_ref, cw_ref, cb_ref, y_ref, so_ref, *, s_len, n):
    for s in range(u_ref.shape[0]):
        lanes = pl.ds(s * LANES, LANES)
        ext = lambda j: st_ref[j, :, lanes] if j < HIST else u_ref[s, pl.ds((j - HIST) * n, n), :]
        for t in range(s_len):
            acc = jnp.broadcast_to(cb_ref[:, lanes], (n, LANES))
            for k in range(CONV_W):
                acc = acc + cw_ref[pl.ds(k, 1), lanes] * ext(t + k)
            y_ref[pl.ds(t * n, n), lanes] = acc
        for j in range(HIST):
            so_ref[j, :, lanes] = ext(j + s_len)


def _sample_dwconv(state_tm, l, u_slab, conv_w, conv_b, tc=512):
    depth, _, n, c = state_tm.shape
    ms = u_slab.shape[1]
    s_len = ms // n
    assert n % SUBLANES == 0
    y, new_state = pl.pallas_call(
        functools.partial(_sample_dwconv_kernel, s_len=s_len, n=n),
        grid=(c // tc,),
        in_specs=[
            pl.BlockSpec((None, HIST, n, tc), lambda j: (l, 0, 0, j)),
            pl.BlockSpec((tc // LANES, ms, LANES), lambda j: (j, 0, 0)),
            pl.BlockSpec((CONV_W, tc), lambda j: (0, j)),
            pl.BlockSpec((1, tc), lambda j: (0, j)),
        ],
        out_specs=[pl.BlockSpec((ms, tc), lambda j: (0, j)),
                   pl.BlockSpec((HIST, n, tc), lambda j: (0, 0, j))],
        out_shape=[jax.ShapeDtypeStruct((ms, c), F32), jax.ShapeDtypeStruct((HIST, n, c), F32)],
        compiler_params=_params(1),
        name="sample_dwconv",
    )(state_tm, u_slab, conv_w, conv_b.reshape(1, c))
    return y, new_state


def _ln_pw_kernel(yp_ref, ys_ref, lng_ref, lnb_ref, wpw_ref, cg_ref, o_ref, wpb, *, n_prompt_tiles):
    @pl.when(pl.program_id(0) == 0)
    def _():
        wpb[...] = wpw_ref[...].astype(BF16)

    y = jnp.where(pl.program_id(0) < n_prompt_tiles, yp_ref[...], ys_ref[...])
    mu = jnp.mean(y, axis=-1, keepdims=True)
    yc = y - mu
    var = jnp.mean(yc * yc, axis=-1, keepdims=True)
    yn = yc * lax.rsqrt(var + LN_EPS) * lng_ref[...] + lnb_ref[...]
    a = _silu(yn).astype(BF16)
    o = jnp.dot(a, wpb[...], preferred_element_type=F32)
    o_ref[...] = (o * cg_ref[...].astype(F32)).astype(o_ref.dtype)


def _ln_pw(y_p, y_s, z, cg_block, ln_g, ln_b, w_pw2, l):
    mp, c = y_p.shape
    tm = y_s.shape[0]
    m = mp + tm
    depth = w_pw2.shape[0]
    assert mp % tm == 0 and z.shape[0] == m
    npt = mp // tm
    const = lambda i: (0, 0)
    return pl.pallas_call(
        functools.partial(_ln_pw_kernel, n_prompt_tiles=npt),
        grid=(npt + 1,),
        in_specs=[
            pl.BlockSpec((tm, c), lambda i: (jnp.minimum(i, npt - 1), 0)),
            pl.BlockSpec((tm, c), const),
            pl.BlockSpec((1, c), const),
            pl.BlockSpec((1, c), const),
            pl.BlockSpec((c, c), lambda i: (l, 0), pipeline_mode=pl.Buffered(1)),
            pl.BlockSpec((tm, c), lambda i: (i, cg_block)),
        ],
        out_specs=pl.BlockSpec((tm, c), lambda i: (i, 0)),
        out_shape=jax.ShapeDtypeStruct((m, c), BF16),
        scratch_shapes=[pltpu.VMEM((c, c), BF16)],
        compiler_params=_params(1),
        name="ln_pw",
    )(y_p, y_s, ln_g.reshape(1, -1), ln_b.reshape(1, -1), w_pw2.reshape(depth * c, c), z)


def _outproj_kernel(a_ref, c_ref, as_ref, cs_ref, w_ref, xp_ref, gp_ref, xs_ref, gs_ref, yp_ref, ys_ref, wb,
                    *, n_prompt_tiles, tiles_per_seq, n_batch, s_len):
    i = pl.program_id(1)

    @pl.when(i == 0)
    def _():
        wb[...] = w_ref[...].astype(BF16)

    def project(att_ref, conv_ref):
        o = jnp.dot(att_ref[...], wb[pl.ds(0, D_ATT), :], preferred_element_type=F32)
        return o + jnp.dot(conv_ref[...], wb[pl.ds(D_ATT, D_CONV), :], preferred_element_type=F32)

    @pl.when(i < n_prompt_tiles)
    def _():
        gate = _select_row(gp_ref, i // tiles_per_seq, n_batch)
        yp_ref[...] = xp_ref[...] + gate * project(a_ref, c_ref)

    @pl.when(i == n_prompt_tiles)
    def _():
        gate = jnp.concatenate([gs_ref[...]] * s_len, axis=0)
        ys = xs_ref[...] + gate * project(as_ref, cs_ref)
        if len(ys_ref.shape) == 2:
            ys_ref[...] = ys
        else:
            ns = ys_ref.shape[0]
            for t in range(s_len):
                ys_ref[:, t, :] = ys[t * ns:(t + 1) * ns]


def _outproj(att_p, att_s, conv, w_out, l, xp, xs, mod, n_batch, seq, s_len, batch_major_out, tn=512):
    mp, d = xp.shape
    ms = xs.shape[0]
    ns = ms // s_len
    depth = w_out.shape[0]
    assert mp % TM == 0 and seq % TM == 0 and mp % ms == 0 and ns % MOD_ROWS == 0
    npt = mp // TM
    gate_col0 = 2 * d // tn
    ptile = lambda j, i: (jnp.minimum(i, npt - 1), j)
    ptile_rows = lambda j, i: (jnp.minimum(i, npt - 1), 0)
    stile = lambda j, i: (mp // ms, 0)
    if batch_major_out:
        ys_spec = pl.BlockSpec((ns, s_len, tn), lambda j, i: (0, 0, j))
        ys_shape = jax.ShapeDtypeStruct((ns, s_len, d), F32)
    else:
        ys_spec = pl.BlockSpec((ms, tn), lambda j, i: (0, j))
        ys_shape = jax.ShapeDtypeStruct((ms, d), F32)
    return pl.pallas_call(
        functools.partial(_outproj_kernel, n_prompt_tiles=npt, tiles_per_seq=seq // TM, n_batch=n_batch,
                          s_len=s_len),
        grid=(d // tn, npt + 1),
        in_specs=[
            pl.BlockSpec((TM, D_ATT), ptile_rows),
            pl.BlockSpec((TM, D_CONV), ptile_rows),
            pl.BlockSpec((ms, D_ATT), lambda j, i: (0, 0)),
            pl.BlockSpec((ms, D_CONV), stile),
            pl.BlockSpec((D_ATT + D_CONV, tn), lambda j, i: (l, j)),
            pl.BlockSpec((TM, tn), ptile),
            pl.BlockSpec((MOD_ROWS, tn), lambda j, i: (ns // MOD_ROWS, gate_col0 + j)),
            pl.BlockSpec((ms, tn), lambda j, i: (0, j)),
            pl.BlockSpec((ns, tn), lambda j, i: (0, gate_col0 + j)),
        ],
        out_specs=[pl.BlockSpec((TM, tn), ptile), ys_spec],
        out_shape=[jax.ShapeDtypeStruct((mp, d), F32), ys_shape],
        scratch_shapes=[pltpu.VMEM((D_ATT + D_CONV, tn), BF16)],
        compiler_params=_params(2),
        name="outproj",
    )(att_p, conv, att_s, conv, w_out.reshape(depth * (D_ATT + D_CONV), d), xp, mod, xs, mod)


def kernel(x_prompt, x_sample, c_prompt, c_sample, cache_k_win, cache_v_win, state_conv, w_ada, b_ada,
           norm_g, w_in, q_norm_g, k_norm_g, sinks, conv_w, conv_b, ln_g, ln_b, w_pw2, w_out):
    depth, d, n_in = w_in.shape
    nb, seq, _ = x_prompt.shape
    ns, s_len, _ = x_sample.shape
    mp, ms = nb * seq, ns * s_len
    m_all = mp + ms
    tm_in = m_all // 8
    assert tm_in * 8 == m_all and tm_in % 16 == 0

    xp = x_prompt.reshape(mp, d)
    xs = jnp.transpose(x_sample, (1, 0, 2)).reshape(ms, d)
    pad = (-(ns + nb)) % 16
    c_all = jnp.concatenate([c_sample, c_prompt, jnp.zeros((pad, d), F32)], axis=0)
    w_in2d = w_in.reshape(depth * d, n_in)
    state_tm = jnp.transpose(state_conv, (0, 2, 1, 3))

    kp_l, vp_l, cp_l, ks_l, vs_l, cs_l = [], [], [], [], [], []
    for l in range(depth):
        mod = _modulation(c_all, w_ada, b_ada, l)

        qg = q_norm_g[l].reshape(1, HEAD_DIM)
        kg = k_norm_g[l].reshape(1, HEAD_DIM)
        hkv = _prenorm_kv(xp, norm_g[l], mod, ns, nb, None, w_in2d, l, kg, ms, m_all, 0, zero_tiles=1)
        h, k, v = _prenorm_kv(xs, norm_g[l], mod, 0, None, ns, w_in2d, l, kg, ms, m_all, mp, prev=hkv)
        z = _proj_qgc(h, w_in2d, l, qg, tm_in)
        y, u_s, u_tail = _proj_glu_conv(h, w_in2d, l, conv_w[l], conv_b[l], nb, seq, ms)

        att = _prompt_attention(z, k, v, sinks[l], nb, seq)
        att_s, k_win, v_win = _sample_attention(z[mp:, :D_ATT], k[mp:], v[mp:], z[mp:, D_ATT:2 * D_ATT],
                                                cache_k_win, cache_v_win,
                                                l, sinks[l], ns, s_len)

        y_s, conv_state = _sample_dwconv(state_tm, l, u_s, conv_w[l], conv_b[l])
        conv = _ln_pw(y, y_s, z, 2 * D_ATT // D_CONV, ln_g[l], ln_b[l], w_pw2, l)

        tail = lambda a, rows: jnp.stack(
            [lax.slice_in_dim(a, (b + 1) * seq - rows, (b + 1) * seq, axis=0) for b in range(nb)])
        kp_l.append(tail(k, WINDOW).reshape(nb, WINDOW, N_KV, HEAD_DIM))
        vp_l.append(tail(v, WINDOW).reshape(nb, WINDOW, N_KV, HEAD_DIM))
        cp_l.append(jnp.transpose(u_tail[:, :, HALO - HIST:, :], (0, 2, 1, 3)).reshape(nb, HIST, D_CONV))
        ks_l.append(k_win)
        vs_l.append(v_win)
        cs_l.append(jnp.transpose(conv_state, (1, 0, 2)))

        xp, xs = _outproj(att, att_s, conv, w_out, l, xp, xs, mod, nb, seq, s_len, l == depth - 1)

    return (xp.reshape(nb, seq, d), xs,
            jnp.stack(kp_l), jnp.stack(vp_l), jnp.stack(cp_l),
            jnp.stack(ks_l), jnp.stack(vs_l), jnp.stack(cs_l))
```

```python
import functools

import numpy as np
import jax
import jax.numpy as jnp
from jax import lax
from jax.experimental import pallas as pl
from jax.experimental.pallas import tpu as pltpu

F32 = jnp.float32
BF16 = jnp.bfloat16

D_MODEL = 4096
D_ATT = 2048
D_CONV = 2048
HEAD_DIM = 128
N_HEADS = 16
N_KV = 4
GQA = 4
KV_W = N_KV * HEAD_DIM
WINDOW = 128
CONV_W = 31
HIST = CONV_W - 1
LANES = 128
SUBLANES = 8
HALO = 32
N_SLAB = D_CONV // LANES
RMS_EPS = 1e-6
LN_EPS = 1e-5
NEG_INF = -1e30
ATT_SCALE = HEAD_DIM ** -0.5
LOG2E = float(np.log2(np.e))
OFF_Q, OFF_K, OFF_V, OFF_GA, OFF_CA, OFF_CB, OFF_CG = 0, 2048, 2560, 3072, 5120, 7168, 9216
SLOPES = [float(np.float32(2.0 ** (-8.0 * (h + 1) / N_HEADS))) for h in range(N_HEADS)]

VMEM_PHYSICAL = 64 * 1024 * 1024
VMEM_LIMIT = 58 * 1024 * 1024
TM = 1024
MOD_ROWS = 8


def _params(n_axes, vmem_limit=VMEM_LIMIT):
    return pltpu.CompilerParams(dimension_semantics=("arbitrary",) * n_axes,
                                vmem_limit_bytes=vmem_limit)


def _silu(x):
    return x * jax.nn.sigmoid(x)


def _select_row(ref, idx, count):
    row = ref[0:1, :]
    for n in range(1, count):
        row = jnp.where(idx == n, ref[n:n + 1, :], row)
    return row


def _mod_kernel(c_ref, w_ref, b_ref, o_ref, act, *, tk):
    k = pl.program_id(0)

    @pl.when(k == 0)
    def _():
        act[...] = _silu(c_ref[...]).astype(BF16)
        o_ref[...] = jnp.broadcast_to(b_ref[...], o_ref.shape)

    a = act[:, pl.ds(pl.multiple_of(k * tk, tk), tk)]
    o_ref[...] += jnp.dot(a, w_ref[...].astype(BF16), preferred_element_type=F32)


def _modulation(c_all, w_ada, b_ada, l, tk=256):
    r, d = c_all.shape
    depth, _, n = w_ada.shape
    return pl.pallas_call(
        functools.partial(_mod_kernel, tk=tk),
        grid=(d // tk,),
        in_specs=[
            pl.BlockSpec((r, d), lambda k: (0, 0)),
            pl.BlockSpec((tk, n), lambda k: (l * (d // tk) + k, 0)),
            pl.BlockSpec((None, 1, n), lambda k: (l, 0, 0)),
        ],
        out_specs=pl.BlockSpec((r, n), lambda k: (0, 0)),
        out_shape=jax.ShapeDtypeStruct((r, n), F32),
        scratch_shapes=[pltpu.VMEM((r, d), BF16)],
        compiler_params=_params(1),
        name="modulation",
    )(c_all, w_ada.reshape(depth * d, n), b_ada.reshape(depth, 1, n))


def _headnorm(a, g):
    return a * lax.rsqrt(jnp.mean(a * a, axis=-1, keepdims=True) + RMS_EPS) * g


def _prenorm_kv_kernel(x_ref, g_ref, scale_ref, shift_ref, wk_ref, wv_ref, kg_ref, *rest,
                       tiles_per_batch, n_batch, rows_per_mod, n_tiles):
    h_ref, k_ref, v_ref, wkb, wvb = rest[-5:]

    @pl.when(pl.program_id(0) == 0)
    def _():
        wkb[...] = wk_ref[...].astype(BF16)
        wvb[...] = wv_ref[...].astype(BF16)

    @pl.when(pl.program_id(0) >= n_tiles)
    def _():
        for ref in (h_ref, k_ref, v_ref):
            ref[...] = jnp.zeros(ref.shape, ref.dtype)

    pl.when(pl.program_id(0) < n_tiles)(functools.partial(
        _prenorm_kv_tile, x_ref, g_ref, scale_ref, shift_ref, kg_ref, h_ref, k_ref, v_ref, wkb, wvb,
        tiles_per_batch, n_batch, rows_per_mod))


def _prenorm_kv_tile(x_ref, g_ref, scale_ref, shift_ref, kg_ref, h_ref, k_ref, v_ref, wkb, wvb,
                     tiles_per_batch, n_batch, rows_per_mod):
    x = x_ref[...]
    y = x * lax.rsqrt(jnp.mean(x * x, axis=-1, keepdims=True) + RMS_EPS)
    y = y * g_ref[...]
    if tiles_per_batch is None:
        reps = x.shape[0] // rows_per_mod
        scale = jnp.concatenate([scale_ref[...]] * reps, axis=0)
        shift = jnp.concatenate([shift_ref[...]] * reps, axis=0)
    else:
        batch = pl.program_id(0) // tiles_per_batch
        scale = _select_row(scale_ref, batch, n_batch)
        shift = _select_row(shift_ref, batch, n_batch)
    h = (y * (1.0 + scale) + shift).astype(BF16)
    h_ref[...] = h
    acc = jnp.dot(h, wkb[...], preferred_element_type=F32)
    kg = kg_ref[...]
    for c in range(N_KV):
        sl = slice(c * HEAD_DIM, (c + 1) * HEAD_DIM)
        k_ref[:, sl] = _headnorm(acc[:, sl], kg)
    v_ref[...] = jnp.dot(h, wvb[...], preferred_element_type=F32)


def _prenorm_kv(x2d, g, mod, mod_row0, n_batch, rows_per_mod, w2d, l, kg, tm, m_total, row0, prev=None,
                zero_tiles=0):
    m, d = x2d.shape
    if n_batch is None:
        assert mod_row0 % rows_per_mod == 0 and tm % rows_per_mod == 0
        mk = lambda c: pl.BlockSpec((rows_per_mod, d), lambda i: (mod_row0 // rows_per_mod, c))
        tiles_per_batch = None
    else:
        assert mod_row0 % MOD_ROWS == 0 and n_batch <= MOD_ROWS and (m // n_batch) % tm == 0
        mk = lambda c: pl.BlockSpec((MOD_ROWS, d), lambda i: (mod_row0 // MOD_ROWS, c))
        tiles_per_batch = m // n_batch // tm
    wspec = lambda off: pl.BlockSpec((d, KV_W), lambda i: (l, off // KV_W), pipeline_mode=pl.Buffered(1))
    n_tiles = m // tm
    in_specs = [pl.BlockSpec((tm, d), lambda i: (jnp.minimum(i, n_tiles - 1), 0)),
                pl.BlockSpec((1, d), lambda i: (0, 0)), mk(1), mk(0),
                wspec(OFF_K), wspec(OFF_V), pl.BlockSpec((1, HEAD_DIM), lambda i: (0, 0))]
    args = [x2d, g.reshape(1, d), mod, mod, w2d, w2d, kg]
    aliases = {}
    if prev is not None:
        aliases = {len(args) + n: n for n in range(3)}
        in_specs += [pl.BlockSpec(memory_space=pl.ANY)] * 3
        args += list(prev)
    assert row0 % tm == 0 and m % tm == 0
    rows = lambda i: (row0 // tm + i, 0)
    return pl.pallas_call(
        functools.partial(_prenorm_kv_kernel, tiles_per_batch=tiles_per_batch, n_batch=n_batch,
                          rows_per_mod=rows_per_mod, n_tiles=n_tiles),
        grid=(n_tiles + zero_tiles,),
        in_specs=in_specs,
        out_specs=[pl.BlockSpec((tm, d), rows), pl.BlockSpec((tm, KV_W), rows), pl.BlockSpec((tm, KV_W), rows)],
        out_shape=[jax.ShapeDtypeStruct((m_total, d), BF16), jax.ShapeDtypeStruct((m_total, KV_W), F32),
                   jax.ShapeDtypeStruct((m_total, KV_W), F32)],
        scratch_shapes=[pltpu.VMEM((d, KV_W), BF16), pltpu.VMEM((d, KV_W), BF16)],
        input_output_aliases=aliases,
        compiler_params=_params(1),
        name="prenorm_kv",
    )(*args)


def _proj_qgc_kernel(h_ref, w_hbm, qg_ref, z_ref, stage, wb, sem, *, row0, col_starts, q_tiles):
    j, i = pl.program_id(0), pl.program_id(1)
    d, tn = stage.shape
    n_tiles = len(col_starts)

    def weight_copy(jj):
        col = col_starts[0]
        for n in range(1, n_tiles):
            col = jnp.where(jj == n, col_starts[n], col)
        src = w_hbm.at[pl.ds(row0, d), pl.ds(pl.multiple_of(col, tn), tn)]
        return pltpu.make_async_copy(src, stage, sem)

    @pl.when(i == 0)
    def _():
        @pl.when(j == 0)
        def _():
            weight_copy(j).start()

        weight_copy(j).wait()
        wb[...] = stage[...].astype(BF16)

        @pl.when(j + 1 < n_tiles)
        def _():
            weight_copy(j + 1).start()

    def project():
        return jnp.dot(h_ref[...], wb[...], preferred_element_type=F32)

    @pl.when(j < q_tiles)
    def _():
        acc = project()
        g = qg_ref[...]
        for c in range(tn // HEAD_DIM):
            sl = slice(c * HEAD_DIM, (c + 1) * HEAD_DIM)
            z_ref[:, sl] = _headnorm(acc[:, sl], g).astype(z_ref.dtype)

    @pl.when(j >= q_tiles)
    def _():
        z_ref[...] = _silu(project()).astype(z_ref.dtype)


def _proj_qgc(h, w2d, l, qg, tm, tn=1024):
    m, d = h.shape
    assert m % tm == 0 and D_ATT % tn == 0 and D_CONV % tn == 0
    col_starts = [off + c for off, width in ((OFF_Q, D_ATT), (OFF_GA, D_ATT), (OFF_CG, D_CONV))
                  for c in range(0, width, tn)]
    return pl.pallas_call(
        functools.partial(_proj_qgc_kernel, row0=l * d, col_starts=col_starts, q_tiles=D_ATT // tn),
        grid=(len(col_starts), m // tm),
        in_specs=[pl.BlockSpec((tm, d), lambda j, i: (i, 0)), pl.BlockSpec(memory_space=pl.ANY),
                  pl.BlockSpec((1, HEAD_DIM), lambda j, i: (0, 0))],
        out_specs=pl.BlockSpec((tm, tn), lambda j, i: (i, j)),
        out_shape=jax.ShapeDtypeStruct((m, len(col_starts) * tn), BF16),
        scratch_shapes=[pltpu.VMEM((d, tn), F32), pltpu.VMEM((d, tn), BF16), pltpu.SemaphoreType.DMA],
        compiler_params=_params(2),
        name="proj_qgc",
    )(h, w2d, qg)


def _proj_glu_conv_kernel(h_ref, hs_ref, w_hbm, cw_ref, cb_ref, y_ref, us_ref, ut_ref,
                          stage_a, stage_b, wab, wbb, ubuf0, ubuf1, sem_a, sem_b,
                          *, tm, rc, tiles_per_seq, n_prompt_tiles, row0, col_a, col_b):
    j, i = pl.program_id(0), pl.program_id(1)
    d, tn = stage_a.shape
    nsl = tn // LANES
    base = HALO - HIST

    def weight_copies(jj):
        rows = pl.ds(row0, d)
        return (pltpu.make_async_copy(w_hbm.at[rows, pl.ds(pl.multiple_of(col_a + jj * tn, tn), tn)],
                                      stage_a, sem_a),
                pltpu.make_async_copy(w_hbm.at[rows, pl.ds(pl.multiple_of(col_b + jj * tn, tn), tn)],
                                      stage_b, sem_b))

    @pl.when(i == 0)
    def _():
        @pl.when(j == 0)
        def _():
            for copy in weight_copies(j):
                copy.start()

        for copy in weight_copies(j):
            copy.wait()
        wab[...] = stage_a[...].astype(BF16)
        wbb[...] = stage_b[...].astype(BF16)

        @pl.when(j + 1 < pl.num_programs(0))
        def _():
            for copy in weight_copies(j + 1):
                copy.start()

    @pl.when((i == 0) & (j == 0))
    def _():
        ubuf1[...] = jnp.zeros(ubuf1.shape, F32)

    def conv(prv):
        for c in range(nsl):
            bias = jnp.broadcast_to(cb_ref[c], (rc, LANES))
            for r0 in range(0, tm, rc):
                acc = bias
                for k in range(CONV_W):
                    acc = acc + cw_ref[c, pl.ds(k, 1), :] * prv[c, pl.ds(r0 + base + k, rc), :]
                y_ref[pl.ds(r0, rc), pl.ds(c * LANES, LANES)] = acc

    def glu(h, c0, width):
        cols = pl.ds(c0, width)
        a = jnp.dot(h, wab[:, cols], preferred_element_type=F32)
        b = jnp.dot(h, wbb[:, cols], preferred_element_type=F32)
        return a * jax.nn.sigmoid(b)

    mxu_cols = 2 * LANES

    def prompt_step(cur, prv):
        conv(prv)
        h = h_ref[...]
        seq_start = lax.rem(i, tiles_per_seq) == 0
        for c0 in range(0, tn, mxu_cols):
            u = glu(h, c0, mxu_cols)
            for c in range(c0 // LANES, (c0 + mxu_cols) // LANES):
                cur[c, pl.ds(0, HALO), :] = jnp.where(seq_start, 0.0, prv[c, pl.ds(tm, HALO), :])
                cur[c, pl.ds(HALO, tm), :] = u[:, c * LANES - c0:(c + 1) * LANES - c0]

        @pl.when(lax.rem(i, tiles_per_seq) == tiles_per_seq - 1)
        def _():
            for c in range(nsl):
                ut_ref[c] = cur[c, pl.ds(tm, HALO), :]

    def sample_step(prv):
        conv(prv)
        h = hs_ref[...]
        for c0 in range(0, tn, mxu_cols):
            u = glu(h, c0, mxu_cols)
            for c in range(c0 // LANES, (c0 + mxu_cols) // LANES):
                us_ref[c] = u[:, c * LANES - c0:(c + 1) * LANES - c0]

    is_prompt = i < n_prompt_tiles
    pl.when(is_prompt & (lax.rem(i, 2) == 0))(lambda: prompt_step(ubuf0, ubuf1))
    pl.when(is_prompt & (lax.rem(i, 2) == 1))(lambda: prompt_step(ubuf1, ubuf0))
    pl.when(i == n_prompt_tiles)(lambda: sample_step(ubuf1 if n_prompt_tiles % 2 == 0 else ubuf0))


def _slab_weights(conv_w, conv_b):
    cw = jnp.pad(conv_w, ((0, HALO - CONV_W), (0, 0)))
    cw = jnp.transpose(cw.reshape(HALO, N_SLAB, LANES), (1, 0, 2))
    return cw, conv_b.reshape(N_SLAB, 1, LANES)


def _proj_glu_conv(h, w2d, l, conv_w, conv_b, n_batch, seq, ms, tn=512):
    m, d = h.shape
    mp = n_batch * seq
    assert seq % TM == 0 and mp % ms == 0 and m == mp + ms
    npt = mp // TM
    assert npt % 2 == 0
    nsl = tn // LANES
    cw, cb = _slab_weights(conv_w, conv_b)
    return pl.pallas_call(
        functools.partial(_proj_glu_conv_kernel, tm=TM, rc=32, tiles_per_seq=seq // TM, n_prompt_tiles=npt,
                          row0=l * d, col_a=OFF_CA, col_b=OFF_CB),
        grid=(D_CONV // tn, npt + 1),
        in_specs=[
            pl.BlockSpec((TM, d), lambda j, i: (jnp.minimum(i, npt - 1), 0)),
            pl.BlockSpec((ms, d), lambda j, i: (mp // ms, 0), pipeline_mode=pl.Buffered(1)),
            pl.BlockSpec(memory_space=pl.ANY),
            pl.BlockSpec((nsl, HALO, LANES), lambda j, i: (j, 0, 0)),
            pl.BlockSpec((nsl, 1, LANES), lambda j, i: (j, 0, 0)),
        ],
        out_specs=[
            pl.BlockSpec((TM, tn), lambda j, i: (jnp.maximum(i - 1, 0), j)),
            pl.BlockSpec((nsl, ms, LANES), lambda j, i: (j, 0, 0)),
            pl.BlockSpec((None, nsl, HALO, LANES),
                         lambda j, i: (jnp.minimum(i // (seq // TM), n_batch - 1), j, 0, 0)),
        ],
        out_shape=[
            jax.ShapeDtypeStruct((mp, D_CONV), F32),
            jax.ShapeDtypeStruct((N_SLAB, ms, LANES), F32),
            jax.ShapeDtypeStruct((n_batch, N_SLAB, HALO, LANES), F32),
        ],
        scratch_shapes=[pltpu.VMEM((d, tn), F32), pltpu.VMEM((d, tn), F32),
                        pltpu.VMEM((d, tn), BF16), pltpu.VMEM((d, tn), BF16),
                        pltpu.VMEM((nsl, HALO + TM, LANES), F32), pltpu.VMEM((nsl, HALO + TM, LANES), F32),
                        pltpu.SemaphoreType.DMA, pltpu.SemaphoreType.DMA],
        compiler_params=_params(2, VMEM_PHYSICAL - 512 * 1024),
        name="proj_glu_conv",
    )(h, h, w2d, cw, cb)


def _prompt_attn_kernel(sinks_ref, q_ref, kp_ref, kc_ref, vp_ref, vc_ref, ga_ref, o_ref, bias, *, bq):
    qi = lax.broadcasted_iota(jnp.int32, (WINDOW, 2 * WINDOW), 0)
    kj = lax.broadcasted_iota(jnp.int32, (WINDOW, 2 * WINDOW), 1)

    @pl.when((pl.program_id(0) == 0) & (pl.program_id(1) == 0))
    def _():
        dist = WINDOW + qi - kj
        in_band = (dist >= 0) & (dist < WINDOW)
        distf = dist.astype(F32)
        for h in range(N_HEADS):
            bias[h] = jnp.where(in_band, (-SLOPES[h] * LOG2E) * distf, NEG_INF)

    has_prev = kj >= WINDOW * (pl.program_id(1) == 0).astype(jnp.int32)
    nsub = bq // WINDOW
    k_all = jnp.concatenate([kp_ref[...], kc_ref[...]], axis=0).astype(BF16)
    v_all = jnp.concatenate([vp_ref[...], vc_ref[...]], axis=0).astype(BF16)
    for s in range(nsub):
        rows = slice(s * WINDOW, (s + 1) * WINDOW)
        for kv in range(N_KV):
            cols = slice(kv * HEAD_DIM, (kv + 1) * HEAD_DIM)
            kb = k_all[s * WINDOW:(s + 2) * WINDOW, cols]
            vb = v_all[s * WINDOW:(s + 2) * WINDOW, cols]
            for g in range(GQA):
                h = kv * GQA + g
                hc = slice(h * HEAD_DIM, (h + 1) * HEAD_DIM)
                sc = lax.dot_general(q_ref[rows, hc], kb, (((1,), (1,)), ((), ())),
                                     preferred_element_type=F32)
                t = sc * (ATT_SCALE * LOG2E) + bias[h]
                if s == 0:
                    t = jnp.where(has_prev, t, NEG_INF)
                sink = sinks_ref[h] * LOG2E
                m = jnp.maximum(jnp.max(t, axis=-1, keepdims=True), sink)
                p = jnp.exp2(t - m)
                denom = jnp.sum(p, axis=-1, keepdims=True) + jnp.exp2(sink - m)
                o = jnp.dot(p.astype(BF16), vb, preferred_element_type=F32) / denom
                o_ref[rows, hc] = (o * ga_ref[rows, hc].astype(F32)).astype(o_ref.dtype)


def _prompt_attention(z, k, v, sinks, n_batch, seq, bq=512):
    m = n_batch * seq
    nb = seq // bq
    ratio = bq // WINDOW
    cur = lambda n, i, s: (n * nb + i, 0)
    prev = lambda n, i, s: (jnp.maximum((n * nb + i) * ratio - 1, 0), 0)
    return pl.pallas_call(
        functools.partial(_prompt_attn_kernel, bq=bq),
        grid_spec=pltpu.PrefetchScalarGridSpec(
            num_scalar_prefetch=1,
            grid=(n_batch, nb),
            in_specs=[
                pl.BlockSpec((bq, D_ATT), cur),
                pl.BlockSpec((WINDOW, KV_W), prev),
                pl.BlockSpec((bq, KV_W), cur),
                pl.BlockSpec((WINDOW, KV_W), prev),
                pl.BlockSpec((bq, KV_W), cur),
                pl.BlockSpec((bq, D_ATT), lambda n, i, s: (n * nb + i, 1)),
            ],
            out_specs=pl.BlockSpec((bq, D_ATT), cur),
            scratch_shapes=[pltpu.VMEM((N_HEADS, WINDOW, 2 * WINDOW), F32)],
        ),
        out_shape=jax.ShapeDtypeStruct((m, D_ATT), BF16),
        compiler_params=_params(2),
        name="prompt_attention",
    )(sinks, z, k, k, v, v, z)


def _sample_attn_kernel(slope_ref, sink_ref, q_ref, kc_ref, kn_ref, vc_ref, vn_ref, ga_ref,
                        o_ref, ko_ref, vo_ref, *, bn, s_len):
    nk = WINDOW + SUBLANES
    r = GQA * s_len
    rows = bn * N_KV * r
    ri = lax.broadcasted_iota(jnp.int32, (rows, nk), 0)
    kj = lax.broadcasted_iota(jnp.int32, (rows, nk), 1)
    dist = lax.rem(ri, s_len) + WINDOW - kj
    valid = (dist >= 0) & (dist < WINDOW)
    shift = s_len * N_KV
    keep = WINDOW * N_KV - shift

    def head_rows(win_ref, new_ref, b, kv):
        win = win_ref.at[b][pl.ds(kv, WINDOW, stride=N_KV), :]
        new = new_ref.at[b][pl.ds(kv, SUBLANES, stride=N_KV), :]
        return jnp.concatenate([win, new], axis=0).astype(BF16)

    chains = [(b, kv) for b in range(bn) for kv in range(N_KV)]
    sc = jnp.concatenate(
        [lax.dot_general(q_ref[pl.ds(c * r, r), :], head_rows(kc_ref, kn_ref, b, kv),
                         (((1,), (1,)), ((), ())), preferred_element_type=F32)
         for c, (b, kv) in enumerate(chains)], axis=0)
    sc = sc * ATT_SCALE - slope_ref[...] * dist.astype(F32)
    sc = jnp.where(valid, sc, NEG_INF)
    sink = sink_ref[...]
    m = jnp.maximum(jnp.max(sc, axis=-1, keepdims=True), sink)
    p = jnp.exp(sc - m)
    denom = jnp.sum(p, axis=-1, keepdims=True) + jnp.exp(sink - m)
    p = p.astype(BF16)
    o = jnp.concatenate(
        [jnp.dot(p[c * r:(c + 1) * r], head_rows(vc_ref, vn_ref, b, kv), preferred_element_type=F32)
         for c, (b, kv) in enumerate(chains)], axis=0)
    o_ref[...] = (o / denom * ga_ref[...].astype(F32)).astype(o_ref.dtype)

    for b in range(bn):
        for win_ref, new_ref, out_ref in ((kc_ref, kn_ref, ko_ref), (vc_ref, vn_ref, vo_ref)):
            out_ref[b, pl.ds(0, keep), :] = win_ref[b, pl.ds(shift, keep), :]
            out_ref[b, pl.ds(keep, shift), :] = new_ref[b, pl.ds(0, shift), :]


def _to_head_rows(a, n, s_len):
    a = a.reshape(s_len, n, N_KV, GQA, HEAD_DIM)
    return jnp.transpose(a, (1, 2, 3, 0, 4)).reshape(n * N_KV * GQA * s_len, HEAD_DIM)


def _from_head_rows(a, n, s_len):
    a = a.reshape(n, N_KV, GQA, s_len, HEAD_DIM)
    return jnp.transpose(a, (3, 0, 1, 2, 4)).reshape(s_len * n, D_ATT)


def _sample_attention(q, k_new, v_new, ga, cache_k, cache_v, l, sinks, n, s_len, bn=8):
    assert s_len <= SUBLANES and n % bn == 0
    r = GQA * s_len
    depth = cache_k.shape[0]
    rows = bn * N_KV * r
    per_row = lambda a: jnp.broadcast_to(a.astype(F32).reshape(1, N_HEADS, 1, 1),
                                         (bn, N_HEADS, s_len, 1)).reshape(rows, 1)
    slope_rows = per_row(jnp.asarray(SLOPES, F32))
    sink_rows = per_row(sinks)
    rows_new = SUBLANES * N_KV

    def new_rows(a):
        a = jnp.transpose(a.reshape(s_len, n, N_KV, HEAD_DIM), (1, 0, 2, 3)).reshape(n, s_len * N_KV, HEAD_DIM)
        return jnp.pad(a, ((0, 0), (0, rows_new - s_len * N_KV), (0, 0)))

    win_rows = lambda c: c.reshape(depth * n, WINDOW * N_KV, HEAD_DIM)
    heads = pl.BlockSpec((rows, HEAD_DIM), lambda i: (i, 0))
    win_in = pl.BlockSpec((bn, WINDOW * N_KV, HEAD_DIM), lambda i: (l * (n // bn) + i, 0, 0))
    win_out = pl.BlockSpec((bn, WINDOW * N_KV, HEAD_DIM), lambda i: (i, 0, 0))
    new = pl.BlockSpec((bn, rows_new, HEAD_DIM), lambda i: (i, 0, 0))
    small = pl.BlockSpec((rows, 1), lambda i: (0, 0))
    win_shape = jax.ShapeDtypeStruct((n, WINDOW * N_KV, HEAD_DIM), F32)
    o, k_win, v_win = pl.pallas_call(
        functools.partial(_sample_attn_kernel, bn=bn, s_len=s_len),
        grid=(n // bn,),
        in_specs=[small, small, heads, win_in, new, win_in, new, heads],
        out_specs=[heads, win_out, win_out],
        out_shape=[jax.ShapeDtypeStruct((n * N_KV * r, HEAD_DIM), BF16), win_shape, win_shape],
        compiler_params=_params(1),
        name="sample_attention",
    )(slope_rows, sink_rows, _to_head_rows(q, n, s_len), win_rows(cache_k), new_rows(k_new),
      win_rows(cache_v), new_rows(v_new), _to_head_rows(ga, n, s_len))
    win5 = lambda a: a.reshape(n, WINDOW, N_KV, HEAD_DIM)
    return _from_head_rows(o, n, s_len), win5(k_win), win5(v_win)


def _sample_dwconv_kernel(st_ref, u_ref, cw_ref, cb_ref, y_ref, so_ref, *, s_len, n):
    for s in range(u_ref.shape[0]):
        lanes = pl.ds(s * LANES, LANES)
        ext = lambda j: st_ref[j, :, lanes] if j < HIST else u_ref[s, pl.ds((j - HIST) * n, n), :]
        for t in range(s_len):
            acc = jnp.broadcast_to(cb_ref[:, lanes], (n, LANES))
            for k in range(CONV_W):
                acc = acc + cw_ref[pl.ds(k, 1), lanes] * ext(t + k)
            y_ref[pl.ds(t * n, n), lanes] = acc
        for j in range(HIST):
            so_ref[j, :, lanes] = ext(j + s_len)


def _sample_dwconv(state_tm, l, u_slab, conv_w, conv_b, tc=512):
    depth, _, n, c = state_tm.shape
    ms = u_slab.shape[1]
    s_len = ms // n
    assert n % SUBLANES == 0
    y, new_state = pl.pallas_call(
        functools.partial(_sample_dwconv_kernel, s_len=s_len, n=n),
        grid=(c // tc,),
        in_specs=[
            pl.BlockSpec((None, HIST, n, tc), lambda j: (l, 0, 0, j)),
            pl.BlockSpec((tc // LANES, ms, LANES), lambda j: (j, 0, 0)),
            pl.BlockSpec((CONV_W, tc), lambda j: (0, j)),
            pl.BlockSpec((1, tc), lambda j: (0, j)),
        ],
        out_specs=[pl.BlockSpec((ms, tc), lambda j: (0, j)),
                   pl.BlockSpec((HIST, n, tc), lambda j: (0, 0, j))],
        out_shape=[jax.ShapeDtypeStruct((ms, c), F32), jax.ShapeDtypeStruct((HIST, n, c), F32)],
        compiler_params=_params(1),
        name="sample_dwconv",
    )(state_tm, u_slab, conv_w, conv_b.reshape(1, c))
    return y, new_state


def _ln_pw_kernel(yp_ref, ys_ref, lng_ref, lnb_ref, wpw_ref, cg_ref, o_ref, wpb, *, n_prompt_tiles):
    @pl.when(pl.program_id(0) == 0)
    def _():
        wpb[...] = wpw_ref[...].astype(BF16)

    y = jnp.where(pl.program_id(0) < n_prompt_tiles, yp_ref[...], ys_ref[...])
    mu = jnp.mean(y, axis=-1, keepdims=True)
    yc = y - mu
    var = jnp.mean(yc * yc, axis=-1, keepdims=True)
    yn = yc * lax.rsqrt(var + LN_EPS) * lng_ref[...] + lnb_ref[...]
    a = _silu(yn).astype(BF16)
    o = jnp.dot(a, wpb[...], preferred_element_type=F32)
    o_ref[...] = (o * cg_ref[...].astype(F32)).astype(o_ref.dtype)


def _ln_pw(y_p, y_s, z, cg_block, ln_g, ln_b, w_pw2, l):
    mp, c = y_p.shape
    tm = y_s.shape[0]
    m = mp + tm
    depth = w_pw2.shape[0]
    assert mp % tm == 0 and z.shape[0] == m
    npt = mp // tm
    const = lambda i: (0, 0)
    return pl.pallas_call(
        functools.partial(_ln_pw_kernel, n_prompt_tiles=npt),
        grid=(npt + 1,),
        in_specs=[
            pl.BlockSpec((tm, c), lambda i: (jnp.minimum(i, npt - 1), 0)),
            pl.BlockSpec((tm, c), const),
            pl.BlockSpec((1, c), const),
            pl.BlockSpec((1, c), const),
            pl.BlockSpec((c, c), lambda i: (l, 0), pipeline_mode=pl.Buffered(1)),
            pl.BlockSpec((tm, c), lambda i: (i, cg_block)),
        ],
        out_specs=pl.BlockSpec((tm, c), lambda i: (i, 0)),
        out_shape=jax.ShapeDtypeStruct((m, c), BF16),
        scratch_shapes=[pltpu.VMEM((c, c), BF16)],
        compiler_params=_params(1),
        name="ln_pw",
    )(y_p, y_s, ln_g.reshape(1, -1), ln_b.reshape(1, -1), w_pw2.reshape(depth * c, c), z)


def _outproj_kernel(a_ref, c_ref, as_ref, cs_ref, w_ref, xp_ref, gp_ref, xs_ref, gs_ref, yp_ref, ys_ref, wb,
                    *, n_prompt_tiles, tiles_per_seq, n_batch, s_len):
    i = pl.program_id(1)

    @pl.when(i == 0)
    def _():
        wb[...] = w_ref[...].astype(BF16)

    def project(att_ref, conv_ref):
        o = jnp.dot(att_ref[...], wb[pl.ds(0, D_ATT), :], preferred_element_type=F32)
        return o + jnp.dot(conv_ref[...], wb[pl.ds(D_ATT, D_CONV), :], preferred_element_type=F32)

    @pl.when(i < n_prompt_tiles)
    def _():
        gate = _select_row(gp_ref, i // tiles_per_seq, n_batch)
        yp_ref[...] = xp_ref[...] + gate * project(a_ref, c_ref)

    @pl.when(i == n_prompt_tiles)
    def _():
        gate = jnp.concatenate([gs_ref[...]] * s_len, axis=0)
        ys = xs_ref[...] + gate * project(as_ref, cs_ref)
        if len(ys_ref.shape) == 2:
            ys_ref[...] = ys
        else:
            ns = ys_ref.shape[0]
            for t in range(s_len):
                ys_ref[:, t, :] = ys[t * ns:(t + 1) * ns]


def _outproj(att_p, att_s, conv, w_out, l, xp, xs, mod, n_batch, seq, s_len, batch_major_out, tn=512):
    mp, d = xp.shape
    ms = xs.shape[0]
    ns = ms // s_len
    depth = w_out.shape[0]
    assert mp % TM == 0 and seq % TM == 0 and mp % ms == 0 and ns % MOD_ROWS == 0
    npt = mp // TM
    gate_col0 = 2 * d // tn
    ptile = lambda j, i: (jnp.minimum(i, npt - 1), j)
    ptile_rows = lambda j, i: (jnp.minimum(i, npt - 1), 0)
    stile = lambda j, i: (mp // ms, 0)
    if batch_major_out:
        ys_spec = pl.BlockSpec((ns, s_len, tn), lambda j, i: (0, 0, j))
        ys_shape = jax.ShapeDtypeStruct((ns, s_len, d), F32)
    else:
        ys_spec = pl.BlockSpec((ms, tn), lambda j, i: (0, j))
        ys_shape = jax.ShapeDtypeStruct((ms, d), F32)
    return pl.pallas_call(
        functools.partial(_outproj_kernel, n_prompt_tiles=npt, tiles_per_seq=seq // TM, n_batch=n_batch,
                          s_len=s_len),
        grid=(d // tn, npt + 1),
        in_specs=[
            pl.BlockSpec((TM, D_ATT), ptile_rows),
            pl.BlockSpec((TM, D_CONV), ptile_rows),
            pl.BlockSpec((ms, D_ATT), lambda j, i: (0, 0)),
            pl.BlockSpec((ms, D_CONV), stile),
            pl.BlockSpec((D_ATT + D_CONV, tn), lambda j, i: (l, j)),
            pl.BlockSpec((TM, tn), ptile),
            pl.BlockSpec((MOD_ROWS, tn), lambda j, i: (ns // MOD_ROWS, gate_col0 + j)),
            pl.BlockSpec((ms, tn), lambda j, i: (0, j)),
            pl.BlockSpec((ns, tn), lambda j, i: (0, gate_col0 + j)),
        ],
        out_specs=[pl.BlockSpec((TM, tn), ptile), ys_spec],
        out_shape=[jax.ShapeDtypeStruct((mp, d), F32), ys_shape],
        scratch_shapes=[pltpu.VMEM((D_ATT + D_CONV, tn), BF16)],
        compiler_params=_params(2),
        name="outproj",
    )(att_p, conv, att_s, conv, w_out.reshape(depth * (D_ATT + D_CONV), d), xp, mod, xs, mod)


def kernel(x_prompt, x_sample, c_prompt, c_sample, cache_k_win, cache_v_win, state_conv, w_ada, b_ada,
           norm_g, w_in, q_norm_g, k_norm_g, sinks, conv_w, conv_b, ln_g, ln_b, w_pw2, w_out):
    depth, d, n_in = w_in.shape
    nb, seq, _ = x_prompt.shape
    ns, s_len, _ = x_sample.shape
    mp, ms = nb * seq, ns * s_len
    m_all = mp + ms
    tm_in = m_all // 8
    assert tm_in * 8 == m_all and tm_in % 16 == 0

    xp = x_prompt.reshape(mp, d)
    xs = jnp.transpose(x_sample, (1, 0, 2)).reshape(ms, d)
    pad = (-(ns + nb)) % 16
    c_all = jnp.concatenate([c_sample, c_prompt, jnp.zeros((pad, d), F32)], axis=0)
    w_in2d = w_in.reshape(depth * d, n_in)
    state_tm = jnp.transpose(state_conv, (0, 2, 1, 3))

    kp_l, vp_l, cp_l, ks_l, vs_l, cs_l = [], [], [], [], [], []
    for l in range(depth):
        mod = _modulation(c_all, w_ada, b_ada, l)

        qg = q_norm_g[l].reshape(1, HEAD_DIM)
        kg = k_norm_g[l].reshape(1, HEAD_DIM)
        hkv = _prenorm_kv(xp, norm_g[l], mod, ns, nb, None, w_in2d, l, kg, ms, m_all, 0, zero_tiles=1)
        h, k, v = _prenorm_kv(xs, norm_g[l], mod, 0, None, ns, w_in2d, l, kg, ms, m_all, mp, prev=hkv)
        z = _proj_qgc(h, w_in2d, l, qg, tm_in)
        y, u_s, u_tail = _proj_glu_conv(h, w_in2d, l, conv_w[l], conv_b[l], nb, seq, ms)

        att = _prompt_attention(z, k, v, sinks[l], nb, seq)
        att_s, k_win, v_win = _sample_attention(z[mp:, :D_ATT], k[mp:], v[mp:], z[mp:, D_ATT:2 * D_ATT],
                                                cache_k_win, cache_v_win,
                                                l, sinks[l], ns, s_len)

        y_s, conv_state = _sample_dwconv(state_tm, l, u_s, conv_w[l], conv_b[l])
        conv = _ln_pw(y, y_s, z, 2 * D_ATT // D_CONV, ln_g[l], ln_b[l], w_pw2, l)

        tail = lambda a, rows: jnp.stack(
            [lax.slice_in_dim(a, (b + 1) * seq - rows, (b + 1) * seq, axis=0) for b in range(nb)])
        kp_l.append(tail(k, WINDOW).reshape(nb, WINDOW, N_KV, HEAD_DIM))
        vp_l.append(tail(v, WINDOW).reshape(nb, WINDOW, N_KV, HEAD_DIM))
        cp_l.append(jnp.transpose(u_tail[:, :, HALO - HIST:, :], (0, 2, 1, 3)).reshape(nb, HIST, D_CONV))
        ks_l.append(k_win)
        vs_l.append(v_win)
        cs_l.append(jnp.transpose(conv_state, (1, 0, 2)))

        xp, xs = _outproj(att, att_s, conv, w_out, l, xp, xs, mod, nb, seq, s_len, l == depth - 1)

    return (xp.reshape(nb, seq, d), xs,
            jnp.stack(kp_l), jnp.stack(vp_l), jnp.stack(cp_l),
            jnp.stack(ks_l), jnp.stack(vs_l), jnp.stack(cs_l))
```

```python
import functools

import numpy as np
import jax
import jax.numpy as jnp
from jax import lax
from jax.experimental import pallas as pl
from jax.experimental.pallas import tpu as pltpu

F32 = jnp.float32
BF16 = jnp.bfloat16

D_MODEL = 4096
D_ATT = 2048
D_CONV = 2048
HEAD_DIM = 128
N_HEADS = 16
N_KV = 4
GQA = 4
KV_W = N_KV * HEAD_DIM
WINDOW = 128
CONV_W = 31
HIST = CONV_W - 1
LANES = 128
SUBLANES = 8
HALO = 32
N_SLAB = D_CONV // LANES
RMS_EPS = 1e-6
LN_EPS = 1e-5
NEG_INF = -1e30
ATT_SCALE = HEAD_DIM ** -0.5
LOG2E = float(np.log2(np.e))
OFF_Q, OFF_K, OFF_V, OFF_GA, OFF_CA, OFF_CB, OFF_CG = 0, 2048, 2560, 3072, 5120, 7168, 9216
SLOPES = [float(np.float32(2.0 ** (-8.0 * (h + 1) / N_HEADS))) for h in range(N_HEADS)]

VMEM_PHYSICAL = 64 * 1024 * 1024
VMEM_LIMIT = 58 * 1024 * 1024
TM = 1024
MOD_ROWS = 8


def _params(n_axes, vmem_limit=VMEM_LIMIT):
    return pltpu.CompilerParams(dimension_semantics=("arbitrary",) * n_axes,
                                vmem_limit_bytes=vmem_limit)


def _silu(x):
    return x * jax.nn.sigmoid(x)


def _select_row(ref, idx, count):
    row = ref[0:1, :]
    for n in range(1, count):
        row = jnp.where(idx == n, ref[n:n + 1, :], row)
    return row


def _mod_kernel(c_ref, w_ref, b_ref, o_ref, act, *, tk):
    k = pl.program_id(0)

    @pl.when(k == 0)
    def _():
        act[...] = _silu(c_ref[...]).astype(BF16)
        o_ref[...] = jnp.broadcast_to(b_ref[...], o_ref.shape)

    a = act[:, pl.ds(pl.multiple_of(k * tk, tk), tk)]
    o_ref[...] += jnp.dot(a, w_ref[...].astype(BF16), preferred_element_type=F32)


def _modulation(c_all, w_ada, b_ada, l, tk=256):
    r, d = c_all.shape
    depth, _, n = w_ada.shape
    return pl.pallas_call(
        functools.partial(_mod_kernel, tk=tk),
        grid=(d // tk,),
        in_specs=[
            pl.BlockSpec((r, d), lambda k: (0, 0)),
            pl.BlockSpec((tk, n), lambda k: (l * (d // tk) + k, 0)),
            pl.BlockSpec((None, 1, n), lambda k: (l, 0, 0)),
        ],
        out_specs=pl.BlockSpec((r, n), lambda k: (0, 0)),
        out_shape=jax.ShapeDtypeStruct((r, n), F32),
        scratch_shapes=[pltpu.VMEM((r, d), BF16)],
        compiler_params=_params(1),
        name="modulation",
    )(c_all, w_ada.reshape(depth * d, n), b_ada.reshape(depth, 1, n))


def _headnorm(a, g):
    return a * lax.rsqrt(jnp.mean(a * a, axis=-1, keepdims=True) + RMS_EPS) * g


def _prenorm_kv_kernel(x_ref, g_ref, scale_ref, shift_ref, wk_ref, wv_ref, kg_ref, *rest,
                       tiles_per_batch, n_batch, rows_per_mod, n_tiles):
    h_ref, k_ref, v_ref, wkb, wvb = rest[-5:]

    @pl.when(pl.program_id(0) == 0)
    def _():
        wkb[...] = wk_ref[...].astype(BF16)
        wvb[...] = wv_ref[...].astype(BF16)

    @pl.when(pl.program_id(0) >= n_tiles)
    def _():
        for ref in (h_ref, k_ref, v_ref):
            ref[...] = jnp.zeros(ref.shape, ref.dtype)

    pl.when(pl.program_id(0) < n_tiles)(functools.partial(
        _prenorm_kv_tile, x_ref, g_ref, scale_ref, shift_ref, kg_ref, h_ref, k_ref, v_ref, wkb, wvb,
        tiles_per_batch, n_batch, rows_per_mod))


def _prenorm_kv_tile(x_ref, g_ref, scale_ref, shift_ref, kg_ref, h_ref, k_ref, v_ref, wkb, wvb,
                     tiles_per_batch, n_batch, rows_per_mod):
    x = x_ref[...]
    y = x * lax.rsqrt(jnp.mean(x * x, axis=-1, keepdims=True) + RMS_EPS)
    y = y * g_ref[...]
    if tiles_per_batch is None:
        reps = x.shape[0] // rows_per_mod
        scale = jnp.concatenate([scale_ref[...]] * reps, axis=0)
        shift = jnp.concatenate([shift_ref[...]] * reps, axis=0)
    else:
        batch = pl.program_id(0) // tiles_per_batch
        scale = _select_row(scale_ref, batch, n_batch)
        shift = _select_row(shift_ref, batch, n_batch)
    h = (y * (1.0 + scale) + shift).astype(BF16)
    h_ref[...] = h
    acc = jnp.dot(h, wkb[...], preferred_element_type=F32)
    kg = kg_ref[...]
    for c in range(N_KV):
        sl = slice(c * HEAD_DIM, (c + 1) * HEAD_DIM)
        k_ref[:, sl] = _headnorm(acc[:, sl], kg)
    v_ref[...] = jnp.dot(h, wvb[...], preferred_element_type=F32)


def _prenorm_kv(x2d, g, mod, mod_row0, n_batch, rows_per_mod, w2d, l, kg, tm, m_total, row0, prev=None,
                zero_tiles=0):
    m, d = x2d.shape
    if n_batch is None:
        assert mod_row0 % rows_per_mod == 0 and tm % rows_per_mod == 0
        mk = lambda c: pl.BlockSpec((rows_per_mod, d), lambda i: (mod_row0 // rows_per_mod, c))
        tiles_per_batch = None
    else:
        assert mod_row0 % MOD_ROWS == 0 and n_batch <= MOD_ROWS and (m // n_batch) % tm == 0
        mk = lambda c: pl.BlockSpec((MOD_ROWS, d), lambda i: (mod_row0 // MOD_ROWS, c))
        tiles_per_batch = m // n_batch // tm
    wspec = lambda off: pl.BlockSpec((d, KV_W), lambda i: (l, off // KV_W), pipeline_mode=pl.Buffered(1))
    n_tiles = m // tm
    in_specs = [pl.BlockSpec((tm, d), lambda i: (jnp.minimum(i, n_tiles - 1), 0)),
                pl.BlockSpec((1, d), lambda i: (0, 0)), mk(1), mk(0),
                wspec(OFF_K), wspec(OFF_V), pl.BlockSpec((1, HEAD_DIM), lambda i: (0, 0))]
    args = [x2d, g.reshape(1, d), mod, mod, w2d, w2d, kg]
    aliases = {}
    if prev is not None:
        aliases = {len(args) + n: n for n in range(3)}
        in_specs += [pl.BlockSpec(memory_space=pl.ANY)] * 3
        args += list(prev)
    assert row0 % tm == 0 and m % tm == 0
    rows = lambda i: (row0 // tm + i, 0)
    return pl.pallas_call(
        functools.partial(_prenorm_kv_kernel, tiles_per_batch=tiles_per_batch, n_batch=n_batch,
                          rows_per_mod=rows_per_mod, n_tiles=n_tiles),
        grid=(n_tiles + zero_tiles,),
        in_specs=in_specs,
        out_specs=[pl.BlockSpec((tm, d), rows), pl.BlockSpec((tm, KV_W), rows), pl.BlockSpec((tm, KV_W), rows)],
        out_shape=[jax.ShapeDtypeStruct((m_total, d), BF16), jax.ShapeDtypeStruct((m_total, KV_W), F32),
                   jax.ShapeDtypeStruct((m_total, KV_W), F32)],
        scratch_shapes=[pltpu.VMEM((d, KV_W), BF16), pltpu.VMEM((d, KV_W), BF16)],
        input_output_aliases=aliases,
        compiler_params=_params(1),
        name="prenorm_kv",
    )(*args)


def _proj_qgc_kernel(h_ref, w_hbm, qg_ref, z_ref, stage, wb, sem, *, row0, col_starts, q_tiles):
    j, i = pl.program_id(0), pl.program_id(1)
    d, tn = stage.shape
    n_tiles = len(col_starts)

    def weight_copy(jj):
        col = col_starts[0]
        for n in range(1, n_tiles):
            col = jnp.where(jj == n, col_starts[n], col)
        src = w_hbm.at[pl.ds(row0, d), pl.ds(pl.multiple_of(col, tn), tn)]
        return pltpu.make_async_copy(src, stage, sem)

    @pl.when(i == 0)
    def _():
        @pl.when(j == 0)
        def _():
            weight_copy(j).start()

        weight_copy(j).wait()
        wb[...] = stage[...].astype(BF16)

        @pl.when(j + 1 < n_tiles)
        def _():
            weight_copy(j + 1).start()

    group = 2 * LANES

    def column_groups(epilogue):
        def body(c, carry):
            cols = pl.ds(pl.multiple_of(c * group, group), group)
            acc = jnp.dot(h_ref[...], wb[:, cols], preferred_element_type=F32)
            z_ref[:, cols] = epilogue(acc).astype(z_ref.dtype)
            return carry
        lax.fori_loop(0, tn // group, body, 0)

    def head_norms(acc):
        g = qg_ref[...]
        return jnp.concatenate([_headnorm(acc[:, c:c + HEAD_DIM], g) for c in range(0, group, HEAD_DIM)], axis=1)

    pl.when(j < q_tiles)(lambda: column_groups(head_norms))
    pl.when(j >= q_tiles)(lambda: column_groups(_silu))


def _proj_qgc(h, w2d, l, qg, tm, tn=1024):
    m, d = h.shape
    assert m % tm == 0 and D_ATT % tn == 0 and D_CONV % tn == 0
    col_starts = [off + c for off, width in ((OFF_Q, D_ATT), (OFF_GA, D_ATT), (OFF_CG, D_CONV))
                  for c in range(0, width, tn)]
    return pl.pallas_call(
        functools.partial(_proj_qgc_kernel, row0=l * d, col_starts=col_starts, q_tiles=D_ATT // tn),
        grid=(len(col_starts), m // tm),
        in_specs=[pl.BlockSpec((tm, d), lambda j, i: (i, 0)), pl.BlockSpec(memory_space=pl.ANY),
                  pl.BlockSpec((1, HEAD_DIM), lambda j, i: (0, 0))],
        out_specs=pl.BlockSpec((tm, tn), lambda j, i: (i, j)),
        out_shape=jax.ShapeDtypeStruct((m, len(col_starts) * tn), BF16),
        scratch_shapes=[pltpu.VMEM((d, tn), F32), pltpu.VMEM((d, tn), BF16), pltpu.SemaphoreType.DMA],
        compiler_params=_params(2),
        name="proj_qgc",
    )(h, w2d, qg)


def _proj_glu_conv_kernel(h_ref, hs_ref, w_hbm, cw_ref, cb_ref, y_ref, us_ref, ut_ref,
                          stage_a, stage_b, wab, wbb, ubuf0, ubuf1, sem_a, sem_b,
                          *, tm, rc, tiles_per_seq, n_prompt_tiles, row0, col_a, col_b):
    j, i = pl.program_id(0), pl.program_id(1)
    d, tn = stage_a.shape
    nsl = tn // LANES
    base = HALO - HIST

    def weight_copies(jj):
        rows = pl.ds(row0, d)
        return (pltpu.make_async_copy(w_hbm.at[rows, pl.ds(pl.multiple_of(col_a + jj * tn, tn), tn)],
                                      stage_a, sem_a),
                pltpu.make_async_copy(w_hbm.at[rows, pl.ds(pl.multiple_of(col_b + jj * tn, tn), tn)],
                                      stage_b, sem_b))

    @pl.when(i == 0)
    def _():
        @pl.when(j == 0)
        def _():
            for copy in weight_copies(j):
                copy.start()

        for copy in weight_copies(j):
            copy.wait()
        wab[...] = stage_a[...].astype(BF16)
        wbb[...] = stage_b[...].astype(BF16)

        @pl.when(j + 1 < pl.num_programs(0))
        def _():
            for copy in weight_copies(j + 1):
                copy.start()

    @pl.when((i == 0) & (j == 0))
    def _():
        ubuf1[...] = jnp.zeros(ubuf1.shape, F32)

    def conv(prv):
        for c in range(nsl):
            bias = jnp.broadcast_to(cb_ref[c], (rc, LANES))
            for r0 in range(0, tm, rc):
                acc = bias
                for k in range(CONV_W):
                    acc = acc + cw_ref[c, pl.ds(k, 1), :] * prv[c, pl.ds(r0 + base + k, rc), :]
                y_ref[pl.ds(r0, rc), pl.ds(c * LANES, LANES)] = acc

    def glu(h, c0, width):
        cols = pl.ds(c0, width)
        a = jnp.dot(h, wab[:, cols], preferred_element_type=F32)
        b = jnp.dot(h, wbb[:, cols], preferred_element_type=F32)
        return a * jax.nn.sigmoid(b)

    mxu_cols = 2 * LANES

    def prompt_step(cur, prv):
        conv(prv)
        h = h_ref[...]
        seq_start = lax.rem(i, tiles_per_seq) == 0
        for c0 in range(0, tn, mxu_cols):
            u = glu(h, c0, mxu_cols)
            for c in range(c0 // LANES, (c0 + mxu_cols) // LANES):
                cur[c, pl.ds(0, HALO), :] = jnp.where(seq_start, 0.0, prv[c, pl.ds(tm, HALO), :])
                cur[c, pl.ds(HALO, tm), :] = u[:, c * LANES - c0:(c + 1) * LANES - c0]

        @pl.when(lax.rem(i, tiles_per_seq) == tiles_per_seq - 1)
        def _():
            for c in range(nsl):
                ut_ref[c] = cur[c, pl.ds(tm, HALO), :]

    def sample_step(prv):
        conv(prv)
        h = hs_ref[...]
        for c0 in range(0, tn, mxu_cols):
            u = glu(h, c0, mxu_cols)
            for c in range(c0 // LANES, (c0 + mxu_cols) // LANES):
                us_ref[c] = u[:, c * LANES - c0:(c + 1) * LANES - c0]

    is_prompt = i < n_prompt_tiles
    pl.when(is_prompt & (lax.rem(i, 2) == 0))(lambda: prompt_step(ubuf0, ubuf1))
    pl.when(is_prompt & (lax.rem(i, 2) == 1))(lambda: prompt_step(ubuf1, ubuf0))
    pl.when(i == n_prompt_tiles)(lambda: sample_step(ubuf1 if n_prompt_tiles % 2 == 0 else ubuf0))


def _slab_weights(conv_w, conv_b):
    cw = jnp.pad(conv_w, ((0, HALO - CONV_W), (0, 0)))
    cw = jnp.transpose(cw.reshape(HALO, N_SLAB, LANES), (1, 0, 2))
    return cw, conv_b.reshape(N_SLAB, 1, LANES)


def _proj_glu_conv(h, w2d, l, conv_w, conv_b, n_batch, seq, ms, tn=512):
    m, d = h.shape
    mp = n_batch * seq
    assert seq % TM == 0 and mp % ms == 0 and m == mp + ms
    npt = mp // TM
    assert npt % 2 == 0
    nsl = tn // LANES
    cw, cb = _slab_weights(conv_w, conv_b)
    return pl.pallas_call(
        functools.partial(_proj_glu_conv_kernel, tm=TM, rc=32, tiles_per_seq=seq // TM, n_prompt_tiles=npt,
                          row0=l * d, col_a=OFF_CA, col_b=OFF_CB),
        grid=(D_CONV // tn, npt + 1),
        in_specs=[
            pl.BlockSpec((TM, d), lambda j, i: (jnp.minimum(i, npt - 1), 0)),
            pl.BlockSpec((ms, d), lambda j, i: (mp // ms, 0), pipeline_mode=pl.Buffered(1)),
            pl.BlockSpec(memory_space=pl.ANY),
            pl.BlockSpec((nsl, HALO, LANES), lambda j, i: (j, 0, 0)),
            pl.BlockSpec((nsl, 1, LANES), lambda j, i: (j, 0, 0)),
        ],
        out_specs=[
            pl.BlockSpec((TM, tn), lambda j, i: (jnp.maximum(i - 1, 0), j)),
            pl.BlockSpec((nsl, ms, LANES), lambda j, i: (j, 0, 0)),
            pl.BlockSpec((None, nsl, HALO, LANES),
                         lambda j, i: (jnp.minimum(i // (seq // TM), n_batch - 1), j, 0, 0)),
        ],
        out_shape=[
            jax.ShapeDtypeStruct((mp, D_CONV), F32),
            jax.ShapeDtypeStruct((N_SLAB, ms, LANES), F32),
            jax.ShapeDtypeStruct((n_batch, N_SLAB, HALO, LANES), F32),
        ],
        scratch_shapes=[pltpu.VMEM((d, tn), F32), pltpu.VMEM((d, tn), F32),
                        pltpu.VMEM((d, tn), BF16), pltpu.VMEM((d, tn), BF16),
                        pltpu.VMEM((nsl, HALO + TM, LANES), F32), pltpu.VMEM((nsl, HALO + TM, LANES), F32),
                        pltpu.SemaphoreType.DMA, pltpu.SemaphoreType.DMA],
        compiler_params=_params(2, VMEM_PHYSICAL - 512 * 1024),
        name="proj_glu_conv",
    )(h, h, w2d, cw, cb)


def _prompt_attn_kernel(sinks_ref, q_ref, kp_ref, kc_ref, vp_ref, vc_ref, ga_ref, o_ref, bias, *, bq):
    qi = lax.broadcasted_iota(jnp.int32, (WINDOW, 2 * WINDOW), 0)
    kj = lax.broadcasted_iota(jnp.int32, (WINDOW, 2 * WINDOW), 1)

    @pl.when((pl.program_id(0) == 0) & (pl.program_id(1) == 0))
    def _():
        dist = WINDOW + qi - kj
        in_band = (dist >= 0) & (dist < WINDOW)
        distf = dist.astype(F32)
        for h in range(N_HEADS):
            bias[h] = jnp.where(in_band, (-SLOPES[h] * LOG2E) * distf, NEG_INF)

    has_prev = kj >= WINDOW * (pl.program_id(1) == 0).astype(jnp.int32)
    nsub = bq // WINDOW
    k_all = jnp.concatenate([kp_ref[...], kc_ref[...]], axis=0).astype(BF16)
    v_all = jnp.concatenate([vp_ref[...], vc_ref[...]], axis=0).astype(BF16)
    for s in range(nsub):
        rows = slice(s * WINDOW, (s + 1) * WINDOW)
        for kv in range(N_KV):
            cols = slice(kv * HEAD_DIM, (kv + 1) * HEAD_DIM)
            kb = k_all[s * WINDOW:(s + 2) * WINDOW, cols]
            vb = v_all[s * WINDOW:(s + 2) * WINDOW, cols]
            for g in range(GQA):
                h = kv * GQA + g
                hc = slice(h * HEAD_DIM, (h + 1) * HEAD_DIM)
                sc = lax.dot_general(q_ref[rows, hc], kb, (((1,), (1,)), ((), ())),
                                     preferred_element_type=F32)
                t = sc * (ATT_SCALE * LOG2E) + bias[h]
                if s == 0:
                    t = jnp.where(has_prev, t, NEG_INF)
                sink = sinks_ref[h] * LOG2E
                m = jnp.maximum(jnp.max(t, axis=-1, keepdims=True), sink)
                p = jnp.exp2(t - m)
                denom = jnp.sum(p, axis=-1, keepdims=True) + jnp.exp2(sink - m)
                o = jnp.dot(p.astype(BF16), vb, preferred_element_type=F32) / denom
                o_ref[rows, hc] = (o * ga_ref[rows, hc].astype(F32)).astype(o_ref.dtype)


def _prompt_attention(z, k, v, sinks, n_batch, seq, bq=512):
    m = n_batch * seq
    nb = seq // bq
    ratio = bq // WINDOW
    cur = lambda n, i, s: (n * nb + i, 0)
    prev = lambda n, i, s: (jnp.maximum((n * nb + i) * ratio - 1, 0), 0)
    return pl.pallas_call(
        functools.partial(_prompt_attn_kernel, bq=bq),
        grid_spec=pltpu.PrefetchScalarGridSpec(
            num_scalar_prefetch=1,
            grid=(n_batch, nb),
            in_specs=[
                pl.BlockSpec((bq, D_ATT), cur),
                pl.BlockSpec((WINDOW, KV_W), prev),
                pl.BlockSpec((bq, KV_W), cur),
                pl.BlockSpec((WINDOW, KV_W), prev),
                pl.BlockSpec((bq, KV_W), cur),
                pl.BlockSpec((bq, D_ATT), lambda n, i, s: (n * nb + i, 1)),
            ],
            out_specs=pl.BlockSpec((bq, D_ATT), cur),
            scratch_shapes=[pltpu.VMEM((N_HEADS, WINDOW, 2 * WINDOW), F32)],
        ),
        out_shape=jax.ShapeDtypeStruct((m, D_ATT), BF16),
        compiler_params=_params(2),
        name="prompt_attention",
    )(sinks, z, k, k, v, v, z)


def _sample_attn_kernel(slope_ref, sink_ref, q_ref, kc_ref, kn_ref, vc_ref, vn_ref, ga_ref,
                        o_ref, ko_ref, vo_ref, *, bn, s_len):
    nk = WINDOW + SUBLANES
    r = GQA * s_len
    rows = bn * N_KV * r
    ri = lax.broadcasted_iota(jnp.int32, (rows, nk), 0)
    kj = lax.broadcasted_iota(jnp.int32, (rows, nk), 1)
    dist = lax.rem(ri, s_len) + WINDOW - kj
    valid = (dist >= 0) & (dist < WINDOW)
    shift = s_len * N_KV
    keep = WINDOW * N_KV - shift

    def head_rows(win_ref, new_ref, b, kv):
        win = win_ref.at[b][pl.ds(kv, WINDOW, stride=N_KV), :]
        new = new_ref.at[b][pl.ds(kv, SUBLANES, stride=N_KV), :]
        return jnp.concatenate([win, new], axis=0).astype(BF16)

    chains = [(b, kv) for b in range(bn) for kv in range(N_KV)]
    sc = jnp.concatenate(
        [lax.dot_general(q_ref[pl.ds(c * r, r), :], head_rows(kc_ref, kn_ref, b, kv),
                         (((1,), (1,)), ((), ())), preferred_element_type=F32)
         for c, (b, kv) in enumerate(chains)], axis=0)
    sc = sc * ATT_SCALE - slope_ref[...] * dist.astype(F32)
    sc = jnp.where(valid, sc, NEG_INF)
    sink = sink_ref[...]
    m = jnp.maximum(jnp.max(sc, axis=-1, keepdims=True), sink)
    p = jnp.exp(sc - m)
    denom = jnp.sum(p, axis=-1, keepdims=True) + jnp.exp(sink - m)
    p = p.astype(BF16)
    o = jnp.concatenate(
        [jnp.dot(p[c * r:(c + 1) * r], head_rows(vc_ref, vn_ref, b, kv), preferred_element_type=F32)
         for c, (b, kv) in enumerate(chains)], axis=0)
    o_ref[...] = (o / denom * ga_ref[...].astype(F32)).astype(o_ref.dtype)

    for b in range(bn):
        for win_ref, new_ref, out_ref in ((kc_ref, kn_ref, ko_ref), (vc_ref, vn_ref, vo_ref)):
            out_ref[b, pl.ds(0, keep), :] = win_ref[b, pl.ds(shift, keep), :]
            out_ref[b, pl.ds(keep, shift), :] = new_ref[b, pl.ds(0, shift), :]


def _to_head_rows(a, n, s_len):
    a = a.reshape(s_len, n, N_KV, GQA, HEAD_DIM)
    return jnp.transpose(a, (1, 2, 3, 0, 4)).reshape(n * N_KV * GQA * s_len, HEAD_DIM)


def _from_head_rows(a, n, s_len):
    a = a.reshape(n, N_KV, GQA, s_len, HEAD_DIM)
    return jnp.transpose(a, (3, 0, 1, 2, 4)).reshape(s_len * n, D_ATT)


def _sample_attention(q, k_new, v_new, ga, cache_k, cache_v, l, sinks, n, s_len, bn=8):
    assert s_len <= SUBLANES and n % bn == 0
    r = GQA * s_len
    depth = cache_k.shape[0]
    rows = bn * N_KV * r
    per_row = lambda a: jnp.broadcast_to(a.astype(F32).reshape(1, N_HEADS, 1, 1),
                                         (bn, N_HEADS, s_len, 1)).reshape(rows, 1)
    slope_rows = per_row(jnp.asarray(SLOPES, F32))
    sink_rows = per_row(sinks)
    rows_new = SUBLANES * N_KV

    def new_rows(a):
        a = jnp.transpose(a.reshape(s_len, n, N_KV, HEAD_DIM), (1, 0, 2, 3)).reshape(n, s_len * N_KV, HEAD_DIM)
        return jnp.pad(a, ((0, 0), (0, rows_new - s_len * N_KV), (0, 0)))

    win_rows = lambda c: c.reshape(depth * n, WINDOW * N_KV, HEAD_DIM)
    heads = pl.BlockSpec((rows, HEAD_DIM), lambda i: (i, 0))
    win_in = pl.BlockSpec((bn, WINDOW * N_KV, HEAD_DIM), lambda i: (l * (n // bn) + i, 0, 0))
    win_out = pl.BlockSpec((bn, WINDOW * N_KV, HEAD_DIM), lambda i: (i, 0, 0))
    new = pl.BlockSpec((bn, rows_new, HEAD_DIM), lambda i: (i, 0, 0))
    small = pl.BlockSpec((rows, 1), lambda i: (0, 0))
    win_shape = jax.ShapeDtypeStruct((n, WINDOW * N_KV, HEAD_DIM), F32)
    o, k_win, v_win = pl.pallas_call(
        functools.partial(_sample_attn_kernel, bn=bn, s_len=s_len),
        grid=(n // bn,),
        in_specs=[small, small, heads, win_in, new, win_in, new, heads],
        out_specs=[heads, win_out, win_out],
        out_shape=[jax.ShapeDtypeStruct((n * N_KV * r, HEAD_DIM), BF16), win_shape, win_shape],
        compiler_params=_params(1),
        name="sample_attention",
    )(slope_rows, sink_rows, _to_head_rows(q, n, s_len), win_rows(cache_k), new_rows(k_new),
      win_rows(cache_v), new_rows(v_new), _to_head_rows(ga, n, s_len))
    win5 = lambda a: a.reshape(n, WINDOW, N_KV, HEAD_DIM)
    return _from_head_rows(o, n, s_len), win5(k_win), win5(v_win)


def _sample_dwconv_kernel(st_ref, u_ref, cw_ref, cb_ref, y_ref, so_ref, *, s_len, n):
    for s in range(u_ref.shape[0]):
        lanes = pl.ds(s * LANES, LANES)
        ext = lambda j: st_ref[j, :, lanes] if j < HIST else u_ref[s, pl.ds((j - HIST) * n, n), :]
        for t in range(s_len):
            acc = jnp.broadcast_to(cb_ref[:, lanes], (n, LANES))
            for k in range(CONV_W):
                acc = acc + cw_ref[pl.ds(k, 1), lanes] * ext(t + k)
            y_ref[pl.ds(t * n, n), lanes] = acc
        for j in range(HIST):
            so_ref[j, :, lanes] = ext(j + s_len)


def _sample_dwconv(state_tm, l, u_slab, conv_w, conv_b, tc=512):
    depth, _, n, c = state_tm.shape
    ms = u_slab.shape[1]
    s_len = ms // n
    assert n % SUBLANES == 0
    y, new_state = pl.pallas_call(
        functools.partial(_sample_dwconv_kernel, s_len=s_len, n=n),
        grid=(c // tc,),
        in_specs=[
            pl.BlockSpec((None, HIST, n, tc), lambda j: (l, 0, 0, j)),
            pl.BlockSpec((tc // LANES, ms, LANES), lambda j: (j, 0, 0)),
            pl.BlockSpec((CONV_W, tc), lambda j: (0, j)),
            pl.BlockSpec((1, tc), lambda j: (0, j)),
        ],
        out_specs=[pl.BlockSpec((ms, tc), lambda j: (0, j)),
                   pl.BlockSpec((HIST, n, tc), lambda j: (0, 0, j))],
        out_shape=[jax.ShapeDtypeStruct((ms, c), F32), jax.ShapeDtypeStruct((HIST, n, c), F32)],
        compiler_params=_params(1),
        name="sample_dwconv",
    )(state_tm, u_slab, conv_w, conv_b.reshape(1, c))
    return y, new_state


def _ln_pw_kernel(yp_ref, ys_ref, lng_ref, lnb_ref, wpw_ref, cg_ref, o_ref, wpb, *, n_prompt_tiles):
    @pl.when(pl.program_id(0) == 0)
    def _():
        wpb[...] = wpw_ref[...].astype(BF16)

    y = jnp.where(pl.program_id(0) < n_prompt_tiles, yp_ref[...], ys_ref[...])
    mu = jnp.mean(y, axis=-1, keepdims=True)
    yc = y - mu
    var = jnp.mean(yc * yc, axis=-1, keepdims=True)
    yn = yc * lax.rsqrt(var + LN_EPS) * lng_ref[...] + lnb_ref[...]
    a = _silu(yn).astype(BF16)
    o = jnp.dot(a, wpb[...], preferred_element_type=F32)
    o_ref[...] = (o * cg_ref[...].astype(F32)).astype(o_ref.dtype)


def _ln_pw(y_p, y_s, z, cg_block, ln_g, ln_b, w_pw2, l):
    mp, c = y_p.shape
    tm = y_s.shape[0]
    m = mp + tm
    depth = w_pw2.shape[0]
    assert mp % tm == 0 and z.shape[0] == m
    npt = mp // tm
    const = lambda i: (0, 0)
    return pl.pallas_call(
        functools.partial(_ln_pw_kernel, n_prompt_tiles=npt),
        grid=(npt + 1,),
        in_specs=[
            pl.BlockSpec((tm, c), lambda i: (jnp.minimum(i, npt - 1), 0)),
            pl.BlockSpec((tm, c), const),
            pl.BlockSpec((1, c), const),
            pl.BlockSpec((1, c), const),
            pl.BlockSpec((c, c), lambda i: (l, 0), pipeline_mode=pl.Buffered(1)),
            pl.BlockSpec((tm, c), lambda i: (i, cg_block)),
        ],
        out_specs=pl.BlockSpec((tm, c), lambda i: (i, 0)),
        out_shape=jax.ShapeDtypeStruct((m, c), BF16),
        scratch_shapes=[pltpu.VMEM((c, c), BF16)],
        compiler_params=_params(1),
        name="ln_pw",
    )(y_p, y_s, ln_g.reshape(1, -1), ln_b.reshape(1, -1), w_pw2.reshape(depth * c, c), z)


def _outproj_kernel(a_ref, c_ref, as_ref, cs_ref, w_ref, xp_ref, gp_ref, xs_ref, gs_ref, yp_ref, ys_ref, wb,
                    *, n_prompt_tiles, tiles_per_seq, n_batch, s_len):
    i = pl.program_id(1)

    @pl.when(i == 0)
    def _():
        wb[...] = w_ref[...].astype(BF16)

    def project(att_ref, conv_ref):
        o = jnp.dot(att_ref[...], wb[pl.ds(0, D_ATT), :], preferred_element_type=F32)
        return o + jnp.dot(conv_ref[...], wb[pl.ds(D_ATT, D_CONV), :], preferred_element_type=F32)

    @pl.when(i < n_prompt_tiles)
    def _():
        gate = _select_row(gp_ref, i // tiles_per_seq, n_batch)
        yp_ref[...] = xp_ref[...] + gate * project(a_ref, c_ref)

    @pl.when(i == n_prompt_tiles)
    def _():
        gate = jnp.concatenate([gs_ref[...]] * s_len, axis=0)
        ys = xs_ref[...] + gate * project(as_ref, cs_ref)
        if len(ys_ref.shape) == 2:
            ys_ref[...] = ys
        else:
            ns = ys_ref.shape[0]
            for t in range(s_len):
                ys_ref[:, t, :] = ys[t * ns:(t + 1) * ns]


def _outproj(att_p, att_s, conv, w_out, l, xp, xs, mod, n_batch, seq, s_len, batch_major_out, tn=512):
    mp, d = xp.shape
    ms = xs.shape[0]
    ns = ms // s_len
    depth = w_out.shape[0]
    assert mp % TM == 0 and seq % TM == 0 and mp % ms == 0 and ns % MOD_ROWS == 0
    npt = mp // TM
    gate_col0 = 2 * d // tn
    ptile = lambda j, i: (jnp.minimum(i, npt - 1), j)
    ptile_rows = lambda j, i: (jnp.minimum(i, npt - 1), 0)
    stile = lambda j, i: (mp // ms, 0)
    if batch_major_out:
        ys_spec = pl.BlockSpec((ns, s_len, tn), lambda j, i: (0, 0, j))
        ys_shape = jax.ShapeDtypeStruct((ns, s_len, d), F32)
    else:
        ys_spec = pl.BlockSpec((ms, tn), lambda j, i: (0, j))
        ys_shape = jax.ShapeDtypeStruct((ms, d), F32)
    return pl.pallas_call(
        functools.partial(_outproj_kernel, n_prompt_tiles=npt, tiles_per_seq=seq // TM, n_batch=n_batch,
                          s_len=s_len),
        grid=(d // tn, npt + 1),
        in_specs=[
            pl.BlockSpec((TM, D_ATT), ptile_rows),
            pl.BlockSpec((TM, D_CONV), ptile_rows),
            pl.BlockSpec((ms, D_ATT), lambda j, i: (0, 0)),
            pl.BlockSpec((ms, D_CONV), stile),
            pl.BlockSpec((D_ATT + D_CONV, tn), lambda j, i: (l, j)),
            pl.BlockSpec((TM, tn), ptile),
            pl.BlockSpec((MOD_ROWS, tn), lambda j, i: (ns // MOD_ROWS, gate_col0 + j)),
            pl.BlockSpec((ms, tn), lambda j, i: (0, j)),
            pl.BlockSpec((ns, tn), lambda j, i: (0, gate_col0 + j)),
        ],
        out_specs=[pl.BlockSpec((TM, tn), ptile), ys_spec],
        out_shape=[jax.ShapeDtypeStruct((mp, d), F32), ys_shape],
        scratch_shapes=[pltpu.VMEM((D_ATT + D_CONV, tn), BF16)],
        compiler_params=_params(2),
        name="outproj",
    )(att_p, conv, att_s, conv, w_out.reshape(depth * (D_ATT + D_CONV), d), xp, mod, xs, mod)


def kernel(x_prompt, x_sample, c_prompt, c_sample, cache_k_win, cache_v_win, state_conv, w_ada, b_ada,
           norm_g, w_in, q_norm_g, k_norm_g, sinks, conv_w, conv_b, ln_g, ln_b, w_pw2, w_out):
    depth, d, n_in = w_in.shape
    nb, seq, _ = x_prompt.shape
    ns, s_len, _ = x_sample.shape
    mp, ms = nb * seq, ns * s_len
    m_all = mp + ms
    tm_in = m_all // 8
    assert tm_in * 8 == m_all and tm_in % 16 == 0

    xp = x_prompt.reshape(mp, d)
    xs = jnp.transpose(x_sample, (1, 0, 2)).reshape(ms, d)
    pad = (-(ns + nb)) % 16
    c_all = jnp.concatenate([c_sample, c_prompt, jnp.zeros((pad, d), F32)], axis=0)
    w_in2d = w_in.reshape(depth * d, n_in)
    state_tm = jnp.transpose(state_conv, (0, 2, 1, 3))

    kp_l, vp_l, cp_l, ks_l, vs_l, cs_l = [], [], [], [], [], []
    for l in range(depth):
        mod = _modulation(c_all, w_ada, b_ada, l)

        qg = q_norm_g[l].reshape(1, HEAD_DIM)
        kg = k_norm_g[l].reshape(1, HEAD_DIM)
        hkv = _prenorm_kv(xp, norm_g[l], mod, ns, nb, None, w_in2d, l, kg, ms, m_all, 0, zero_tiles=1)
        h, k, v = _prenorm_kv(xs, norm_g[l], mod, 0, None, ns, w_in2d, l, kg, ms, m_all, mp, prev=hkv)
        z = _proj_qgc(h, w_in2d, l, qg, tm_in)
        y, u_s, u_tail = _proj_glu_conv(h, w_in2d, l, conv_w[l], conv_b[l], nb, seq, ms)

        att = _prompt_attention(z, k, v, sinks[l], nb, seq)
        att_s, k_win, v_win = _sample_attention(z[mp:, :D_ATT], k[mp:], v[mp:], z[mp:, D_ATT:2 * D_ATT],
                                                cache_k_win, cache_v_win,
                                                l, sinks[l], ns, s_len)

        y_s, conv_state = _sample_dwconv(state_tm, l, u_s, conv_w[l], conv_b[l])
        conv = _ln_pw(y, y_s, z, 2 * D_ATT // D_CONV, ln_g[l], ln_b[l], w_pw2, l)

        tail = lambda a, rows: jnp.stack(
            [lax.slice_in_dim(a, (b + 1) * seq - rows, (b + 1) * seq, axis=0) for b in range(nb)])
        kp_l.append(tail(k, WINDOW).reshape(nb, WINDOW, N_KV, HEAD_DIM))
        vp_l.append(tail(v, WINDOW).reshape(nb, WINDOW, N_KV, HEAD_DIM))
        cp_l.append(jnp.transpose(u_tail[:, :, HALO - HIST:, :], (0, 2, 1, 3)).reshape(nb, HIST, D_CONV))
        ks_l.append(k_win)
        vs_l.append(v_win)
        cs_l.append(jnp.transpose(conv_state, (1, 0, 2)))

        xp, xs = _outproj(att, att_s, conv, w_out, l, xp, xs, mod, nb, seq, s_len, l == depth - 1)

    return (xp.reshape(nb, seq, d), xs,
            jnp.stack(kp_l), jnp.stack(vp_l), jnp.stack(cp_l),
            jnp.stack(ks_l), jnp.stack(vs_l), jnp.stack(cs_l))
```

```python
import functools

import numpy as np
import jax
import jax.numpy as jnp
from jax import lax
from jax.experimental import pallas as pl
from jax.experimental.pallas import tpu as pltpu

F32 = jnp.float32
BF16 = jnp.bfloat16

D_MODEL = 4096
D_ATT = 2048
D_CONV = 2048
HEAD_DIM = 128
N_HEADS = 16
N_KV = 4
GQA = 4
KV_W = N_KV * HEAD_DIM
WINDOW = 128
CONV_W = 31
HIST = CONV_W - 1
LANES = 128
SUBLANES = 8
HALO = 32
N_SLAB = D_CONV // LANES
RMS_EPS = 1e-6
LN_EPS = 1e-5
NEG_INF = -1e30
ATT_SCALE = HEAD_DIM ** -0.5
LOG2E = float(np.log2(np.e))
OFF_Q, OFF_K, OFF_V, OFF_GA, OFF_CA, OFF_CB, OFF_CG = 0, 2048, 2560, 3072, 5120, 7168, 9216
SLOPES = [float(np.float32(2.0 ** (-8.0 * (h + 1) / N_HEADS))) for h in range(N_HEADS)]

VMEM_PHYSICAL = 64 * 1024 * 1024
VMEM_LIMIT = 58 * 1024 * 1024
TM = 1024
MOD_ROWS = 8


def _params(n_axes, vmem_limit=VMEM_LIMIT):
    return pltpu.CompilerParams(dimension_semantics=("arbitrary",) * n_axes,
                                vmem_limit_bytes=vmem_limit)


def _silu(x):
    return x * jax.nn.sigmoid(x)


def _select_row(ref, idx, count):
    row = ref[0:1, :]
    for n in range(1, count):
        row = jnp.where(idx == n, ref[n:n + 1, :], row)
    return row


def _mod_kernel(c_ref, w_ref, b_ref, o_ref, act, *, tk):
    k = pl.program_id(0)

    @pl.when(k == 0)
    def _():
        act[...] = _silu(c_ref[...]).astype(BF16)
        o_ref[...] = jnp.broadcast_to(b_ref[...], o_ref.shape)

    a = act[:, pl.ds(pl.multiple_of(k * tk, tk), tk)]
    o_ref[...] += jnp.dot(a, w_ref[...].astype(BF16), preferred_element_type=F32)


def _modulation(c_all, w_ada, b_ada, l, tk=256):
    r, d = c_all.shape
    depth, _, n = w_ada.shape
    return pl.pallas_call(
        functools.partial(_mod_kernel, tk=tk),
        grid=(d // tk,),
        in_specs=[
            pl.BlockSpec((r, d), lambda k: (0, 0)),
            pl.BlockSpec((tk, n), lambda k: (l * (d // tk) + k, 0)),
            pl.BlockSpec((None, 1, n), lambda k: (l, 0, 0)),
        ],
        out_specs=pl.BlockSpec((r, n), lambda k: (0, 0)),
        out_shape=jax.ShapeDtypeStruct((r, n), F32),
        scratch_shapes=[pltpu.VMEM((r, d), BF16)],
        compiler_params=_params(1),
        name="modulation",
    )(c_all, w_ada.reshape(depth * d, n), b_ada.reshape(depth, 1, n))


def _headnorm(a, g):
    return a * lax.rsqrt(jnp.mean(a * a, axis=-1, keepdims=True) + RMS_EPS) * g


def _prenorm_kv_kernel(x_ref, g_ref, scale_ref, shift_ref, wk_ref, wv_ref, kg_ref, *rest,
                       tiles_per_batch, n_batch, rows_per_mod, n_tiles):
    h_ref, k_ref, v_ref, wkb, wvb = rest[-5:]

    @pl.when(pl.program_id(0) == 0)
    def _():
        wkb[...] = wk_ref[...].astype(BF16)
        wvb[...] = wv_ref[...].astype(BF16)

    @pl.when(pl.program_id(0) >= n_tiles)
    def _():
        for ref in (h_ref, k_ref, v_ref):
            ref[...] = jnp.zeros(ref.shape, ref.dtype)

    pl.when(pl.program_id(0) < n_tiles)(functools.partial(
        _prenorm_kv_tile, x_ref, g_ref, scale_ref, shift_ref, kg_ref, h_ref, k_ref, v_ref, wkb, wvb,
        tiles_per_batch, n_batch, rows_per_mod))


def _prenorm_kv_tile(x_ref, g_ref, scale_ref, shift_ref, kg_ref, h_ref, k_ref, v_ref, wkb, wvb,
                     tiles_per_batch, n_batch, rows_per_mod):
    x = x_ref[...]
    y = x * lax.rsqrt(jnp.mean(x * x, axis=-1, keepdims=True) + RMS_EPS)
    y = y * g_ref[...]
    if tiles_per_batch is None:
        reps = x.shape[0] // rows_per_mod
        scale = jnp.concatenate([scale_ref[...]] * reps, axis=0)
        shift = jnp.concatenate([shift_ref[...]] * reps, axis=0)
    else:
        batch = pl.program_id(0) // tiles_per_batch
        scale = _select_row(scale_ref, batch, n_batch)
        shift = _select_row(shift_ref, batch, n_batch)
    h = (y * (1.0 + scale) + shift).astype(BF16)
    h_ref[...] = h
    acc = jnp.dot(h, wkb[...], preferred_element_type=F32)
    kg = kg_ref[...]
    for c in range(N_KV):
        sl = slice(c * HEAD_DIM, (c + 1) * HEAD_DIM)
        k_ref[:, sl] = _headnorm(acc[:, sl], kg)
    v_ref[...] = jnp.dot(h, wvb[...], preferred_element_type=F32)


def _prenorm_kv(x2d, g, mod, mod_row0, n_batch, rows_per_mod, w2d, l, kg, tm, m_total, row0, prev=None,
                zero_tiles=0):
    m, d = x2d.shape
    if n_batch is None:
        assert mod_row0 % rows_per_mod == 0 and tm % rows_per_mod == 0
        mk = lambda c: pl.BlockSpec((rows_per_mod, d), lambda i: (mod_row0 // rows_per_mod, c))
        tiles_per_batch = None
    else:
        assert mod_row0 % MOD_ROWS == 0 and n_batch <= MOD_ROWS and (m // n_batch) % tm == 0
        mk = lambda c: pl.BlockSpec((MOD_ROWS, d), lambda i: (mod_row0 // MOD_ROWS, c))
        tiles_per_batch = m // n_batch // tm
    wspec = lambda off: pl.BlockSpec((d, KV_W), lambda i: (l, off // KV_W), pipeline_mode=pl.Buffered(1))
    n_tiles = m // tm
    in_specs = [pl.BlockSpec((tm, d), lambda i: (jnp.minimum(i, n_tiles - 1), 0)),
                pl.BlockSpec((1, d), lambda i: (0, 0)), mk(1), mk(0),
                wspec(OFF_K), wspec(OFF_V), pl.BlockSpec((1, HEAD_DIM), lambda i: (0, 0))]
    args = [x2d, g.reshape(1, d), mod, mod, w2d, w2d, kg]
    aliases = {}
    if prev is not None:
        aliases = {len(args) + n: n for n in range(3)}
        in_specs += [pl.BlockSpec(memory_space=pl.ANY)] * 3
        args += list(prev)
    assert row0 % tm == 0 and m % tm == 0
    rows = lambda i: (row0 // tm + i, 0)
    return pl.pallas_call(
        functools.partial(_prenorm_kv_kernel, tiles_per_batch=tiles_per_batch, n_batch=n_batch,
                          rows_per_mod=rows_per_mod, n_tiles=n_tiles),
        grid=(n_tiles + zero_tiles,),
        in_specs=in_specs,
        out_specs=[pl.BlockSpec((tm, d), rows), pl.BlockSpec((tm, KV_W), rows), pl.BlockSpec((tm, KV_W), rows)],
        out_shape=[jax.ShapeDtypeStruct((m_total, d), BF16), jax.ShapeDtypeStruct((m_total, KV_W), F32),
                   jax.ShapeDtypeStruct((m_total, KV_W), F32)],
        scratch_shapes=[pltpu.VMEM((d, KV_W), BF16), pltpu.VMEM((d, KV_W), BF16)],
        input_output_aliases=aliases,
        compiler_params=_params(1),
        name="prenorm_kv",
    )(*args)


def _proj_qgc_kernel(h_ref, w_hbm, qg_ref, z_ref, stage, wb, sem, *, row0, col_starts, q_tiles):
    j, i = pl.program_id(0), pl.program_id(1)
    d, tn = stage.shape
    n_tiles = len(col_starts)

    def weight_copy(jj):
        col = col_starts[0]
        for n in range(1, n_tiles):
            col = jnp.where(jj == n, col_starts[n], col)
        src = w_hbm.at[pl.ds(row0, d), pl.ds(pl.multiple_of(col, tn), tn)]
        return pltpu.make_async_copy(src, stage, sem)

    @pl.when(i == 0)
    def _():
        @pl.when(j == 0)
        def _():
            weight_copy(j).start()

        weight_copy(j).wait()
        wb[...] = stage[...].astype(BF16)

        @pl.when(j + 1 < n_tiles)
        def _():
            weight_copy(j + 1).start()

    def project():
        return jnp.dot(h_ref[...], wb[...], preferred_element_type=F32)

    @pl.when(j < q_tiles)
    def _():
        acc = project()
        g = qg_ref[...]
        for c in range(tn // HEAD_DIM):
            sl = slice(c * HEAD_DIM, (c + 1) * HEAD_DIM)
            z_ref[:, sl] = _headnorm(acc[:, sl], g).astype(z_ref.dtype)

    @pl.when(j >= q_tiles)
    def _():
        z_ref[...] = _silu(project()).astype(z_ref.dtype)


def _proj_qgc(h, w2d, l, qg, tm, tn=1024):
    m, d = h.shape
    assert m % tm == 0 and D_ATT % tn == 0 and D_CONV % tn == 0
    col_starts = [off + c for off, width in ((OFF_Q, D_ATT), (OFF_GA, D_ATT), (OFF_CG, D_CONV))
                  for c in range(0, width, tn)]
    return pl.pallas_call(
        functools.partial(_proj_qgc_kernel, row0=l * d, col_starts=col_starts, q_tiles=D_ATT // tn),
        grid=(len(col_starts), m // tm),
        in_specs=[pl.BlockSpec((tm, d), lambda j, i: (i, 0)), pl.BlockSpec(memory_space=pl.ANY),
                  pl.BlockSpec((1, HEAD_DIM), lambda j, i: (0, 0))],
        out_specs=pl.BlockSpec((tm, tn), lambda j, i: (i, j)),
        out_shape=jax.ShapeDtypeStruct((m, len(col_starts) * tn), BF16),
        scratch_shapes=[pltpu.VMEM((d, tn), F32), pltpu.VMEM((d, tn), BF16), pltpu.SemaphoreType.DMA],
        compiler_params=_params(2),
        name="proj_qgc",
    )(h, w2d, qg)


def _proj_glu_conv_kernel(h_ref, hs_ref, w_hbm, cw_ref, cb_ref, y_ref, us_ref, ut_ref,
                          stage_a, stage_b, wab, wbb, ubuf0, ubuf1, sem_a, sem_b,
                          *, tm, rc, tiles_per_seq, n_prompt_tiles, row0, col_a, col_b):
    j, i = pl.program_id(0), pl.program_id(1)
    d, tn = stage_a.shape
    nsl = tn // LANES
    base = HALO - HIST

    def weight_copies(jj):
        rows = pl.ds(row0, d)
        return (pltpu.make_async_copy(w_hbm.at[rows, pl.ds(pl.multiple_of(col_a + jj * tn, tn), tn)],
                                      stage_a, sem_a),
                pltpu.make_async_copy(w_hbm.at[rows, pl.ds(pl.multiple_of(col_b + jj * tn, tn), tn)],
                                      stage_b, sem_b))

    @pl.when(i == 0)
    def _():
        @pl.when(j == 0)
        def _():
            for copy in weight_copies(j):
                copy.start()

        for copy in weight_copies(j):
            copy.wait()
        wab[...] = stage_a[...].astype(BF16)
        wbb[...] = stage_b[...].astype(BF16)

        @pl.when(j + 1 < pl.num_programs(0))
        def _():
            for copy in weight_copies(j + 1):
                copy.start()

    @pl.when((i == 0) & (j == 0))
    def _():
        ubuf1[...] = jnp.zeros(ubuf1.shape, F32)

    def conv(prv):
        for c in range(nsl):
            bias = jnp.broadcast_to(cb_ref[c], (rc, LANES))
            for r0 in range(0, tm, rc):
                acc = bias
                for k in range(CONV_W):
                    acc = acc + cw_ref[c, pl.ds(k, 1), :] * prv[c, pl.ds(r0 + base + k, rc), :]
                y_ref[pl.ds(r0, rc), pl.ds(c * LANES, LANES)] = acc

    def glu(h, c0, width):
        cols = pl.ds(c0, width)
        a = jnp.dot(h, wab[:, cols], preferred_element_type=F32)
        b = jnp.dot(h, wbb[:, cols], preferred_element_type=F32)
        return a * jax.nn.sigmoid(b)

    mxu_cols = 2 * LANES

    def prompt_step(cur, prv):
        conv(prv)
        h = h_ref[...]
        seq_start = lax.rem(i, tiles_per_seq) == 0
        for c0 in range(0, tn, mxu_cols):
            u = glu(h, c0, mxu_cols)
            for c in range(c0 // LANES, (c0 + mxu_cols) // LANES):
                cur[c, pl.ds(0, HALO), :] = jnp.where(seq_start, 0.0, prv[c, pl.ds(tm, HALO), :])
                cur[c, pl.ds(HALO, tm), :] = u[:, c * LANES - c0:(c + 1) * LANES - c0]

        @pl.when(lax.rem(i, tiles_per_seq) == tiles_per_seq - 1)
        def _():
            for c in range(nsl):
                ut_ref[c] = cur[c, pl.ds(tm, HALO), :]

    def sample_step(prv):
        conv(prv)
        h = hs_ref[...]
        for c0 in range(0, tn, mxu_cols):
            u = glu(h, c0, mxu_cols)
            for c in range(c0 // LANES, (c0 + mxu_cols) // LANES):
                us_ref[c] = u[:, c * LANES - c0:(c + 1) * LANES - c0]

    is_prompt = i < n_prompt_tiles
    pl.when(is_prompt & (lax.rem(i, 2) == 0))(lambda: prompt_step(ubuf0, ubuf1))
    pl.when(is_prompt & (lax.rem(i, 2) == 1))(lambda: prompt_step(ubuf1, ubuf0))
    pl.when(i == n_prompt_tiles)(lambda: sample_step(ubuf1 if n_prompt_tiles % 2 == 0 else ubuf0))


def _slab_weights(conv_w, conv_b):
    cw = jnp.pad(conv_w, ((0, HALO - CONV_W), (0, 0)))
    cw = jnp.transpose(cw.reshape(HALO, N_SLAB, LANES), (1, 0, 2))
    return cw, conv_b.reshape(N_SLAB, 1, LANES)


def _proj_glu_conv(h, w2d, l, conv_w, conv_b, n_batch, seq, ms, tn=512):
    m, d = h.shape
    mp = n_batch * seq
    assert seq % TM == 0 and mp % ms == 0 and m == mp + ms
    npt = mp // TM
    assert npt % 2 == 0
    nsl = tn // LANES
    cw, cb = _slab_weights(conv_w, conv_b)
    return pl.pallas_call(
        functools.partial(_proj_glu_conv_kernel, tm=TM, rc=32, tiles_per_seq=seq // TM, n_prompt_tiles=npt,
                          row0=l * d, col_a=OFF_CA, col_b=OFF_CB),
        grid=(D_CONV // tn, npt + 1),
        in_specs=[
            pl.BlockSpec((TM, d), lambda j, i: (jnp.minimum(i, npt - 1), 0)),
            pl.BlockSpec((ms, d), lambda j, i: (mp // ms, 0), pipeline_mode=pl.Buffered(1)),
            pl.BlockSpec(memory_space=pl.ANY),
            pl.BlockSpec((nsl, HALO, LANES), lambda j, i: (j, 0, 0)),
            pl.BlockSpec((nsl, 1, LANES), lambda j, i: (j, 0, 0)),
        ],
        out_specs=[
            pl.BlockSpec((TM, tn), lambda j, i: (jnp.maximum(i - 1, 0), j)),
            pl.BlockSpec((nsl, ms, LANES), lambda j, i: (j, 0, 0)),
            pl.BlockSpec((None, nsl, HALO, LANES),
                         lambda j, i: (jnp.minimum(i // (seq // TM), n_batch - 1), j, 0, 0)),
        ],
        out_shape=[
            jax.ShapeDtypeStruct((mp, D_CONV), F32),
            jax.ShapeDtypeStruct((N_SLAB, ms, LANES), F32),
            jax.ShapeDtypeStruct((n_batch, N_SLAB, HALO, LANES), F32),
        ],
        scratch_shapes=[pltpu.VMEM((d, tn), F32), pltpu.VMEM((d, tn), F32),
                        pltpu.VMEM((d, tn), BF16), pltpu.VMEM((d, tn), BF16),
                        pltpu.VMEM((nsl, HALO + TM, LANES), F32), pltpu.VMEM((nsl, HALO + TM, LANES), F32),
                        pltpu.SemaphoreType.DMA, pltpu.SemaphoreType.DMA],
        compiler_params=_params(2, VMEM_PHYSICAL - 512 * 1024),
        name="proj_glu_conv",
    )(h, h, w2d, cw, cb)


def _prompt_attn_kernel(sinks_ref, q_ref, kp_ref, kc_ref, vp_ref, vc_ref, ga_ref, o_ref, bias, *, bq):
    qi = lax.broadcasted_iota(jnp.int32, (WINDOW, 2 * WINDOW), 0)
    kj = lax.broadcasted_iota(jnp.int32, (WINDOW, 2 * WINDOW), 1)

    @pl.when((pl.program_id(0) == 0) & (pl.program_id(1) == 0))
    def _():
        dist = WINDOW + qi - kj
        in_band = (dist >= 0) & (dist < WINDOW)
        distf = dist.astype(F32)
        for h in range(N_HEADS):
            bias[h] = jnp.where(in_band, (-SLOPES[h] * LOG2E) * distf, NEG_INF)

    has_prev = kj >= WINDOW * (pl.program_id(1) == 0).astype(jnp.int32)
    nsub = bq // WINDOW
    k_all = jnp.concatenate([kp_ref[...], kc_ref[...]], axis=0).astype(BF16)
    v_all = jnp.concatenate([vp_ref[...], vc_ref[...]], axis=0).astype(BF16)
    for s in range(nsub):
        rows = slice(s * WINDOW, (s + 1) * WINDOW)
        for kv in range(N_KV):
            cols = slice(kv * HEAD_DIM, (kv + 1) * HEAD_DIM)
            kb = k_all[s * WINDOW:(s + 2) * WINDOW, cols]
            vb = v_all[s * WINDOW:(s + 2) * WINDOW, cols]
            for g in range(GQA):
                h = kv * GQA + g
                hc = slice(h * HEAD_DIM, (h + 1) * HEAD_DIM)
                sc = lax.dot_general(q_ref[rows, hc], kb, (((1,), (1,)), ((), ())),
                                     preferred_element_type=F32)
                t = sc * (ATT_SCALE * LOG2E) + bias[h]
                if s == 0:
                    t = jnp.where(has_prev, t, NEG_INF)
                sink = sinks_ref[h] * LOG2E
                m = jnp.maximum(jnp.max(t, axis=-1, keepdims=True), sink)
                p = jnp.exp2(t - m)
                denom = jnp.sum(p, axis=-1, keepdims=True) + jnp.exp2(sink - m)
                o = jnp.dot(p.astype(BF16), vb, preferred_element_type=F32) / denom
                o_ref[rows, hc] = (o * ga_ref[rows, hc].astype(F32)).astype(o_ref.dtype)


def _prompt_attention(z, k, v, sinks, n_batch, seq, bq=512):
    m = n_batch * seq
    nb = seq // bq
    ratio = bq // WINDOW
    cur = lambda n, i, s: (n * nb + i, 0)
    prev = lambda n, i, s: (jnp.maximum((n * nb + i) * ratio - 1, 0), 0)
    return pl.pallas_call(
        functools.partial(_prompt_attn_kernel, bq=bq),
        grid_spec=pltpu.PrefetchScalarGridSpec(
            num_scalar_prefetch=1,
            grid=(n_batch, nb),
            in_specs=[
                pl.BlockSpec((bq, D_ATT), cur),
                pl.BlockSpec((WINDOW, KV_W), prev),
                pl.BlockSpec((bq, KV_W), cur),
                pl.BlockSpec((WINDOW, KV_W), prev),
                pl.BlockSpec((bq, KV_W), cur),
                pl.BlockSpec((bq, D_ATT), lambda n, i, s: (n * nb + i, 1)),
            ],
            out_specs=pl.BlockSpec((bq, D_ATT), cur),
            scratch_shapes=[pltpu.VMEM((N_HEADS, WINDOW, 2 * WINDOW), F32)],
        ),
        out_shape=jax.ShapeDtypeStruct((m, D_ATT), BF16),
        compiler_params=_params(2),
        name="prompt_attention",
    )(sinks, z, k, k, v, v, z)


def _sample_attn_kernel(slope_ref, sink_ref, qg_ref, kc_ref, vc_ref, new_ref,
                        o_ref, ko_ref, vo_ref, *, bn, s_len):
    q_ref, ga_ref = qg_ref.at[0], qg_ref.at[1]
    kn_ref, vn_ref = new_ref.at[0], new_ref.at[1]
    nk = WINDOW + SUBLANES
    r = GQA * s_len
    rows = bn * N_KV * r
    ri = lax.broadcasted_iota(jnp.int32, (rows, nk), 0)
    kj = lax.broadcasted_iota(jnp.int32, (rows, nk), 1)
    dist = lax.rem(ri, s_len) + WINDOW - kj
    valid = (dist >= 0) & (dist < WINDOW)
    shift = s_len * N_KV
    keep = WINDOW * N_KV - shift

    def head_rows(win_ref, new_ref, b, kv):
        win = win_ref.at[b][pl.ds(kv, WINDOW, stride=N_KV), :]
        new = new_ref.at[b][pl.ds(kv, SUBLANES, stride=N_KV), :]
        return jnp.concatenate([win, new], axis=0).astype(BF16)

    chains = [(b, kv) for b in range(bn) for kv in range(N_KV)]
    sc = jnp.concatenate(
        [lax.dot_general(q_ref[pl.ds(c * r, r), :], head_rows(kc_ref, kn_ref, b, kv),
                         (((1,), (1,)), ((), ())), preferred_element_type=F32)
         for c, (b, kv) in enumerate(chains)], axis=0)
    sc = sc * ATT_SCALE - slope_ref[...] * dist.astype(F32)
    sc = jnp.where(valid, sc, NEG_INF)
    sink = sink_ref[...]
    m = jnp.maximum(jnp.max(sc, axis=-1, keepdims=True), sink)
    p = jnp.exp(sc - m)
    denom = jnp.sum(p, axis=-1, keepdims=True) + jnp.exp(sink - m)
    p = p.astype(BF16)
    o = jnp.concatenate(
        [jnp.dot(p[c * r:(c + 1) * r], head_rows(vc_ref, vn_ref, b, kv), preferred_element_type=F32)
         for c, (b, kv) in enumerate(chains)], axis=0)
    o_ref[...] = (o / denom * ga_ref[...].astype(F32)).astype(o_ref.dtype)

    for b in range(bn):
        for win_ref, new_ref, out_ref in ((kc_ref, kn_ref, ko_ref), (vc_ref, vn_ref, vo_ref)):
            out_ref[b, pl.ds(0, keep), :] = win_ref[b, pl.ds(shift, keep), :]
            out_ref[b, pl.ds(keep, shift), :] = new_ref[b, pl.ds(0, shift), :]


def _to_head_rows(a, n, s_len):
    planes = a.shape[1] // D_ATT
    a = a.reshape(s_len, n, planes, N_KV, GQA, HEAD_DIM)
    return jnp.transpose(a, (2, 1, 3, 4, 0, 5)).reshape(planes, n * N_KV * GQA * s_len, HEAD_DIM)


def _from_head_rows(a, n, s_len):
    a = a.reshape(n, N_KV, GQA, s_len, HEAD_DIM)
    return jnp.transpose(a, (3, 0, 1, 2, 4)).reshape(s_len * n, D_ATT)


def _sample_attention(qg, k_new, v_new, cache_k, cache_v, l, sinks, n, s_len, bn=8):
    assert s_len <= SUBLANES and n % bn == 0
    r = GQA * s_len
    depth = cache_k.shape[0]
    rows = bn * N_KV * r
    per_row = lambda a: jnp.broadcast_to(a.astype(F32).reshape(1, N_HEADS, 1, 1),
                                         (bn, N_HEADS, s_len, 1)).reshape(rows, 1)
    slope_rows = per_row(jnp.asarray(SLOPES, F32))
    sink_rows = per_row(sinks)
    rows_new = SUBLANES * N_KV

    new = jnp.stack([k_new, v_new]).reshape(2, s_len, n, N_KV, HEAD_DIM)
    new = jnp.transpose(new, (0, 2, 1, 3, 4)).reshape(2, n, s_len * N_KV, HEAD_DIM)
    new = jnp.pad(new, ((0, 0), (0, 0), (0, rows_new - s_len * N_KV), (0, 0)))

    win_rows = lambda c: c.reshape(depth * n, WINDOW * N_KV, HEAD_DIM)
    heads = pl.BlockSpec((rows, HEAD_DIM), lambda i: (i, 0))
    heads2 = pl.BlockSpec((2, rows, HEAD_DIM), lambda i: (0, i, 0))
    win_in = pl.BlockSpec((bn, WINDOW * N_KV, HEAD_DIM), lambda i: (l * (n // bn) + i, 0, 0))
    win_out = pl.BlockSpec((bn, WINDOW * N_KV, HEAD_DIM), lambda i: (i, 0, 0))
    new_spec = pl.BlockSpec((2, bn, rows_new, HEAD_DIM), lambda i: (0, i, 0, 0))
    small = pl.BlockSpec((rows, 1), lambda i: (0, 0))
    win_shape = jax.ShapeDtypeStruct((n, WINDOW * N_KV, HEAD_DIM), F32)
    o, k_win, v_win = pl.pallas_call(
        functools.partial(_sample_attn_kernel, bn=bn, s_len=s_len),
        grid=(n // bn,),
        in_specs=[small, small, heads2, win_in, win_in, new_spec],
        out_specs=[heads, win_out, win_out],
        out_shape=[jax.ShapeDtypeStruct((n * N_KV * r, HEAD_DIM), BF16), win_shape, win_shape],
        compiler_params=_params(1),
        name="sample_attention",
    )(slope_rows, sink_rows, _to_head_rows(qg, n, s_len), win_rows(cache_k), win_rows(cache_v), new)
    win5 = lambda a: a.reshape(n, WINDOW, N_KV, HEAD_DIM)
    return _from_head_rows(o, n, s_len), win5(k_win), win5(v_win)


def _sample_dwconv_kernel(st_ref, u_ref, cw_ref, cb_ref, y_ref, so_ref, *, s_len, n):
    half = n // 2
    for s in range(u_ref.shape[0]):
        lanes = pl.ds(s * LANES, LANES)
        for r0 in range(0, n, half):
            rows = pl.ds(r0, half)
            ext = lambda j: st_ref[j, rows, lanes] if j < HIST else u_ref[s, pl.ds((j - HIST) * n + r0, half), :]
            acc = [jnp.broadcast_to(cb_ref[:, lanes], (half, LANES)) for _ in range(s_len)]
            for j in range(HIST + s_len):
                x = ext(j)
                for t in range(max(0, j - HIST), min(s_len - 1, j) + 1):
                    acc[t] = acc[t] + cw_ref[pl.ds(j - t, 1), lanes] * x
                if j >= s_len:
                    so_ref[j - s_len, rows, lanes] = x
            for t in range(s_len):
                y_ref[pl.ds(t * n + r0, half), lanes] = acc[t]


def _sample_dwconv(state_tm, l, u_slab, conv_w, conv_b, tc=512):
    depth, _, n, c = state_tm.shape
    ms = u_slab.shape[1]
    s_len = ms // n
    assert n % SUBLANES == 0
    y, new_state = pl.pallas_call(
        functools.partial(_sample_dwconv_kernel, s_len=s_len, n=n),
        grid=(c // tc,),
        in_specs=[
            pl.BlockSpec((None, HIST, n, tc), lambda j: (l, 0, 0, j)),
            pl.BlockSpec((tc // LANES, ms, LANES), lambda j: (j, 0, 0)),
            pl.BlockSpec((CONV_W, tc), lambda j: (0, j)),
            pl.BlockSpec((1, tc), lambda j: (0, j)),
        ],
        out_specs=[pl.BlockSpec((ms, tc), lambda j: (0, j)),
                   pl.BlockSpec((HIST, n, tc), lambda j: (0, 0, j))],
        out_shape=[jax.ShapeDtypeStruct((ms, c), F32), jax.ShapeDtypeStruct((HIST, n, c), F32)],
        compiler_params=_params(1),
        name="sample_dwconv",
    )(state_tm, u_slab, conv_w, conv_b.reshape(1, c))
    return y, new_state


def _ln_pw_kernel(yp_ref, ys_ref, lng_ref, lnb_ref, wpw_ref, cg_ref, o_ref, wpb, *, n_prompt_tiles):
    @pl.when(pl.program_id(0) == 0)
    def _():
        wpb[...] = wpw_ref[...].astype(BF16)

    y = jnp.where(pl.program_id(0) < n_prompt_tiles, yp_ref[...], ys_ref[...])
    mu = jnp.mean(y, axis=-1, keepdims=True)
    yc = y - mu
    var = jnp.mean(yc * yc, axis=-1, keepdims=True)
    yn = yc * lax.rsqrt(var + LN_EPS) * lng_ref[...] + lnb_ref[...]
    a = _silu(yn).astype(BF16)
    o = jnp.dot(a, wpb[...], preferred_element_type=F32)
    o_ref[...] = (o * cg_ref[...].astype(F32)).astype(o_ref.dtype)


def _ln_pw(y_p, y_s, z, cg_block, ln_g, ln_b, w_pw2, l):
    mp, c = y_p.shape
    tm = y_s.shape[0]
    m = mp + tm
    depth = w_pw2.shape[0]
    assert mp % tm == 0 and z.shape[0] == m
    npt = mp // tm
    const = lambda i: (0, 0)
    return pl.pallas_call(
        functools.partial(_ln_pw_kernel, n_prompt_tiles=npt),
        grid=(npt + 1,),
        in_specs=[
            pl.BlockSpec((tm, c), lambda i: (jnp.minimum(i, npt - 1), 0)),
            pl.BlockSpec((tm, c), const),
            pl.BlockSpec((1, c), const),
            pl.BlockSpec((1, c), const),
            pl.BlockSpec((c, c), lambda i: (l, 0), pipeline_mode=pl.Buffered(1)),
            pl.BlockSpec((tm, c), lambda i: (i, cg_block)),
        ],
        out_specs=pl.BlockSpec((tm, c), lambda i: (i, 0)),
        out_shape=jax.ShapeDtypeStruct((m, c), BF16),
        scratch_shapes=[pltpu.VMEM((c, c), BF16)],
        compiler_params=_params(1),
        name="ln_pw",
    )(y_p, y_s, ln_g.reshape(1, -1), ln_b.reshape(1, -1), w_pw2.reshape(depth * c, c), z)


def _outproj_kernel(a_ref, c_ref, as_ref, cs_ref, w_ref, xp_ref, gp_ref, xs_ref, gs_ref, yp_ref, ys_ref, wb,
                    *, n_prompt_tiles, tiles_per_seq, n_batch, s_len):
    i = pl.program_id(1)

    @pl.when(i == 0)
    def _():
        wb[...] = w_ref[...].astype(BF16)

    def project(att_ref, conv_ref):
        o = jnp.dot(att_ref[...], wb[pl.ds(0, D_ATT), :], preferred_element_type=F32)
        return o + jnp.dot(conv_ref[...], wb[pl.ds(D_ATT, D_CONV), :], preferred_element_type=F32)

    @pl.when(i < n_prompt_tiles)
    def _():
        gate = _select_row(gp_ref, i // tiles_per_seq, n_batch)
        yp_ref[...] = xp_ref[...] + gate * project(a_ref, c_ref)

    @pl.when(i == n_prompt_tiles)
    def _():
        gate = jnp.concatenate([gs_ref[...]] * s_len, axis=0)
        ys = xs_ref[...] + gate * project(as_ref, cs_ref)
        if len(ys_ref.shape) == 2:
            ys_ref[...] = ys
        else:
            ns = ys_ref.shape[0]
            for t in range(s_len):
                ys_ref[:, t, :] = ys[t * ns:(t + 1) * ns]


def _outproj(att_p, att_s, conv, w_out, l, xp, xs, mod, n_batch, seq, s_len, batch_major_out, tn=512):
    mp, d = xp.shape
    ms = xs.shape[0]
    ns = ms // s_len
    depth = w_out.shape[0]
    assert mp % TM == 0 and seq % TM == 0 and mp % ms == 0 and ns % MOD_ROWS == 0
    npt = mp // TM
    gate_col0 = 2 * d // tn
    ptile = lambda j, i: (jnp.minimum(i, npt - 1), j)
    ptile_rows = lambda j, i: (jnp.minimum(i, npt - 1), 0)
    stile = lambda j, i: (mp // ms, 0)
    if batch_major_out:
        ys_spec = pl.BlockSpec((ns, s_len, tn), lambda j, i: (0, 0, j))
        ys_shape = jax.ShapeDtypeStruct((ns, s_len, d), F32)
    else:
        ys_spec = pl.BlockSpec((ms, tn), lambda j, i: (0, j))
        ys_shape = jax.ShapeDtypeStruct((ms, d), F32)
    return pl.pallas_call(
        functools.partial(_outproj_kernel, n_prompt_tiles=npt, tiles_per_seq=seq // TM, n_batch=n_batch,
                          s_len=s_len),
        grid=(d // tn, npt + 1),
        in_specs=[
            pl.BlockSpec((TM, D_ATT), ptile_rows),
            pl.BlockSpec((TM, D_CONV), ptile_rows),
            pl.BlockSpec((ms, D_ATT), lambda j, i: (0, 0)),
            pl.BlockSpec((ms, D_CONV), stile),
            pl.BlockSpec((D_ATT + D_CONV, tn), lambda j, i: (l, j)),
            pl.BlockSpec((TM, tn), ptile),
            pl.BlockSpec((MOD_ROWS, tn), lambda j, i: (ns // MOD_ROWS, gate_col0 + j)),
            pl.BlockSpec((ms, tn), lambda j, i: (0, j)),
            pl.BlockSpec((ns, tn), lambda j, i: (0, gate_col0 + j)),
        ],
        out_specs=[pl.BlockSpec((TM, tn), ptile), ys_spec],
        out_shape=[jax.ShapeDtypeStruct((mp, d), F32), ys_shape],
        scratch_shapes=[pltpu.VMEM((D_ATT + D_CONV, tn), BF16)],
        compiler_params=_params(2),
        name="outproj",
    )(att_p, conv, att_s, conv, w_out.reshape(depth * (D_ATT + D_CONV), d), xp, mod, xs, mod)


def kernel(x_prompt, x_sample, c_prompt, c_sample, cache_k_win, cache_v_win, state_conv, w_ada, b_ada,
           norm_g, w_in, q_norm_g, k_norm_g, sinks, conv_w, conv_b, ln_g, ln_b, w_pw2, w_out):
    depth, d, n_in = w_in.shape
    nb, seq, _ = x_prompt.shape
    ns, s_len, _ = x_sample.shape
    mp, ms = nb * seq, ns * s_len
    m_all = mp + ms
    tm_in = m_all // 8
    assert tm_in * 8 == m_all and tm_in % 16 == 0

    xp = x_prompt.reshape(mp, d)
    xs = jnp.transpose(x_sample, (1, 0, 2)).reshape(ms, d)
    pad = (-(ns + nb)) % 16
    c_all = jnp.concatenate([c_sample, c_prompt, jnp.zeros((pad, d), F32)], axis=0)
    w_in2d = w_in.reshape(depth * d, n_in)
    state_tm = jnp.transpose(state_conv, (0, 2, 1, 3))

    kp_l, vp_l, cp_l, ks_l, vs_l, cs_l = [], [], [], [], [], []
    for l in range(depth):
        mod = _modulation(c_all, w_ada, b_ada, l)

        qg = q_norm_g[l].reshape(1, HEAD_DIM)
        kg = k_norm_g[l].reshape(1, HEAD_DIM)
        hkv = _prenorm_kv(xp, norm_g[l], mod, ns, nb, None, w_in2d, l, kg, ms, m_all, 0, zero_tiles=1)
        h, k, v = _prenorm_kv(xs, norm_g[l], mod, 0, None, ns, w_in2d, l, kg, ms, m_all, mp, prev=hkv)
        z = _proj_qgc(h, w_in2d, l, qg, tm_in)
        y, u_s, u_tail = _proj_glu_conv(h, w_in2d, l, conv_w[l], conv_b[l], nb, seq, ms)

        att = _prompt_attention(z, k, v, sinks[l], nb, seq)
        att_s, k_win, v_win = _sample_attention(z[mp:, :2 * D_ATT], k[mp:], v[mp:], cache_k_win, cache_v_win,
                                                l, sinks[l], ns, s_len)

        y_s, conv_state = _sample_dwconv(state_tm, l, u_s, conv_w[l], conv_b[l])
        conv = _ln_pw(y, y_s, z, 2 * D_ATT // D_CONV, ln_g[l], ln_b[l], w_pw2, l)

        tail = lambda a, rows: jnp.stack(
            [lax.slice_in_dim(a, (b + 1) * seq - rows, (b + 1) * seq, axis=0) for b in range(nb)])
        kp_l.append(tail(k, WINDOW).reshape(nb, WINDOW, N_KV, HEAD_DIM))
        vp_l.append(tail(v, WINDOW).reshape(nb, WINDOW, N_KV, HEAD_DIM))
        cp_l.append(jnp.transpose(u_tail[:, :, HALO - HIST:, :], (0, 2, 1, 3)).reshape(nb, HIST, D_CONV))
        ks_l.append(k_win)
        vs_l.append(v_win)
        cs_l.append(jnp.transpose(conv_state, (1, 0, 2)))

        xp, xs = _outproj(att, att_s, conv, w_out, l, xp, xs, mod, nb, seq, s_len, l == depth - 1)

    return (xp.reshape(nb, seq, d), xs,
            jnp.stack(kp_l), jnp.stack(vp_l), jnp.stack(cp_l),
            jnp.stack(ks_l), jnp.stack(vs_l), jnp.stack(cs_l))
```

```python
import functools

import numpy as np
import jax
import jax.numpy as jnp
from jax import lax
from jax.experimental import pallas as pl
from jax.experimental.pallas import tpu as pltpu

F32 = jnp.float32
BF16 = jnp.bfloat16

D_MODEL = 4096
D_ATT = 2048
D_CONV = 2048
HEAD_DIM = 128
N_HEADS = 16
N_KV = 4
GQA = 4
KV_W = N_KV * HEAD_DIM
WINDOW = 128
CONV_W = 31
HIST = CONV_W - 1
LANES = 128
SUBLANES = 8
HALO = 32
N_SLAB = D_CONV // LANES
RMS_EPS = 1e-6
LN_EPS = 1e-5
NEG_INF = -1e30
ATT_SCALE = HEAD_DIM ** -0.5
LOG2E = float(np.log2(np.e))
OFF_Q, OFF_K, OFF_V, OFF_GA, OFF_CA, OFF_CB, OFF_CG = 0, 2048, 2560, 3072, 5120, 7168, 9216
SLOPES = [float(np.float32(2.0 ** (-8.0 * (h + 1) / N_HEADS))) for h in range(N_HEADS)]

VMEM_PHYSICAL = 64 * 1024 * 1024
VMEM_LIMIT = 58 * 1024 * 1024
TM = 1024
MOD_ROWS = 8


def _params(n_axes, vmem_limit=VMEM_LIMIT):
    return pltpu.CompilerParams(dimension_semantics=("arbitrary",) * n_axes,
                                vmem_limit_bytes=vmem_limit)


def _silu(x):
    return x * jax.nn.sigmoid(x)


def _select_row(ref, idx, count):
    row = ref[0:1, :]
    for n in range(1, count):
        row = jnp.where(idx == n, ref[n:n + 1, :], row)
    return row


def _mod_kernel(c_ref, w_ref, b_ref, o_ref, act, *, tk):
    k = pl.program_id(0)

    @pl.when(k == 0)
    def _():
        act[...] = _silu(c_ref[...]).astype(BF16)
        o_ref[...] = jnp.broadcast_to(b_ref[...], o_ref.shape)

    a = act[:, pl.ds(pl.multiple_of(k * tk, tk), tk)]
    o_ref[...] += jnp.dot(a, w_ref[...].astype(BF16), preferred_element_type=F32)


def _modulation(c_all, w_ada, b_ada, l, tk=256):
    r, d = c_all.shape
    depth, _, n = w_ada.shape
    return pl.pallas_call(
        functools.partial(_mod_kernel, tk=tk),
        grid=(d // tk,),
        in_specs=[
            pl.BlockSpec((r, d), lambda k: (0, 0)),
            pl.BlockSpec((tk, n), lambda k: (l * (d // tk) + k, 0)),
            pl.BlockSpec((None, 1, n), lambda k: (l, 0, 0)),
        ],
        out_specs=pl.BlockSpec((r, n), lambda k: (0, 0)),
        out_shape=jax.ShapeDtypeStruct((r, n), F32),
        scratch_shapes=[pltpu.VMEM((r, d), BF16)],
        compiler_params=_params(1),
        name="modulation",
    )(c_all, w_ada.reshape(depth * d, n), b_ada.reshape(depth, 1, n))


def _headnorm(a, g):
    return a * lax.rsqrt(jnp.mean(a * a, axis=-1, keepdims=True) + RMS_EPS) * g


def _prenorm_kv_kernel(x_ref, g_ref, scale_ref, shift_ref, wk_ref, wv_ref, kg_ref, *rest,
                       tiles_per_batch, n_batch, rows_per_mod, n_tiles):
    h_ref, k_ref, v_ref, wkb, wvb = rest[-5:]

    @pl.when(pl.program_id(0) == 0)
    def _():
        wkb[...] = wk_ref[...].astype(BF16)
        wvb[...] = wv_ref[...].astype(BF16)

    @pl.when(pl.program_id(0) >= n_tiles)
    def _():
        for ref in (h_ref, k_ref, v_ref):
            ref[...] = jnp.zeros(ref.shape, ref.dtype)

    pl.when(pl.program_id(0) < n_tiles)(functools.partial(
        _prenorm_kv_tile, x_ref, g_ref, scale_ref, shift_ref, kg_ref, h_ref, k_ref, v_ref, wkb, wvb,
        tiles_per_batch, n_batch, rows_per_mod))


def _prenorm_kv_tile(x_ref, g_ref, scale_ref, shift_ref, kg_ref, h_ref, k_ref, v_ref, wkb, wvb,
                     tiles_per_batch, n_batch, rows_per_mod):
    x = x_ref[...]
    y = x * lax.rsqrt(jnp.mean(x * x, axis=-1, keepdims=True) + RMS_EPS)
    y = y * g_ref[...]
    if tiles_per_batch is None:
        reps = x.shape[0] // rows_per_mod
        scale = jnp.concatenate([scale_ref[...]] * reps, axis=0)
        shift = jnp.concatenate([shift_ref[...]] * reps, axis=0)
    else:
        batch = pl.program_id(0) // tiles_per_batch
        scale = _select_row(scale_ref, batch, n_batch)
        shift = _select_row(shift_ref, batch, n_batch)
    h = (y * (1.0 + scale) + shift).astype(BF16)
    h_ref[...] = h
    acc = jnp.dot(h, wkb[...], preferred_element_type=F32)
    kg = kg_ref[...]
    for c in range(N_KV):
        sl = slice(c * HEAD_DIM, (c + 1) * HEAD_DIM)
        k_ref[:, sl] = _headnorm(acc[:, sl], kg)
    v_ref[...] = jnp.dot(h, wvb[...], preferred_element_type=F32)


def _prenorm_kv(x2d, g, mod, mod_row0, n_batch, rows_per_mod, w2d, l, kg, tm, m_total, row0, prev=None,
                zero_tiles=0):
    m, d = x2d.shape
    if n_batch is None:
        assert mod_row0 % rows_per_mod == 0 and tm % rows_per_mod == 0
        mk = lambda c: pl.BlockSpec((rows_per_mod, d), lambda i: (mod_row0 // rows_per_mod, c))
        tiles_per_batch = None
    else:
        assert mod_row0 % MOD_ROWS == 0 and n_batch <= MOD_ROWS and (m // n_batch) % tm == 0
        mk = lambda c: pl.BlockSpec((MOD_ROWS, d), lambda i: (mod_row0 // MOD_ROWS, c))
        tiles_per_batch = m // n_batch // tm
    wspec = lambda off: pl.BlockSpec((d, KV_W), lambda i: (l, off // KV_W), pipeline_mode=pl.Buffered(1))
    n_tiles = m // tm
    in_specs = [pl.BlockSpec((tm, d), lambda i: (jnp.minimum(i, n_tiles - 1), 0)),
                pl.BlockSpec((1, d), lambda i: (0, 0)), mk(1), mk(0),
                wspec(OFF_K), wspec(OFF_V), pl.BlockSpec((1, HEAD_DIM), lambda i: (0, 0))]
    args = [x2d, g.reshape(1, d), mod, mod, w2d, w2d, kg]
    aliases = {}
    if prev is not None:
        aliases = {len(args) + n: n for n in range(3)}
        in_specs += [pl.BlockSpec(memory_space=pl.ANY)] * 3
        args += list(prev)
    assert row0 % tm == 0 and m % tm == 0
    rows = lambda i: (row0 // tm + i, 0)
    return pl.pallas_call(
        functools.partial(_prenorm_kv_kernel, tiles_per_batch=tiles_per_batch, n_batch=n_batch,
                          rows_per_mod=rows_per_mod, n_tiles=n_tiles),
        grid=(n_tiles + zero_tiles,),
        in_specs=in_specs,
        out_specs=[pl.BlockSpec((tm, d), rows), pl.BlockSpec((tm, KV_W), rows), pl.BlockSpec((tm, KV_W), rows)],
        out_shape=[jax.ShapeDtypeStruct((m_total, d), BF16), jax.ShapeDtypeStruct((m_total, KV_W), F32),
                   jax.ShapeDtypeStruct((m_total, KV_W), F32)],
        scratch_shapes=[pltpu.VMEM((d, KV_W), BF16), pltpu.VMEM((d, KV_W), BF16)],
        input_output_aliases=aliases,
        compiler_params=_params(1),
        name="prenorm_kv",
    )(*args)


def _proj_qgc_kernel(h_ref, w_hbm, qg_ref, z_ref, stage, wb, sem, *, row0, col_starts, q_tiles):
    j, i = pl.program_id(0), pl.program_id(1)
    d, tn = stage.shape
    n_tiles = len(col_starts)

    def weight_copy(jj):
        col = col_starts[0]
        for n in range(1, n_tiles):
            col = jnp.where(jj == n, col_starts[n], col)
        src = w_hbm.at[pl.ds(row0, d), pl.ds(pl.multiple_of(col, tn), tn)]
        return pltpu.make_async_copy(src, stage, sem)

    @pl.when(i == 0)
    def _():
        @pl.when(j == 0)
        def _():
            weight_copy(j).start()

        weight_copy(j).wait()
        wb[...] = stage[...].astype(BF16)

        @pl.when(j + 1 < n_tiles)
        def _():
            weight_copy(j + 1).start()

    def project():
        return jnp.dot(h_ref[...], wb[...], preferred_element_type=F32)

    @pl.when(j < q_tiles)
    def _():
        acc = project()
        g = qg_ref[...]
        for c in range(tn // HEAD_DIM):
            sl = slice(c * HEAD_DIM, (c + 1) * HEAD_DIM)
            z_ref[:, sl] = _headnorm(acc[:, sl], g).astype(z_ref.dtype)

    @pl.when(j >= q_tiles)
    def _():
        z_ref[...] = _silu(project()).astype(z_ref.dtype)


def _proj_qgc(h, w2d, l, qg, tm, tn=1024):
    m, d = h.shape
    assert m % tm == 0 and D_ATT % tn == 0 and D_CONV % tn == 0
    col_starts = [off + c for off, width in ((OFF_Q, D_ATT), (OFF_GA, D_ATT), (OFF_CG, D_CONV))
                  for c in range(0, width, tn)]
    return pl.pallas_call(
        functools.partial(_proj_qgc_kernel, row0=l * d, col_starts=col_starts, q_tiles=D_ATT // tn),
        grid=(len(col_starts), m // tm),
        in_specs=[pl.BlockSpec((tm, d), lambda j, i: (i, 0)), pl.BlockSpec(memory_space=pl.ANY),
                  pl.BlockSpec((1, HEAD_DIM), lambda j, i: (0, 0))],
        out_specs=pl.BlockSpec((tm, tn), lambda j, i: (i, j)),
        out_shape=jax.ShapeDtypeStruct((m, len(col_starts) * tn), BF16),
        scratch_shapes=[pltpu.VMEM((d, tn), F32), pltpu.VMEM((d, tn), BF16), pltpu.SemaphoreType.DMA],
        compiler_params=_params(2),
        name="proj_qgc",
    )(h, w2d, qg)


def _proj_glu_conv_kernel(h_ref, hs_ref, w_hbm, cw_ref, cb_ref, y_ref, us_ref, ut_ref,
                          stage_a, stage_b, wab, wbb, ubuf0, ubuf1, sem_a, sem_b,
                          *, tm, rc, tiles_per_seq, n_prompt_tiles, row0, col_a, col_b):
    j, i = pl.program_id(0), pl.program_id(1)
    d, tn = stage_a.shape
    nsl = tn // LANES
    base = HALO - HIST

    def weight_copies(jj):
        rows = pl.ds(row0, d)
        return (pltpu.make_async_copy(w_hbm.at[rows, pl.ds(pl.multiple_of(col_a + jj * tn, tn), tn)],
                                      stage_a, sem_a),
                pltpu.make_async_copy(w_hbm.at[rows, pl.ds(pl.multiple_of(col_b + jj * tn, tn), tn)],
                                      stage_b, sem_b))

    @pl.when(i == 0)
    def _():
        @pl.when(j == 0)
        def _():
            for copy in weight_copies(j):
                copy.start()

        for copy in weight_copies(j):
            copy.wait()
        wab[...] = stage_a[...].astype(BF16)
        wbb[...] = stage_b[...].astype(BF16)

        @pl.when(j + 1 < pl.num_programs(0))
        def _():
            for copy in weight_copies(j + 1):
                copy.start()

    @pl.when((i == 0) & (j == 0))
    def _():
        ubuf1[...] = jnp.zeros(ubuf1.shape, F32)

    def conv(prv):
        for c in range(nsl):
            bias = jnp.broadcast_to(cb_ref[c], (rc, LANES))
            for r0 in range(0, tm, rc):
                acc = bias
                for k in range(CONV_W):
                    acc = acc + cw_ref[c, pl.ds(k, 1), :] * prv[c, pl.ds(r0 + base + k, rc), :]
                y_ref[pl.ds(r0, rc), pl.ds(c * LANES, LANES)] = acc

    def glu(h, c0, width):
        cols = pl.ds(c0, width)
        a = jnp.dot(h, wab[:, cols], preferred_element_type=F32)
        b = jnp.dot(h, wbb[:, cols], preferred_element_type=F32)
        return a * jax.nn.sigmoid(b)

    mxu_cols = 2 * LANES

    def prompt_step(cur, prv):
        conv(prv)
        h = h_ref[...]
        seq_start = lax.rem(i, tiles_per_seq) == 0
        for c0 in range(0, tn, mxu_cols):
            u = glu(h, c0, mxu_cols)
            for c in range(c0 // LANES, (c0 + mxu_cols) // LANES):
                cur[c, pl.ds(0, HALO), :] = jnp.where(seq_start, 0.0, prv[c, pl.ds(tm, HALO), :])
                cur[c, pl.ds(HALO, tm), :] = u[:, c * LANES - c0:(c + 1) * LANES - c0]

        @pl.when(lax.rem(i, tiles_per_seq) == tiles_per_seq - 1)
        def _():
            for c in range(nsl):
                ut_ref[c] = cur[c, pl.ds(tm, HALO), :]

    def sample_step(prv):
        conv(prv)
        h = hs_ref[...]
        for c0 in range(0, tn, mxu_cols):
            u = glu(h, c0, mxu_cols)
            for c in range(c0 // LANES, (c0 + mxu_cols) // LANES):
                us_ref[c] = u[:, c * LANES - c0:(c + 1) * LANES - c0]

    is_prompt = i < n_prompt_tiles
    pl.when(is_prompt & (lax.rem(i, 2) == 0))(lambda: prompt_step(ubuf0, ubuf1))
    pl.when(is_prompt & (lax.rem(i, 2) == 1))(lambda: prompt_step(ubuf1, ubuf0))
    pl.when(i == n_prompt_tiles)(lambda: sample_step(ubuf1 if n_prompt_tiles % 2 == 0 else ubuf0))


def _slab_weights(conv_w, conv_b):
    cw = jnp.pad(conv_w, ((0, HALO - CONV_W), (0, 0)))
    cw = jnp.transpose(cw.reshape(HALO, N_SLAB, LANES), (1, 0, 2))
    return cw, conv_b.reshape(N_SLAB, 1, LANES)


def _proj_glu_conv(h, w2d, l, conv_w, conv_b, n_batch, seq, ms, tn=512):
    m, d = h.shape
    mp = n_batch * seq
    assert seq % TM == 0 and mp % ms == 0 and m == mp + ms
    npt = mp // TM
    assert npt % 2 == 0
    nsl = tn // LANES
    cw, cb = _slab_weights(conv_w, conv_b)
    return pl.pallas_call(
        functools.partial(_proj_glu_conv_kernel, tm=TM, rc=32, tiles_per_seq=seq // TM, n_prompt_tiles=npt,
                          row0=l * d, col_a=OFF_CA, col_b=OFF_CB),
        grid=(D_CONV // tn, npt + 1),
        in_specs=[
            pl.BlockSpec((TM, d), lambda j, i: (jnp.minimum(i, npt - 1), 0)),
            pl.BlockSpec((ms, d), lambda j, i: (mp // ms, 0), pipeline_mode=pl.Buffered(1)),
            pl.BlockSpec(memory_space=pl.ANY),
            pl.BlockSpec((nsl, HALO, LANES), lambda j, i: (j, 0, 0)),
            pl.BlockSpec((nsl, 1, LANES), lambda j, i: (j, 0, 0)),
        ],
        out_specs=[
            pl.BlockSpec((TM, tn), lambda j, i: (jnp.maximum(i - 1, 0), j)),
            pl.BlockSpec((nsl, ms, LANES), lambda j, i: (j, 0, 0)),
            pl.BlockSpec((None, nsl, HALO, LANES),
                         lambda j, i: (jnp.minimum(i // (seq // TM), n_batch - 1), j, 0, 0)),
        ],
        out_shape=[
            jax.ShapeDtypeStruct((mp, D_CONV), F32),
            jax.ShapeDtypeStruct((N_SLAB, ms, LANES), F32),
            jax.ShapeDtypeStruct((n_batch, N_SLAB, HALO, LANES), F32),
        ],
        scratch_shapes=[pltpu.VMEM((d, tn), F32), pltpu.VMEM((d, tn), F32),
                        pltpu.VMEM((d, tn), BF16), pltpu.VMEM((d, tn), BF16),
                        pltpu.VMEM((nsl, HALO + TM, LANES), F32), pltpu.VMEM((nsl, HALO + TM, LANES), F32),
                        pltpu.SemaphoreType.DMA, pltpu.SemaphoreType.DMA],
        compiler_params=_params(2, VMEM_PHYSICAL - 512 * 1024),
        name="proj_glu_conv",
    )(h, h, w2d, cw, cb)


def _prompt_attn_kernel(sinks_ref, q_ref, kp_ref, kc_ref, vp_ref, vc_ref, ga_ref, o_ref, bias, *, bq):
    qi = lax.broadcasted_iota(jnp.int32, (WINDOW, 2 * WINDOW), 0)
    kj = lax.broadcasted_iota(jnp.int32, (WINDOW, 2 * WINDOW), 1)

    @pl.when((pl.program_id(0) == 0) & (pl.program_id(1) == 0))
    def _():
        dist = WINDOW + qi - kj
        in_band = (dist >= 0) & (dist < WINDOW)
        distf = dist.astype(F32)
        for h in range(N_HEADS):
            bias[h] = jnp.where(in_band, (-SLOPES[h] * LOG2E) * distf, NEG_INF)

    has_prev = kj >= WINDOW * (pl.program_id(1) == 0).astype(jnp.int32)
    nsub = bq // WINDOW
    k_all = jnp.concatenate([kp_ref[...], kc_ref[...]], axis=0).astype(BF16)
    v_all = jnp.concatenate([vp_ref[...], vc_ref[...]], axis=0).astype(BF16)
    for s in range(nsub):
        rows = slice(s * WINDOW, (s + 1) * WINDOW)
        for kv in range(N_KV):
            cols = slice(kv * HEAD_DIM, (kv + 1) * HEAD_DIM)
            kb = k_all[s * WINDOW:(s + 2) * WINDOW, cols]
            vb = v_all[s * WINDOW:(s + 2) * WINDOW, cols]
            for g in range(GQA):
                h = kv * GQA + g
                hc = slice(h * HEAD_DIM, (h + 1) * HEAD_DIM)
                sc = lax.dot_general(q_ref[rows, hc], kb, (((1,), (1,)), ((), ())),
                                     preferred_element_type=F32)
                t = sc * (ATT_SCALE * LOG2E) + bias[h]
                if s == 0:
                    t = jnp.where(has_prev, t, NEG_INF)
                sink = sinks_ref[h] * LOG2E
                m = jnp.maximum(jnp.max(t, axis=-1, keepdims=True), sink)
                p = jnp.exp2(t - m)
                denom = jnp.sum(p, axis=-1, keepdims=True) + jnp.exp2(sink - m)
                o = jnp.dot(p.astype(BF16), vb, preferred_element_type=F32) / denom
                o_ref[rows, hc] = (o * ga_ref[rows, hc].astype(F32)).astype(o_ref.dtype)


def _sample_attn_kernel(slope_ref, sink_ref, qg_ref, kc_ref, vc_ref, new_ref,
                        o_ref, ko_ref, vo_ref, *, bn, s_len):
    q_ref, ga_ref = qg_ref.at[0], qg_ref.at[1]
    kn_ref, vn_ref = new_ref.at[0], new_ref.at[1]
    nk = WINDOW + SUBLANES
    r = GQA * s_len
    rows = bn * N_KV * r
    ri = lax.broadcasted_iota(jnp.int32, (rows, nk), 0)
    kj = lax.broadcasted_iota(jnp.int32, (rows, nk), 1)
    dist = lax.rem(ri, s_len) + WINDOW - kj
    valid = (dist >= 0) & (dist < WINDOW)
    shift = s_len * N_KV
    keep = WINDOW * N_KV - shift

    def head_rows(win_ref, new_ref, b, kv):
        win = win_ref.at[b][pl.ds(kv, WINDOW, stride=N_KV), :]
        new = new_ref.at[b][pl.ds(kv, SUBLANES, stride=N_KV), :]
        return jnp.concatenate([win, new], axis=0).astype(BF16)

    chains = [(b, kv) for b in range(bn) for kv in range(N_KV)]
    sc = jnp.concatenate(
        [lax.dot_general(q_ref[pl.ds(c * r, r), :], head_rows(kc_ref, kn_ref, b, kv),
                         (((1,), (1,)), ((), ())), preferred_element_type=F32)
         for c, (b, kv) in enumerate(chains)], axis=0)
    sc = sc * ATT_SCALE - slope_ref[...] * dist.astype(F32)
    sc = jnp.where(valid, sc, NEG_INF)
    sink = sink_ref[...]
    m = jnp.maximum(jnp.max(sc, axis=-1, keepdims=True), sink)
    p = jnp.exp(sc - m)
    denom = jnp.sum(p, axis=-1, keepdims=True) + jnp.exp(sink - m)
    p = p.astype(BF16)
    o = jnp.concatenate(
        [jnp.dot(p[c * r:(c + 1) * r], head_rows(vc_ref, vn_ref, b, kv), preferred_element_type=F32)
         for c, (b, kv) in enumerate(chains)], axis=0)
    o_ref[...] = (o / denom * ga_ref[...].astype(F32)).astype(o_ref.dtype)

    for b in range(bn):
        for win_ref, new_ref, out_ref in ((kc_ref, kn_ref, ko_ref), (vc_ref, vn_ref, vo_ref)):
            out_ref[b, pl.ds(0, keep), :] = win_ref[b, pl.ds(shift, keep), :]
            out_ref[b, pl.ds(keep, shift), :] = new_ref[b, pl.ds(0, shift), :]


def _to_head_rows(a, n, s_len):
    planes = a.shape[1] // D_ATT
    a = a.reshape(s_len, n, planes, N_KV, GQA, HEAD_DIM)
    return jnp.transpose(a, (2, 1, 3, 4, 0, 5)).reshape(planes, n * N_KV * GQA * s_len, HEAD_DIM)


def _from_head_rows(a, n, s_len):
    a = a.reshape(n, N_KV, GQA, s_len, HEAD_DIM)
    return jnp.transpose(a, (3, 0, 1, 2, 4)).reshape(s_len * n, D_ATT)


def _sample_dwconv_kernel(st_ref, u_ref, cw_ref, cb_ref, y_ref, so_ref, *, s_len, n):
    half = n // 2
    for s in range(u_ref.shape[0]):
        lanes = pl.ds(s * LANES, LANES)
        for r0 in range(0, n, half):
            rows = pl.ds(r0, half)
            ext = lambda j: st_ref[j, rows, lanes] if j < HIST else u_ref[s, pl.ds((j - HIST) * n + r0, half), :]
            acc = [jnp.broadcast_to(cb_ref[:, lanes], (half, LANES)) for _ in range(s_len)]
            for j in range(HIST + s_len):
                x = ext(j)
                for t in range(max(0, j - HIST), min(s_len - 1, j) + 1):
                    acc[t] = acc[t] + cw_ref[pl.ds(j - t, 1), lanes] * x
                if j >= s_len:
                    so_ref[j - s_len, rows, lanes] = x
            for t in range(s_len):
                y_ref[pl.ds(t * n + r0, half), lanes] = acc[t]


def _attention_kernel(sinks_ref, q_ref, kp_ref, kc_ref, vp_ref, vc_ref, ga_ref,
                      slope_ref, sink_ref, qg_ref, wk_ref, wv_ref, new_ref,
                      st_ref, u_ref, cw_ref, cb_ref,
                      o_ref, os_ref, ko_ref, vo_ref, ys_ref, so_ref, bias, *, bq, bn, s_len, n):
    _prompt_attn_kernel(sinks_ref, q_ref, kp_ref, kc_ref, vp_ref, vc_ref, ga_ref, o_ref, bias, bq=bq)
    _sample_attn_kernel(slope_ref, sink_ref, qg_ref, wk_ref, wv_ref, new_ref, os_ref, ko_ref, vo_ref,
                        bn=bn, s_len=s_len)
    _sample_dwconv_kernel(st_ref, u_ref, cw_ref, cb_ref, ys_ref, so_ref, s_len=s_len, n=n)


def _attention(z, k, v, sinks, cache_k, cache_v, state_tm, u_slab, conv_w, conv_b, l, n_batch, seq, n, s_len,
               bq=512):
    mp, ms = n_batch * seq, s_len * n
    nb = seq // bq
    steps = n_batch * nb
    depth, _, _, c = state_tm.shape
    assert n % steps == 0 and c % (steps * LANES) == 0 and s_len <= SUBLANES and n % SUBLANES == 0
    bn, tc = n // steps, c // steps
    ratio = bq // WINDOW
    step = lambda b, i: b * nb + i
    cur = lambda b, i, s: (step(b, i), 0)
    prev = lambda b, i, s: (jnp.maximum(step(b, i) * ratio - 1, 0), 0)

    r = GQA * s_len
    rows = bn * N_KV * r
    per_row = lambda a: jnp.broadcast_to(a.astype(F32).reshape(1, N_HEADS, 1, 1),
                                         (bn, N_HEADS, s_len, 1)).reshape(rows, 1)
    rows_new = SUBLANES * N_KV
    new = jnp.stack([k[mp:], v[mp:]]).reshape(2, s_len, n, N_KV, HEAD_DIM)
    new = jnp.transpose(new, (0, 2, 1, 3, 4)).reshape(2, n, s_len * N_KV, HEAD_DIM)
    new = jnp.pad(new, ((0, 0), (0, 0), (0, rows_new - s_len * N_KV), (0, 0)))
    win_rows = lambda a: a.reshape(depth * n, WINDOW * N_KV, HEAD_DIM)
    small = pl.BlockSpec((rows, 1), lambda b, i, s: (0, 0))
    heads = pl.BlockSpec((rows, HEAD_DIM), cur)
    win_in = pl.BlockSpec((bn, WINDOW * N_KV, HEAD_DIM), lambda b, i, s: (l * steps + step(b, i), 0, 0))
    win_out = pl.BlockSpec((bn, WINDOW * N_KV, HEAD_DIM), lambda b, i, s: (step(b, i), 0, 0))
    win_shape = jax.ShapeDtypeStruct((n, WINDOW * N_KV, HEAD_DIM), F32)
    col = lambda b, i, s: (0, step(b, i))

    att, att_s, k_win, v_win, y_s, new_state = pl.pallas_call(
        functools.partial(_attention_kernel, bq=bq, bn=bn, s_len=s_len, n=n),
        grid_spec=pltpu.PrefetchScalarGridSpec(
            num_scalar_prefetch=1,
            grid=(n_batch, nb),
            in_specs=[
                pl.BlockSpec((bq, D_ATT), cur),
                pl.BlockSpec((WINDOW, KV_W), prev),
                pl.BlockSpec((bq, KV_W), cur),
                pl.BlockSpec((WINDOW, KV_W), prev),
                pl.BlockSpec((bq, KV_W), cur),
                pl.BlockSpec((bq, D_ATT), lambda b, i, s: (step(b, i), 1)),
                small, small,
                pl.BlockSpec((2, rows, HEAD_DIM), lambda b, i, s: (0, step(b, i), 0)),
                win_in, win_in,
                pl.BlockSpec((2, bn, rows_new, HEAD_DIM), lambda b, i, s: (0, step(b, i), 0, 0)),
                pl.BlockSpec((None, HIST, n, tc), lambda b, i, s: (l, 0, 0, step(b, i))),
                pl.BlockSpec((tc // LANES, ms, LANES), lambda b, i, s: (step(b, i), 0, 0)),
                pl.BlockSpec((CONV_W, tc), col),
                pl.BlockSpec((1, tc), col),
            ],
            out_specs=[
                pl.BlockSpec((bq, D_ATT), cur),
                heads, win_out, win_out,
                pl.BlockSpec((ms, tc), col),
                pl.BlockSpec((HIST, n, tc), lambda b, i, s: (0, 0, step(b, i))),
            ],
            scratch_shapes=[pltpu.VMEM((N_HEADS, WINDOW, 2 * WINDOW), F32)],
        ),
        out_shape=[jax.ShapeDtypeStruct((mp, D_ATT), BF16),
                   jax.ShapeDtypeStruct((n * N_KV * r, HEAD_DIM), BF16), win_shape, win_shape,
                   jax.ShapeDtypeStruct((ms, c), F32), jax.ShapeDtypeStruct((HIST, n, c), F32)],
        compiler_params=_params(2),
        name="attention",
    )(sinks, z, k, k, v, v, z,
      per_row(jnp.asarray(SLOPES, F32)), per_row(sinks), _to_head_rows(z[mp:, :2 * D_ATT], n, s_len),
      win_rows(cache_k), win_rows(cache_v), new,
      state_tm, u_slab, conv_w, conv_b.reshape(1, c))
    win5 = lambda a: a.reshape(n, WINDOW, N_KV, HEAD_DIM)
    return att, _from_head_rows(att_s, n, s_len), win5(k_win), win5(v_win), y_s, new_state


def _ln_pw_kernel(yp_ref, ys_ref, lng_ref, lnb_ref, wpw_ref, cg_ref, o_ref, wpb, *, n_prompt_tiles):
    @pl.when(pl.program_id(0) == 0)
    def _():
        wpb[...] = wpw_ref[...].astype(BF16)

    y = jnp.where(pl.program_id(0) < n_prompt_tiles, yp_ref[...], ys_ref[...])
    mu = jnp.mean(y, axis=-1, keepdims=True)
    yc = y - mu
    var = jnp.mean(yc * yc, axis=-1, keepdims=True)
    yn = yc * lax.rsqrt(var + LN_EPS) * lng_ref[...] + lnb_ref[...]
    a = _silu(yn).astype(BF16)
    o = jnp.dot(a, wpb[...], preferred_element_type=F32)
    o_ref[...] = (o * cg_ref[...].astype(F32)).astype(o_ref.dtype)


def _ln_pw(y_p, y_s, z, cg_block, ln_g, ln_b, w_pw2, l):
    mp, c = y_p.shape
    tm = y_s.shape[0]
    m = mp + tm
    depth = w_pw2.shape[0]
    assert mp % tm == 0 and z.shape[0] == m
    npt = mp // tm
    const = lambda i: (0, 0)
    return pl.pallas_call(
        functools.partial(_ln_pw_kernel, n_prompt_tiles=npt),
        grid=(npt + 1,),
        in_specs=[
            pl.BlockSpec((tm, c), lambda i: (jnp.minimum(i, npt - 1), 0)),
            pl.BlockSpec((tm, c), const),
            pl.BlockSpec((1, c), const),
            pl.BlockSpec((1, c), const),
            pl.BlockSpec((c, c), lambda i: (l, 0), pipeline_mode=pl.Buffered(1)),
            pl.BlockSpec((tm, c), lambda i: (i, cg_block)),
        ],
        out_specs=pl.BlockSpec((tm, c), lambda i: (i, 0)),
        out_shape=jax.ShapeDtypeStruct((m, c), BF16),
        scratch_shapes=[pltpu.VMEM((c, c), BF16)],
        compiler_params=_params(1),
        name="ln_pw",
    )(y_p, y_s, ln_g.reshape(1, -1), ln_b.reshape(1, -1), w_pw2.reshape(depth * c, c), z)


def _outproj_kernel(a_ref, c_ref, as_ref, cs_ref, w_ref, xp_ref, gp_ref, xs_ref, gs_ref, yp_ref, ys_ref, wb,
                    *, n_prompt_tiles, tiles_per_seq, n_batch, s_len):
    i = pl.program_id(1)

    @pl.when(i == 0)
    def _():
        wb[...] = w_ref[...].astype(BF16)

    def project(att_ref, conv_ref):
        o = jnp.dot(att_ref[...], wb[pl.ds(0, D_ATT), :], preferred_element_type=F32)
        return o + jnp.dot(conv_ref[...], wb[pl.ds(D_ATT, D_CONV), :], preferred_element_type=F32)

    @pl.when(i < n_prompt_tiles)
    def _():
        gate = _select_row(gp_ref, i // tiles_per_seq, n_batch)
        yp_ref[...] = xp_ref[...] + gate * project(a_ref, c_ref)

    @pl.when(i == n_prompt_tiles)
    def _():
        gate = jnp.concatenate([gs_ref[...]] * s_len, axis=0)
        ys = xs_ref[...] + gate * project(as_ref, cs_ref)
        if len(ys_ref.shape) == 2:
            ys_ref[...] = ys
        else:
            ns = ys_ref.shape[0]
            for t in range(s_len):
                ys_ref[:, t, :] = ys[t * ns:(t + 1) * ns]


def _outproj(att_p, att_s, conv, w_out, l, xp, xs, mod, n_batch, seq, s_len, batch_major_out, tn=512):
    mp, d = xp.shape
    ms = xs.shape[0]
    ns = ms // s_len
    depth = w_out.shape[0]
    assert mp % TM == 0 and seq % TM == 0 and mp % ms == 0 and ns % MOD_ROWS == 0
    npt = mp // TM
    gate_col0 = 2 * d // tn
    ptile = lambda j, i: (jnp.minimum(i, npt - 1), j)
    ptile_rows = lambda j, i: (jnp.minimum(i, npt - 1), 0)
    stile = lambda j, i: (mp // ms, 0)
    if batch_major_out:
        ys_spec = pl.BlockSpec((ns, s_len, tn), lambda j, i: (0, 0, j))
        ys_shape = jax.ShapeDtypeStruct((ns, s_len, d), F32)
    else:
        ys_spec = pl.BlockSpec((ms, tn), lambda j, i: (0, j))
        ys_shape = jax.ShapeDtypeStruct((ms, d), F32)
    return pl.pallas_call(
        functools.partial(_outproj_kernel, n_prompt_tiles=npt, tiles_per_seq=seq // TM, n_batch=n_batch,
                          s_len=s_len),
        grid=(d // tn, npt + 1),
        in_specs=[
            pl.BlockSpec((TM, D_ATT), ptile_rows),
            pl.BlockSpec((TM, D_CONV), ptile_rows),
            pl.BlockSpec((ms, D_ATT), lambda j, i: (0, 0)),
            pl.BlockSpec((ms, D_CONV), stile),
            pl.BlockSpec((D_ATT + D_CONV, tn), lambda j, i: (l, j)),
            pl.BlockSpec((TM, tn), ptile),
            pl.BlockSpec((MOD_ROWS, tn), lambda j, i: (ns // MOD_ROWS, gate_col0 + j)),
            pl.BlockSpec((ms, tn), lambda j, i: (0, j)),
            pl.BlockSpec((ns, tn), lambda j, i: (0, gate_col0 + j)),
        ],
        out_specs=[pl.BlockSpec((TM, tn), ptile), ys_spec],
        out_shape=[jax.ShapeDtypeStruct((mp, d), F32), ys_shape],
        scratch_shapes=[pltpu.VMEM((D_ATT + D_CONV, tn), BF16)],
        compiler_params=_params(2),
        name="outproj",
    )(att_p, conv, att_s, conv, w_out.reshape(depth * (D_ATT + D_CONV), d), xp, mod, xs, mod)


def kernel(x_prompt, x_sample, c_prompt, c_sample, cache_k_win, cache_v_win, state_conv, w_ada, b_ada,
           norm_g, w_in, q_norm_g, k_norm_g, sinks, conv_w, conv_b, ln_g, ln_b, w_pw2, w_out):
    depth, d, n_in = w_in.shape
    nb, seq, _ = x_prompt.shape
    ns, s_len, _ = x_sample.shape
    mp, ms = nb * seq, ns * s_len
    m_all = mp + ms
    tm_in = m_all // 8
    assert tm_in * 8 == m_all and tm_in % 16 == 0

    xp = x_prompt.reshape(mp, d)
    xs = jnp.transpose(x_sample, (1, 0, 2)).reshape(ms, d)
    pad = (-(ns + nb)) % 16
    c_all = jnp.concatenate([c_sample, c_prompt, jnp.zeros((pad, d), F32)], axis=0)
    w_in2d = w_in.reshape(depth * d, n_in)
    state_tm = jnp.transpose(state_conv, (0, 2, 1, 3))

    kp_l, vp_l, cp_l, ks_l, vs_l, cs_l = [], [], [], [], [], []
    for l in range(depth):
        mod = _modulation(c_all, w_ada, b_ada, l)

        qg = q_norm_g[l].reshape(1, HEAD_DIM)
        kg = k_norm_g[l].reshape(1, HEAD_DIM)
        hkv = _prenorm_kv(xp, norm_g[l], mod, ns, nb, None, w_in2d, l, kg, ms, m_all, 0, zero_tiles=1)
        h, k, v = _prenorm_kv(xs, norm_g[l], mod, 0, None, ns, w_in2d, l, kg, ms, m_all, mp, prev=hkv)
        z = _proj_qgc(h, w_in2d, l, qg, tm_in)
        y, u_s, u_tail = _proj_glu_conv(h, w_in2d, l, conv_w[l], conv_b[l], nb, seq, ms)

        att, att_s, k_win, v_win, y_s, conv_state = _attention(
            z, k, v, sinks[l], cache_k_win, cache_v_win, state_tm, u_s, conv_w[l], conv_b[l], l, nb, seq, ns, s_len)
        conv = _ln_pw(y, y_s, z, 2 * D_ATT // D_CONV, ln_g[l], ln_b[l], w_pw2, l)

        tail = lambda a, rows: jnp.stack(
            [lax.slice_in_dim(a, (b + 1) * seq - rows, (b + 1) * seq, axis=0) for b in range(nb)])
        kp_l.append(tail(k, WINDOW).reshape(nb, WINDOW, N_KV, HEAD_DIM))
        vp_l.append(tail(v, WINDOW).reshape(nb, WINDOW, N_KV, HEAD_DIM))
        cp_l.append(jnp.transpose(u_tail[:, :, HALO - HIST:, :], (0, 2, 1, 3)).reshape(nb, HIST, D_CONV))
        ks_l.append(k_win)
        vs_l.append(v_win)
        cs_l.append(jnp.transpose(conv_state, (1, 0, 2)))

        xp, xs = _outproj(att, att_s, conv, w_out, l, xp, xs, mod, nb, seq, s_len, l == depth - 1)

    return (xp.reshape(nb, seq, d), xs,
            jnp.stack(kp_l), jnp.stack(vp_l), jnp.stack(cp_l),
            jnp.stack(ks_l), jnp.stack(vs_l), jnp.stack(cs_l))
```

```python
import functools

import numpy as np
import jax
import jax.numpy as jnp
from jax import lax
from jax.experimental import pallas as pl
from jax.experimental.pallas import tpu as pltpu

F32 = jnp.float32
BF16 = jnp.bfloat16

D_MODEL = 4096
D_ATT = 2048
D_CONV = 2048
HEAD_DIM = 128
N_HEADS = 16
N_KV = 4
GQA = 4
KV_W = N_KV * HEAD_DIM
WINDOW = 128
CONV_W = 31
HIST = CONV_W - 1
LANES = 128
SUBLANES = 8
HALO = 32
N_SLAB = D_CONV // LANES
RMS_EPS = 1e-6
LN_EPS = 1e-5
NEG_INF = -1e30
ATT_SCALE = HEAD_DIM ** -0.5
LOG2E = float(np.log2(np.e))
OFF_Q, OFF_K, OFF_V, OFF_GA, OFF_CA, OFF_CB, OFF_CG = 0, 2048, 2560, 3072, 5120, 7168, 9216
SLOPES = [float(np.float32(2.0 ** (-8.0 * (h + 1) / N_HEADS))) for h in range(N_HEADS)]

VMEM_PHYSICAL = 64 * 1024 * 1024
VMEM_LIMIT = 58 * 1024 * 1024
TM = 1024
MOD_ROWS = 8


def _params(n_axes, vmem_limit=VMEM_LIMIT):
    return pltpu.CompilerParams(dimension_semantics=("arbitrary",) * n_axes,
                                vmem_limit_bytes=vmem_limit)


def _silu(x):
    return x * jax.nn.sigmoid(x)


def _select_row(ref, idx, count):
    row = ref[0:1, :]
    for n in range(1, count):
        row = jnp.where(idx == n, ref[n:n + 1, :], row)
    return row


def _mod_kernel(c_ref, w_ref, b_ref, o_ref, act, *, tk):
    k = pl.program_id(0)

    @pl.when(k == 0)
    def _():
        act[...] = _silu(c_ref[...]).astype(BF16)
        o_ref[...] = jnp.broadcast_to(b_ref[...], o_ref.shape)

    a = act[:, pl.ds(pl.multiple_of(k * tk, tk), tk)]
    o_ref[...] += jnp.dot(a, w_ref[...].astype(BF16), preferred_element_type=F32)


def _modulation(c_all, w_ada, b_ada, l, tk=256):
    r, d = c_all.shape
    depth, _, n = w_ada.shape
    return pl.pallas_call(
        functools.partial(_mod_kernel, tk=tk),
        grid=(d // tk,),
        in_specs=[
            pl.BlockSpec((r, d), lambda k: (0, 0)),
            pl.BlockSpec((tk, n), lambda k: (l * (d // tk) + k, 0)),
            pl.BlockSpec((None, 1, n), lambda k: (l, 0, 0)),
        ],
        out_specs=pl.BlockSpec((r, n), lambda k: (0, 0)),
        out_shape=jax.ShapeDtypeStruct((r, n), F32),
        scratch_shapes=[pltpu.VMEM((r, d), BF16)],
        compiler_params=_params(1),
        name="modulation",
    )(c_all, w_ada.reshape(depth * d, n), b_ada.reshape(depth, 1, n))


def _headnorm(a, g):
    return a * lax.rsqrt(jnp.mean(a * a, axis=-1, keepdims=True) + RMS_EPS) * g


def _prenorm_kv_kernel(x_ref, g_ref, scale_ref, shift_ref, wk_ref, wv_ref, kg_ref, *rest,
                       tiles_per_batch, n_batch, rows_per_mod, n_tiles):
    h_ref, k_ref, v_ref, wkb, wvb = rest[-5:]

    @pl.when(pl.program_id(0) == 0)
    def _():
        wkb[...] = wk_ref[...].astype(BF16)
        wvb[...] = wv_ref[...].astype(BF16)

    @pl.when(pl.program_id(0) >= n_tiles)
    def _():
        for ref in (h_ref, k_ref, v_ref):
            ref[...] = jnp.zeros(ref.shape, ref.dtype)

    pl.when(pl.program_id(0) < n_tiles)(functools.partial(
        _prenorm_kv_tile, x_ref, g_ref, scale_ref, shift_ref, kg_ref, h_ref, k_ref, v_ref, wkb, wvb,
        tiles_per_batch, n_batch, rows_per_mod))


def _prenorm_kv_tile(x_ref, g_ref, scale_ref, shift_ref, kg_ref, h_ref, k_ref, v_ref, wkb, wvb,
                     tiles_per_batch, n_batch, rows_per_mod):
    x = x_ref[...]
    y = x * lax.rsqrt(jnp.mean(x * x, axis=-1, keepdims=True) + RMS_EPS)
    y = y * g_ref[...]
    if tiles_per_batch is None:
        reps = x.shape[0] // rows_per_mod
        scale = jnp.concatenate([scale_ref[...]] * reps, axis=0)
        shift = jnp.concatenate([shift_ref[...]] * reps, axis=0)
    else:
        batch = pl.program_id(0) // tiles_per_batch
        scale = _select_row(scale_ref, batch, n_batch)
        shift = _select_row(shift_ref, batch, n_batch)
    h = (y * (1.0 + scale) + shift).astype(BF16)
    h_ref[...] = h
    acc = jnp.dot(h, wkb[...], preferred_element_type=F32)
    kg = kg_ref[...]
    for c in range(N_KV):
        sl = slice(c * HEAD_DIM, (c + 1) * HEAD_DIM)
        k_ref[:, sl] = _headnorm(acc[:, sl], kg)
    v_ref[...] = jnp.dot(h, wvb[...], preferred_element_type=F32)


def _prenorm_kv(x2d, g, mod, mod_row0, n_batch, rows_per_mod, w2d, l, kg, tm, m_total, row0, prev=None,
                zero_tiles=0):
    m, d = x2d.shape
    if n_batch is None:
        assert mod_row0 % rows_per_mod == 0 and tm % rows_per_mod == 0
        mk = lambda c: pl.BlockSpec((rows_per_mod, d), lambda i: (mod_row0 // rows_per_mod, c))
        tiles_per_batch = None
    else:
        assert mod_row0 % MOD_ROWS == 0 and n_batch <= MOD_ROWS and (m // n_batch) % tm == 0
        mk = lambda c: pl.BlockSpec((MOD_ROWS, d), lambda i: (mod_row0 // MOD_ROWS, c))
        tiles_per_batch = m // n_batch // tm
    wspec = lambda off: pl.BlockSpec((d, KV_W), lambda i: (l, off // KV_W), pipeline_mode=pl.Buffered(1))
    n_tiles = m // tm
    in_specs = [pl.BlockSpec((tm, d), lambda i: (jnp.minimum(i, n_tiles - 1), 0)),
                pl.BlockSpec((1, d), lambda i: (0, 0)), mk(1), mk(0),
                wspec(OFF_K), wspec(OFF_V), pl.BlockSpec((1, HEAD_DIM), lambda i: (0, 0))]
    args = [x2d, g.reshape(1, d), mod, mod, w2d, w2d, kg]
    aliases = {}
    if prev is not None:
        aliases = {len(args) + n: n for n in range(3)}
        in_specs += [pl.BlockSpec(memory_space=pl.ANY)] * 3
        args += list(prev)
    assert row0 % tm == 0 and m % tm == 0
    rows = lambda i: (row0 // tm + i, 0)
    return pl.pallas_call(
        functools.partial(_prenorm_kv_kernel, tiles_per_batch=tiles_per_batch, n_batch=n_batch,
                          rows_per_mod=rows_per_mod, n_tiles=n_tiles),
        grid=(n_tiles + zero_tiles,),
        in_specs=in_specs,
        out_specs=[pl.BlockSpec((tm, d), rows), pl.BlockSpec((tm, KV_W), rows), pl.BlockSpec((tm, KV_W), rows)],
        out_shape=[jax.ShapeDtypeStruct((m_total, d), BF16), jax.ShapeDtypeStruct((m_total, KV_W), F32),
                   jax.ShapeDtypeStruct((m_total, KV_W), F32)],
        scratch_shapes=[pltpu.VMEM((d, KV_W), BF16), pltpu.VMEM((d, KV_W), BF16)],
        input_output_aliases=aliases,
        compiler_params=_params(1),
        name="prenorm_kv",
    )(*args)


def _proj_qgc_kernel(h_ref, w_hbm, qg_ref, z_ref, stage, wb, sem, *, row0, col_starts, q_tiles):
    j, i = pl.program_id(0), pl.program_id(1)
    d, tn = stage.shape
    n_tiles = len(col_starts)

    def weight_copy(jj):
        col = col_starts[0]
        for n in range(1, n_tiles):
            col = jnp.where(jj == n, col_starts[n], col)
        src = w_hbm.at[pl.ds(row0, d), pl.ds(pl.multiple_of(col, tn), tn)]
        return pltpu.make_async_copy(src, stage, sem)

    @pl.when(i == 0)
    def _():
        @pl.when(j == 0)
        def _():
            weight_copy(j).start()

        weight_copy(j).wait()
        wb[...] = stage[...].astype(BF16)

        @pl.when(j + 1 < n_tiles)
        def _():
            weight_copy(j + 1).start()

    def project():
        return jnp.dot(h_ref[...], wb[...], preferred_element_type=F32)

    @pl.when(j < q_tiles)
    def _():
        acc = project()
        g = qg_ref[...]
        for c in range(tn // HEAD_DIM):
            sl = slice(c * HEAD_DIM, (c + 1) * HEAD_DIM)
            z_ref[:, sl] = _headnorm(acc[:, sl], g).astype(z_ref.dtype)

    @pl.when(j >= q_tiles)
    def _():
        z_ref[...] = _silu(project()).astype(z_ref.dtype)


def _proj_qgc(h, w2d, l, qg, tm, tn=1024):
    m, d = h.shape
    assert m % tm == 0 and D_ATT % tn == 0 and D_CONV % tn == 0
    col_starts = [off + c for off, width in ((OFF_Q, D_ATT), (OFF_GA, D_ATT), (OFF_CG, D_CONV))
                  for c in range(0, width, tn)]
    return pl.pallas_call(
        functools.partial(_proj_qgc_kernel, row0=l * d, col_starts=col_starts, q_tiles=D_ATT // tn),
        grid=(len(col_starts), m // tm),
        in_specs=[pl.BlockSpec((tm, d), lambda j, i: (i, 0)), pl.BlockSpec(memory_space=pl.ANY),
                  pl.BlockSpec((1, HEAD_DIM), lambda j, i: (0, 0))],
        out_specs=pl.BlockSpec((tm, tn), lambda j, i: (i, j)),
        out_shape=jax.ShapeDtypeStruct((m, len(col_starts) * tn), BF16),
        scratch_shapes=[pltpu.VMEM((d, tn), F32), pltpu.VMEM((d, tn), BF16), pltpu.SemaphoreType.DMA],
        compiler_params=_params(2),
        name="proj_qgc",
    )(h, w2d, qg)


def _proj_glu_conv_kernel(h_ref, hs_ref, w_hbm, cw_ref, cb_ref, y_ref, us_ref, ut_ref,
                          stage_a, stage_b, wab, wbb, ubuf0, ubuf1, sem_a, sem_b,
                          *, tm, rc, tiles_per_seq, n_prompt_tiles, row0, col_a, col_b):
    j, i = pl.program_id(0), pl.program_id(1)
    d, tn = stage_a.shape
    nsl = tn // LANES
    base = HALO - HIST

    def weight_copies(jj):
        rows = pl.ds(row0, d)
        return (pltpu.make_async_copy(w_hbm.at[rows, pl.ds(pl.multiple_of(col_a + jj * tn, tn), tn)],
                                      stage_a, sem_a),
                pltpu.make_async_copy(w_hbm.at[rows, pl.ds(pl.multiple_of(col_b + jj * tn, tn), tn)],
                                      stage_b, sem_b))

    @pl.when(i == 0)
    def _():
        @pl.when(j == 0)
        def _():
            for copy in weight_copies(j):
                copy.start()

        for copy in weight_copies(j):
            copy.wait()
        wab[...] = stage_a[...].astype(BF16)
        wbb[...] = stage_b[...].astype(BF16)

        @pl.when(j + 1 < pl.num_programs(0))
        def _():
            for copy in weight_copies(j + 1):
                copy.start()

    @pl.when((i == 0) & (j == 0))
    def _():
        ubuf1[...] = jnp.zeros(ubuf1.shape, F32)

    def conv(prv):
        for c in range(nsl):
            bias = jnp.broadcast_to(cb_ref[c], (rc, LANES))
            for r0 in range(0, tm, rc):
                acc = bias
                for k in range(CONV_W):
                    acc = acc + cw_ref[c, pl.ds(k, 1), :] * prv[c, pl.ds(r0 + base + k, rc), :]
                y_ref[pl.ds(r0, rc), pl.ds(c * LANES, LANES)] = acc

    def glu(h, c0, width):
        cols = pl.ds(c0, width)
        a = jnp.dot(h, wab[:, cols], preferred_element_type=F32)
        b = jnp.dot(h, wbb[:, cols], preferred_element_type=F32)
        return a * jax.nn.sigmoid(b)

    mxu_cols = 2 * LANES

    def prompt_step(cur, prv):
        conv(prv)
        h = h_ref[...]
        seq_start = lax.rem(i, tiles_per_seq) == 0
        for c0 in range(0, tn, mxu_cols):
            u = glu(h, c0, mxu_cols)
            for c in range(c0 // LANES, (c0 + mxu_cols) // LANES):
                cur[c, pl.ds(0, HALO), :] = jnp.where(seq_start, 0.0, prv[c, pl.ds(tm, HALO), :])
                cur[c, pl.ds(HALO, tm), :] = u[:, c * LANES - c0:(c + 1) * LANES - c0]

        @pl.when(lax.rem(i, tiles_per_seq) == tiles_per_seq - 1)
        def _():
            for c in range(nsl):
                ut_ref[c] = cur[c, pl.ds(tm, HALO), :]

    def sample_step(prv):
        conv(prv)
        h = hs_ref[...]
        for c0 in range(0, tn, mxu_cols):
            u = glu(h, c0, mxu_cols)
            for c in range(c0 // LANES, (c0 + mxu_cols) // LANES):
                us_ref[c] = u[:, c * LANES - c0:(c + 1) * LANES - c0]

    is_prompt = i < n_prompt_tiles
    pl.when(is_prompt & (lax.rem(i, 2) == 0))(lambda: prompt_step(ubuf0, ubuf1))
    pl.when(is_prompt & (lax.rem(i, 2) == 1))(lambda: prompt_step(ubuf1, ubuf0))
    pl.when(i == n_prompt_tiles)(lambda: sample_step(ubuf1 if n_prompt_tiles % 2 == 0 else ubuf0))


def _slab_weights(conv_w, conv_b):
    cw = jnp.pad(conv_w, ((0, HALO - CONV_W), (0, 0)))
    cw = jnp.transpose(cw.reshape(HALO, N_SLAB, LANES), (1, 0, 2))
    return cw, conv_b.reshape(N_SLAB, 1, LANES)


def _proj_glu_conv(h, w2d, l, conv_w, conv_b, n_batch, seq, ms, tn=512):
    m, d = h.shape
    mp = n_batch * seq
    assert seq % TM == 0 and mp % ms == 0 and m == mp + ms
    npt = mp // TM
    assert npt % 2 == 0
    nsl = tn // LANES
    cw, cb = _slab_weights(conv_w, conv_b)
    return pl.pallas_call(
        functools.partial(_proj_glu_conv_kernel, tm=TM, rc=32, tiles_per_seq=seq // TM, n_prompt_tiles=npt,
                          row0=l * d, col_a=OFF_CA, col_b=OFF_CB),
        grid=(D_CONV // tn, npt + 1),
        in_specs=[
            pl.BlockSpec((TM, d), lambda j, i: (jnp.minimum(i, npt - 1), 0)),
            pl.BlockSpec((ms, d), lambda j, i: (mp // ms, 0), pipeline_mode=pl.Buffered(1)),
            pl.BlockSpec(memory_space=pl.ANY),
            pl.BlockSpec((nsl, HALO, LANES), lambda j, i: (j, 0, 0)),
            pl.BlockSpec((nsl, 1, LANES), lambda j, i: (j, 0, 0)),
        ],
        out_specs=[
            pl.BlockSpec((TM, tn), lambda j, i: (jnp.maximum(i - 1, 0), j)),
            pl.BlockSpec((nsl, ms, LANES), lambda j, i: (j, 0, 0)),
            pl.BlockSpec((None, nsl, HALO, LANES),
                         lambda j, i: (jnp.minimum(i // (seq // TM), n_batch - 1), j, 0, 0)),
        ],
        out_shape=[
            jax.ShapeDtypeStruct((mp, D_CONV), F32),
            jax.ShapeDtypeStruct((N_SLAB, ms, LANES), F32),
            jax.ShapeDtypeStruct((n_batch, N_SLAB, HALO, LANES), F32),
        ],
        scratch_shapes=[pltpu.VMEM((d, tn), F32), pltpu.VMEM((d, tn), F32),
                        pltpu.VMEM((d, tn), BF16), pltpu.VMEM((d, tn), BF16),
                        pltpu.VMEM((nsl, HALO + TM, LANES), F32), pltpu.VMEM((nsl, HALO + TM, LANES), F32),
                        pltpu.SemaphoreType.DMA, pltpu.SemaphoreType.DMA],
        compiler_params=_params(2, VMEM_PHYSICAL - 512 * 1024),
        name="proj_glu_conv",
    )(h, h, w2d, cw, cb)


def _prompt_attn_kernel(sinks_ref, q_ref, kp_ref, kc_ref, vp_ref, vc_ref, ga_ref, o_ref, bias, *, bq):
    qi = lax.broadcasted_iota(jnp.int32, (WINDOW, 2 * WINDOW), 0)
    kj = lax.broadcasted_iota(jnp.int32, (WINDOW, 2 * WINDOW), 1)

    @pl.when((pl.program_id(0) == 0) & (pl.program_id(1) == 0))
    def _():
        dist = WINDOW + qi - kj
        in_band = (dist >= 0) & (dist < WINDOW)
        distf = dist.astype(F32)
        for h in range(N_HEADS):
            bias[h] = jnp.where(in_band, (-SLOPES[h] * LOG2E) * distf, NEG_INF)

    has_prev = kj >= WINDOW * (pl.program_id(1) == 0).astype(jnp.int32)
    nsub = bq // WINDOW
    k_all = jnp.concatenate([kp_ref[...], kc_ref[...]], axis=0).astype(BF16)
    v_all = jnp.concatenate([vp_ref[...], vc_ref[...]], axis=0).astype(BF16)
    for s in range(nsub):
        rows = slice(s * WINDOW, (s + 1) * WINDOW)
        for kv in range(N_KV):
            cols = slice(kv * HEAD_DIM, (kv + 1) * HEAD_DIM)
            kb = k_all[s * WINDOW:(s + 2) * WINDOW, cols]
            vb = v_all[s * WINDOW:(s + 2) * WINDOW, cols]
            for g in range(GQA):
                h = kv * GQA + g
                hc = slice(h * HEAD_DIM, (h + 1) * HEAD_DIM)
                sc = lax.dot_general(q_ref[rows, hc], kb, (((1,), (1,)), ((), ())),
                                     preferred_element_type=F32)
                t = sc * (ATT_SCALE * LOG2E) + bias[h]
                if s == 0:
                    t = jnp.where(has_prev, t, NEG_INF)
                sink = sinks_ref[h] * LOG2E
                m = jnp.maximum(jnp.max(t, axis=-1, keepdims=True), sink)
                p = jnp.exp2(t - m)
                denom = jnp.sum(p, axis=-1, keepdims=True) + jnp.exp2(sink - m)
                o = jnp.dot(p.astype(BF16), vb, preferred_element_type=F32) / denom
                o_ref[rows, hc] = (o * ga_ref[rows, hc].astype(F32)).astype(o_ref.dtype)


def _sample_attn_kernel(slope_ref, sink_ref, qg_ref, kc_ref, vc_ref, new_ref,
                        o_ref, ko_ref, vo_ref, *, bn, s_len):
    q_ref, ga_ref = qg_ref.at[0], qg_ref.at[1]
    kn_ref, vn_ref = new_ref.at[0], new_ref.at[1]
    nk = WINDOW + SUBLANES
    r = GQA * s_len
    rows = bn * N_KV * r
    ri = lax.broadcasted_iota(jnp.int32, (rows, nk), 0)
    kj = lax.broadcasted_iota(jnp.int32, (rows, nk), 1)
    dist = lax.div(lax.rem(ri, r), GQA) + WINDOW - kj
    valid = (dist >= 0) & (dist < WINDOW)
    shift = s_len * N_KV
    keep = WINDOW * N_KV - shift

    def head_rows(win_ref, new_ref, b, kv):
        win = win_ref.at[b][pl.ds(kv, WINDOW, stride=N_KV), :]
        new = new_ref.at[b][pl.ds(kv, SUBLANES, stride=N_KV), :]
        return jnp.concatenate([win, new], axis=0).astype(BF16)

    chains = [(b, kv) for b in range(bn) for kv in range(N_KV)]
    sc = jnp.concatenate(
        [lax.dot_general(q_ref[pl.ds(c * r, r), :], head_rows(kc_ref, kn_ref, b, kv),
                         (((1,), (1,)), ((), ())), preferred_element_type=F32)
         for c, (b, kv) in enumerate(chains)], axis=0)
    sc = sc * ATT_SCALE - slope_ref[...] * dist.astype(F32)
    sc = jnp.where(valid, sc, NEG_INF)
    sink = sink_ref[...]
    m = jnp.maximum(jnp.max(sc, axis=-1, keepdims=True), sink)
    p = jnp.exp(sc - m)
    denom = jnp.sum(p, axis=-1, keepdims=True) + jnp.exp(sink - m)
    p = p.astype(BF16)
    o = jnp.concatenate(
        [jnp.dot(p[c * r:(c + 1) * r], head_rows(vc_ref, vn_ref, b, kv), preferred_element_type=F32)
         for c, (b, kv) in enumerate(chains)], axis=0)
    o_ref[...] = (o / denom * ga_ref[...].astype(F32)).astype(o_ref.dtype)

    for b in range(bn):
        for win_ref, new_ref, out_ref in ((kc_ref, kn_ref, ko_ref), (vc_ref, vn_ref, vo_ref)):
            out_ref[b, pl.ds(0, keep), :] = win_ref[b, pl.ds(shift, keep), :]
            out_ref[b, pl.ds(keep, shift), :] = new_ref[b, pl.ds(0, shift), :]


def _to_head_rows(a, n, s_len):
    planes = a.shape[1] // D_ATT
    a = a.reshape(s_len, n, planes, N_KV, GQA, HEAD_DIM)
    return jnp.transpose(a, (2, 1, 3, 0, 4, 5)).reshape(planes, n * N_KV * s_len * GQA, HEAD_DIM)


def _from_head_rows(a, n, s_len):
    a = a.reshape(n, N_KV, s_len, GQA, HEAD_DIM)
    return jnp.transpose(a, (2, 0, 1, 3, 4)).reshape(s_len * n, D_ATT)


def _sample_dwconv_kernel(st_ref, u_ref, cw_ref, cb_ref, y_ref, so_ref, *, s_len, n):
    half = n // 2
    for s in range(u_ref.shape[0]):
        lanes = pl.ds(s * LANES, LANES)
        for r0 in range(0, n, half):
            rows = pl.ds(r0, half)
            ext = lambda j: st_ref[j, rows, lanes] if j < HIST else u_ref[s, pl.ds((j - HIST) * n + r0, half), :]
            acc = [jnp.broadcast_to(cb_ref[:, lanes], (half, LANES)) for _ in range(s_len)]
            for j in range(HIST + s_len):
                x = ext(j)
                for t in range(max(0, j - HIST), min(s_len - 1, j) + 1):
                    acc[t] = acc[t] + cw_ref[pl.ds(j - t, 1), lanes] * x
                if j >= s_len:
                    so_ref[j - s_len, rows, lanes] = x
            for t in range(s_len):
                y_ref[pl.ds(t * n + r0, half), lanes] = acc[t]


def _attention_kernel(sinks_ref, q_ref, kp_ref, kc_ref, vp_ref, vc_ref, ga_ref,
                      slope_ref, sink_ref, qg_ref, wk_ref, wv_ref, new_ref,
                      st_ref, u_ref, cw_ref, cb_ref,
                      o_ref, os_ref, ko_ref, vo_ref, ys_ref, so_ref, bias, *, bq, bn, s_len, n):
    _prompt_attn_kernel(sinks_ref, q_ref, kp_ref, kc_ref, vp_ref, vc_ref, ga_ref, o_ref, bias, bq=bq)
    _sample_attn_kernel(slope_ref, sink_ref, qg_ref, wk_ref, wv_ref, new_ref, os_ref, ko_ref, vo_ref,
                        bn=bn, s_len=s_len)
    _sample_dwconv_kernel(st_ref, u_ref, cw_ref, cb_ref, ys_ref, so_ref, s_len=s_len, n=n)


def _attention(z, k, v, sinks, cache_k, cache_v, state_tm, u_slab, conv_w, conv_b, l, n_batch, seq, n, s_len,
               bq=512):
    mp, ms = n_batch * seq, s_len * n
    nb = seq // bq
    steps = n_batch * nb
    depth, _, _, c = state_tm.shape
    assert n % steps == 0 and c % (steps * LANES) == 0 and s_len <= SUBLANES and n % SUBLANES == 0
    bn, tc = n // steps, c // steps
    ratio = bq // WINDOW
    step = lambda b, i: b * nb + i
    cur = lambda b, i, s: (step(b, i), 0)
    prev = lambda b, i, s: (jnp.maximum(step(b, i) * ratio - 1, 0), 0)

    r = GQA * s_len
    rows = bn * N_KV * r
    per_row = lambda a: jnp.broadcast_to(a.astype(F32).reshape(1, N_KV, 1, GQA),
                                         (bn, N_KV, s_len, GQA)).reshape(rows, 1)
    rows_new = SUBLANES * N_KV
    new = jnp.stack([k[mp:], v[mp:]]).reshape(2, s_len, n, N_KV, HEAD_DIM)
    new = jnp.transpose(new, (0, 2, 1, 3, 4)).reshape(2, n, s_len * N_KV, HEAD_DIM)
    new = jnp.pad(new, ((0, 0), (0, 0), (0, rows_new - s_len * N_KV), (0, 0)))
    win_rows = lambda a: a.reshape(depth * n, WINDOW * N_KV, HEAD_DIM)
    small = pl.BlockSpec((rows, 1), lambda b, i, s: (0, 0))
    heads = pl.BlockSpec((rows, HEAD_DIM), cur)
    win_in = pl.BlockSpec((bn, WINDOW * N_KV, HEAD_DIM), lambda b, i, s: (l * steps + step(b, i), 0, 0))
    win_out = pl.BlockSpec((bn, WINDOW * N_KV, HEAD_DIM), lambda b, i, s: (step(b, i), 0, 0))
    win_shape = jax.ShapeDtypeStruct((n, WINDOW * N_KV, HEAD_DIM), F32)
    col = lambda b, i, s: (0, step(b, i))

    att, att_s, k_win, v_win, y_s, new_state = pl.pallas_call(
        functools.partial(_attention_kernel, bq=bq, bn=bn, s_len=s_len, n=n),
        grid_spec=pltpu.PrefetchScalarGridSpec(
            num_scalar_prefetch=1,
            grid=(n_batch, nb),
            in_specs=[
                pl.BlockSpec((bq, D_ATT), cur),
                pl.BlockSpec((WINDOW, KV_W), prev),
                pl.BlockSpec((bq, KV_W), cur),
                pl.BlockSpec((WINDOW, KV_W), prev),
                pl.BlockSpec((bq, KV_W), cur),
                pl.BlockSpec((bq, D_ATT), lambda b, i, s: (step(b, i), 1)),
                small, small,
                pl.BlockSpec((2, rows, HEAD_DIM), lambda b, i, s: (0, step(b, i), 0)),
                win_in, win_in,
                pl.BlockSpec((2, bn, rows_new, HEAD_DIM), lambda b, i, s: (0, step(b, i), 0, 0)),
                pl.BlockSpec((None, HIST, n, tc), lambda b, i, s: (l, 0, 0, step(b, i))),
                pl.BlockSpec((tc // LANES, ms, LANES), lambda b, i, s: (step(b, i), 0, 0)),
                pl.BlockSpec((CONV_W, tc), col),
                pl.BlockSpec((1, tc), col),
            ],
            out_specs=[
                pl.BlockSpec((bq, D_ATT), cur),
                heads, win_out, win_out,
                pl.BlockSpec((ms, tc), col),
                pl.BlockSpec((HIST, n, tc), lambda b, i, s: (0, 0, step(b, i))),
            ],
            scratch_shapes=[pltpu.VMEM((N_HEADS, WINDOW, 2 * WINDOW), F32)],
        ),
        out_shape=[jax.ShapeDtypeStruct((mp, D_ATT), BF16),
                   jax.ShapeDtypeStruct((n * N_KV * r, HEAD_DIM), BF16), win_shape, win_shape,
                   jax.ShapeDtypeStruct((ms, c), F32), jax.ShapeDtypeStruct((HIST, n, c), F32)],
        compiler_params=_params(2),
        name="attention",
    )(sinks, z, k, k, v, v, z,
      per_row(jnp.asarray(SLOPES, F32)), per_row(sinks), _to_head_rows(z[mp:, :2 * D_ATT], n, s_len),
      win_rows(cache_k), win_rows(cache_v), new,
      state_tm, u_slab, conv_w, conv_b.reshape(1, c))
    win5 = lambda a: a.reshape(n, WINDOW, N_KV, HEAD_DIM)
    return att, _from_head_rows(att_s, n, s_len), win5(k_win), win5(v_win), y_s, new_state


def _ln_pw_kernel(yp_ref, ys_ref, lng_ref, lnb_ref, wpw_ref, cg_ref, o_ref, wpb, *, n_prompt_tiles):
    @pl.when(pl.program_id(0) == 0)
    def _():
        wpb[...] = wpw_ref[...].astype(BF16)

    y = jnp.where(pl.program_id(0) < n_prompt_tiles, yp_ref[...], ys_ref[...])
    mu = jnp.mean(y, axis=-1, keepdims=True)
    yc = y - mu
    var = jnp.mean(yc * yc, axis=-1, keepdims=True)
    yn = yc * lax.rsqrt(var + LN_EPS) * lng_ref[...] + lnb_ref[...]
    a = _silu(yn).astype(BF16)
    o = jnp.dot(a, wpb[...], preferred_element_type=F32)
    o_ref[...] = (o * cg_ref[...].astype(F32)).astype(o_ref.dtype)


def _ln_pw(y_p, y_s, z, cg_block, ln_g, ln_b, w_pw2, l):
    mp, c = y_p.shape
    tm = y_s.shape[0]
    m = mp + tm
    depth = w_pw2.shape[0]
    assert mp % tm == 0 and z.shape[0] == m
    npt = mp // tm
    const = lambda i: (0, 0)
    return pl.pallas_call(
        functools.partial(_ln_pw_kernel, n_prompt_tiles=npt),
        grid=(npt + 1,),
        in_specs=[
            pl.BlockSpec((tm, c), lambda i: (jnp.minimum(i, npt - 1), 0)),
            pl.BlockSpec((tm, c), const),
            pl.BlockSpec((1, c), const),
            pl.BlockSpec((1, c), const),
            pl.BlockSpec((c, c), lambda i: (l, 0), pipeline_mode=pl.Buffered(1)),
            pl.BlockSpec((tm, c), lambda i: (i, cg_block)),
        ],
        out_specs=pl.BlockSpec((tm, c), lambda i: (i, 0)),
        out_shape=jax.ShapeDtypeStruct((m, c), BF16),
        scratch_shapes=[pltpu.VMEM((c, c), BF16)],
        compiler_params=_params(1),
        name="ln_pw",
    )(y_p, y_s, ln_g.reshape(1, -1), ln_b.reshape(1, -1), w_pw2.reshape(depth * c, c), z)


def _outproj_kernel(a_ref, c_ref, as_ref, cs_ref, w_ref, xp_ref, gp_ref, xs_ref, gs_ref, yp_ref, ys_ref, wb,
                    *, n_prompt_tiles, tiles_per_seq, n_batch, s_len):
    i = pl.program_id(1)

    @pl.when(i == 0)
    def _():
        wb[...] = w_ref[...].astype(BF16)

    def project(att_ref, conv_ref):
        o = jnp.dot(att_ref[...], wb[pl.ds(0, D_ATT), :], preferred_element_type=F32)
        return o + jnp.dot(conv_ref[...], wb[pl.ds(D_ATT, D_CONV), :], preferred_element_type=F32)

    @pl.when(i < n_prompt_tiles)
    def _():
        gate = _select_row(gp_ref, i // tiles_per_seq, n_batch)
        yp_ref[...] = xp_ref[...] + gate * project(a_ref, c_ref)

    @pl.when(i == n_prompt_tiles)
    def _():
        gate = jnp.concatenate([gs_ref[...]] * s_len, axis=0)
        ys = xs_ref[...] + gate * project(as_ref, cs_ref)
        if len(ys_ref.shape) == 2:
            ys_ref[...] = ys
        else:
            ns = ys_ref.shape[0]
            for t in range(s_len):
                ys_ref[:, t, :] = ys[t * ns:(t + 1) * ns]


def _outproj(att_p, att_s, conv, w_out, l, xp, xs, mod, n_batch, seq, s_len, batch_major_out, tn=512):
    mp, d = xp.shape
    ms = xs.shape[0]
    ns = ms // s_len
    depth = w_out.shape[0]
    assert mp % TM == 0 and seq % TM == 0 and mp % ms == 0 and ns % MOD_ROWS == 0
    npt = mp // TM
    gate_col0 = 2 * d // tn
    ptile = lambda j, i: (jnp.minimum(i, npt - 1), j)
    ptile_rows = lambda j, i: (jnp.minimum(i, npt - 1), 0)
    stile = lambda j, i: (mp // ms, 0)
    if batch_major_out:
        ys_spec = pl.BlockSpec((ns, s_len, tn), lambda j, i: (0, 0, j))
        ys_shape = jax.ShapeDtypeStruct((ns, s_len, d), F32)
    else:
        ys_spec = pl.BlockSpec((ms, tn), lambda j, i: (0, j))
        ys_shape = jax.ShapeDtypeStruct((ms, d), F32)
    return pl.pallas_call(
        functools.partial(_outproj_kernel, n_prompt_tiles=npt, tiles_per_seq=seq // TM, n_batch=n_batch,
                          s_len=s_len),
        grid=(d // tn, npt + 1),
        in_specs=[
            pl.BlockSpec((TM, D_ATT), ptile_rows),
            pl.BlockSpec((TM, D_CONV), ptile_rows),
            pl.BlockSpec((ms, D_ATT), lambda j, i: (0, 0)),
            pl.BlockSpec((ms, D_CONV), stile),
            pl.BlockSpec((D_ATT + D_CONV, tn), lambda j, i: (l, j)),
            pl.BlockSpec((TM, tn), ptile),
            pl.BlockSpec((MOD_ROWS, tn), lambda j, i: (ns // MOD_ROWS, gate_col0 + j)),
            pl.BlockSpec((ms, tn), lambda j, i: (0, j)),
            pl.BlockSpec((ns, tn), lambda j, i: (0, gate_col0 + j)),
        ],
        out_specs=[pl.BlockSpec((TM, tn), ptile), ys_spec],
        out_shape=[jax.ShapeDtypeStruct((mp, d), F32), ys_shape],
        scratch_shapes=[pltpu.VMEM((D_ATT + D_CONV, tn), BF16)],
        compiler_params=_params(2),
        name="outproj",
    )(att_p, conv, att_s, conv, w_out.reshape(depth * (D_ATT + D_CONV), d), xp, mod, xs, mod)


def kernel(x_prompt, x_sample, c_prompt, c_sample, cache_k_win, cache_v_win, state_conv, w_ada, b_ada,
           norm_g, w_in, q_norm_g, k_norm_g, sinks, conv_w, conv_b, ln_g, ln_b, w_pw2, w_out):
    depth, d, n_in = w_in.shape
    nb, seq, _ = x_prompt.shape
    ns, s_len, _ = x_sample.shape
    mp, ms = nb * seq, ns * s_len
    m_all = mp + ms
    tm_in = m_all // 8
    assert tm_in * 8 == m_all and tm_in % 16 == 0

    xp = x_prompt.reshape(mp, d)
    xs = jnp.transpose(x_sample, (1, 0, 2)).reshape(ms, d)
    pad = (-(ns + nb)) % 16
    c_all = jnp.concatenate([c_sample, c_prompt, jnp.zeros((pad, d), F32)], axis=0)
    w_in2d = w_in.reshape(depth * d, n_in)
    state_tm = jnp.transpose(state_conv, (0, 2, 1, 3))

    kp_l, vp_l, cp_l, ks_l, vs_l, cs_l = [], [], [], [], [], []
    for l in range(depth):
        mod = _modulation(c_all, w_ada, b_ada, l)

        qg = q_norm_g[l].reshape(1, HEAD_DIM)
        kg = k_norm_g[l].reshape(1, HEAD_DIM)
        hkv = _prenorm_kv(xp, norm_g[l], mod, ns, nb, None, w_in2d, l, kg, ms, m_all, 0, zero_tiles=1)
        h, k, v = _prenorm_kv(xs, norm_g[l], mod, 0, None, ns, w_in2d, l, kg, ms, m_all, mp, prev=hkv)
        z = _proj_qgc(h, w_in2d, l, qg, tm_in)
        y, u_s, u_tail = _proj_glu_conv(h, w_in2d, l, conv_w[l], conv_b[l], nb, seq, ms)

        att, att_s, k_win, v_win, y_s, conv_state = _attention(
            z, k, v, sinks[l], cache_k_win, cache_v_win, state_tm, u_s, conv_w[l], conv_b[l], l, nb, seq, ns, s_len)
        conv = _ln_pw(y, y_s, z, 2 * D_ATT // D_CONV, ln_g[l], ln_b[l], w_pw2, l)

        tail = lambda a, rows: jnp.stack(
            [lax.slice_in_dim(a, (b + 1) * seq - rows, (b + 1) * seq, axis=0) for b in range(nb)])
        kp_l.append(tail(k, WINDOW).reshape(nb, WINDOW, N_KV, HEAD_DIM))
        vp_l.append(tail(v, WINDOW).reshape(nb, WINDOW, N_KV, HEAD_DIM))
        cp_l.append(jnp.transpose(u_tail[:, :, HALO - HIST:, :], (0, 2, 1, 3)).reshape(nb, HIST, D_CONV))
        ks_l.append(k_win)
        vs_l.append(v_win)
        cs_l.append(jnp.transpose(conv_state, (1, 0, 2)))

        xp, xs = _outproj(att, att_s, conv, w_out, l, xp, xs, mod, nb, seq, s_len, l == depth - 1)

    return (xp.reshape(nb, seq, d), xs,
            jnp.stack(kp_l), jnp.stack(vp_l), jnp.stack(cp_l),
            jnp.stack(ks_l), jnp.stack(vs_l), jnp.stack(cs_l))
```

```python
import functools

import numpy as np
import jax
import jax.numpy as jnp
from jax import lax
from jax.experimental import pallas as pl
from jax.experimental.pallas import tpu as pltpu

F32 = jnp.float32
BF16 = jnp.bfloat16

D_MODEL = 4096
D_ATT = 2048
D_CONV = 2048
HEAD_DIM = 128
N_HEADS = 16
N_KV = 4
GQA = 4
KV_W = N_KV * HEAD_DIM
WINDOW = 128
CONV_W = 31
HIST = CONV_W - 1
LANES = 128
SUBLANES = 8
HALO = 32
N_SLAB = D_CONV // LANES
RMS_EPS = 1e-6
LN_EPS = 1e-5
NEG_INF = -1e30
ATT_SCALE = HEAD_DIM ** -0.5
LOG2E = float(np.log2(np.e))
OFF_Q, OFF_K, OFF_V, OFF_GA, OFF_CA, OFF_CB, OFF_CG = 0, 2048, 2560, 3072, 5120, 7168, 9216
SLOPES = [float(np.float32(2.0 ** (-8.0 * (h + 1) / N_HEADS))) for h in range(N_HEADS)]

VMEM_PHYSICAL = 64 * 1024 * 1024
VMEM_LIMIT = 58 * 1024 * 1024
TM = 1024
MOD_ROWS = 8


def _params(n_axes, vmem_limit=VMEM_LIMIT):
    return pltpu.CompilerParams(dimension_semantics=("arbitrary",) * n_axes,
                                vmem_limit_bytes=vmem_limit)


def _silu(x):
    return x * jax.nn.sigmoid(x)


def _select_row(ref, idx, count):
    row = ref[0:1, :]
    for n in range(1, count):
        row = jnp.where(idx == n, ref[n:n + 1, :], row)
    return row


def _mod_kernel(c_ref, w_ref, b_ref, o_ref, act, *, tk):
    k = pl.program_id(0)

    @pl.when(k == 0)
    def _():
        act[...] = _silu(c_ref[...]).astype(BF16)
        o_ref[...] = jnp.broadcast_to(b_ref[...], o_ref.shape)

    a = act[:, pl.ds(pl.multiple_of(k * tk, tk), tk)]
    o_ref[...] += jnp.dot(a, w_ref[...].astype(BF16), preferred_element_type=F32)


def _modulation(c_all, w_ada, b_ada, l, tk=256):
    r, d = c_all.shape
    depth, _, n = w_ada.shape
    return pl.pallas_call(
        functools.partial(_mod_kernel, tk=tk),
        grid=(d // tk,),
        in_specs=[
            pl.BlockSpec((r, d), lambda k: (0, 0)),
            pl.BlockSpec((tk, n), lambda k: (l * (d // tk) + k, 0)),
            pl.BlockSpec((None, 1, n), lambda k: (l, 0, 0)),
        ],
        out_specs=pl.BlockSpec((r, n), lambda k: (0, 0)),
        out_shape=jax.ShapeDtypeStruct((r, n), F32),
        scratch_shapes=[pltpu.VMEM((r, d), BF16)],
        compiler_params=_params(1),
        name="modulation",
    )(c_all, w_ada.reshape(depth * d, n), b_ada.reshape(depth, 1, n))


def _headnorm(a, g):
    return a * lax.rsqrt(jnp.mean(a * a, axis=-1, keepdims=True) + RMS_EPS) * g


def _prenorm_kv_kernel(x_ref, g_ref, scale_ref, shift_ref, wk_ref, wv_ref, kg_ref, *rest,
                       tiles_per_batch, n_batch, rows_per_mod, n_tiles):
    h_ref, k_ref, v_ref, wkb, wvb = rest[-5:]

    @pl.when(pl.program_id(0) == 0)
    def _():
        wkb[...] = wk_ref[...].astype(BF16)
        wvb[...] = wv_ref[...].astype(BF16)

    @pl.when(pl.program_id(0) >= n_tiles)
    def _():
        for ref in (h_ref, k_ref, v_ref):
            ref[...] = jnp.zeros(ref.shape, ref.dtype)

    pl.when(pl.program_id(0) < n_tiles)(functools.partial(
        _prenorm_kv_tile, x_ref, g_ref, scale_ref, shift_ref, kg_ref, h_ref, k_ref, v_ref, wkb, wvb,
        tiles_per_batch, n_batch, rows_per_mod))


def _prenorm_kv_tile(x_ref, g_ref, scale_ref, shift_ref, kg_ref, h_ref, k_ref, v_ref, wkb, wvb,
                     tiles_per_batch, n_batch, rows_per_mod):
    x = x_ref[...]
    y = x * lax.rsqrt(jnp.mean(x * x, axis=-1, keepdims=True) + RMS_EPS)
    y = y * g_ref[...]
    if tiles_per_batch is None:
        reps = x.shape[0] // rows_per_mod
        scale = jnp.concatenate([scale_ref[...]] * reps, axis=0)
        shift = jnp.concatenate([shift_ref[...]] * reps, axis=0)
    else:
        batch = pl.program_id(0) // tiles_per_batch
        scale = _select_row(scale_ref, batch, n_batch)
        shift = _select_row(shift_ref, batch, n_batch)
    h = (y * (1.0 + scale) + shift).astype(BF16)
    h_ref[...] = h
    acc = jnp.dot(h, wkb[...], preferred_element_type=F32)
    kg = kg_ref[...]
    for c in range(N_KV):
        sl = slice(c * HEAD_DIM, (c + 1) * HEAD_DIM)
        k_ref[:, sl] = _headnorm(acc[:, sl], kg)
    v_ref[...] = jnp.dot(h, wvb[...], preferred_element_type=F32)


def _prenorm_kv(x2d, g, mod, mod_row0, n_batch, rows_per_mod, w2d, l, kg, tm, m_total, row0, prev=None,
                zero_tiles=0):
    m, d = x2d.shape
    if n_batch is None:
        assert mod_row0 % rows_per_mod == 0 and tm % rows_per_mod == 0
        mk = lambda c: pl.BlockSpec((rows_per_mod, d), lambda i: (mod_row0 // rows_per_mod, c))
        tiles_per_batch = None
    else:
        assert mod_row0 % MOD_ROWS == 0 and n_batch <= MOD_ROWS and (m // n_batch) % tm == 0
        mk = lambda c: pl.BlockSpec((MOD_ROWS, d), lambda i: (mod_row0 // MOD_ROWS, c))
        tiles_per_batch = m // n_batch // tm
    wspec = lambda off: pl.BlockSpec((d, KV_W), lambda i: (l, off // KV_W), pipeline_mode=pl.Buffered(1))
    n_tiles = m // tm
    in_specs = [pl.BlockSpec((tm, d), lambda i: (jnp.minimum(i, n_tiles - 1), 0)),
                pl.BlockSpec((1, d), lambda i: (0, 0)), mk(1), mk(0),
                wspec(OFF_K), wspec(OFF_V), pl.BlockSpec((1, HEAD_DIM), lambda i: (0, 0))]
    args = [x2d, g.reshape(1, d), mod, mod, w2d, w2d, kg]
    aliases = {}
    if prev is not None:
        aliases = {len(args) + n: n for n in range(3)}
        in_specs += [pl.BlockSpec(memory_space=pl.ANY)] * 3
        args += list(prev)
    assert row0 % tm == 0 and m % tm == 0
    rows = lambda i: (row0 // tm + i, 0)
    return pl.pallas_call(
        functools.partial(_prenorm_kv_kernel, tiles_per_batch=tiles_per_batch, n_batch=n_batch,
                          rows_per_mod=rows_per_mod, n_tiles=n_tiles),
        grid=(n_tiles + zero_tiles,),
        in_specs=in_specs,
        out_specs=[pl.BlockSpec((tm, d), rows), pl.BlockSpec((tm, KV_W), rows), pl.BlockSpec((tm, KV_W), rows)],
        out_shape=[jax.ShapeDtypeStruct((m_total, d), BF16), jax.ShapeDtypeStruct((m_total, KV_W), F32),
                   jax.ShapeDtypeStruct((m_total, KV_W), F32)],
        scratch_shapes=[pltpu.VMEM((d, KV_W), BF16), pltpu.VMEM((d, KV_W), BF16)],
        input_output_aliases=aliases,
        compiler_params=_params(1),
        name="prenorm_kv",
    )(*args)


def _proj_qgc_kernel(h_ref, w_hbm, qg_ref, z_ref, stage, wb, sem, *, row0, col_starts, q_tiles):
    j, i = pl.program_id(0), pl.program_id(1)
    d, tn = stage.shape
    n_tiles = len(col_starts)

    def weight_copy(jj):
        col = col_starts[0]
        for n in range(1, n_tiles):
            col = jnp.where(jj == n, col_starts[n], col)
        src = w_hbm.at[pl.ds(row0, d), pl.ds(pl.multiple_of(col, tn), tn)]
        return pltpu.make_async_copy(src, stage, sem)

    @pl.when(i == 0)
    def _():
        @pl.when(j == 0)
        def _():
            weight_copy(j).start()

        weight_copy(j).wait()
        wb[...] = stage[...].astype(BF16)

        @pl.when(j + 1 < n_tiles)
        def _():
            weight_copy(j + 1).start()

    def project():
        return jnp.dot(h_ref[...], wb[...], preferred_element_type=F32)

    @pl.when(j < q_tiles)
    def _():
        acc = project()
        g = qg_ref[...]
        for c in range(tn // HEAD_DIM):
            sl = slice(c * HEAD_DIM, (c + 1) * HEAD_DIM)
            z_ref[:, sl] = _headnorm(acc[:, sl], g).astype(z_ref.dtype)

    @pl.when(j >= q_tiles)
    def _():
        z_ref[...] = _silu(project()).astype(z_ref.dtype)


def _proj_qgc(h, w2d, l, qg, tm, tn=1024):
    m, d = h.shape
    assert m % tm == 0 and D_ATT % tn == 0 and D_CONV % tn == 0
    col_starts = [off + c for off, width in ((OFF_Q, D_ATT), (OFF_GA, D_ATT), (OFF_CG, D_CONV))
                  for c in range(0, width, tn)]
    return pl.pallas_call(
        functools.partial(_proj_qgc_kernel, row0=l * d, col_starts=col_starts, q_tiles=D_ATT // tn),
        grid=(len(col_starts), m // tm),
        in_specs=[pl.BlockSpec((tm, d), lambda j, i: (i, 0)), pl.BlockSpec(memory_space=pl.ANY),
                  pl.BlockSpec((1, HEAD_DIM), lambda j, i: (0, 0))],
        out_specs=pl.BlockSpec((tm, tn), lambda j, i: (i, j)),
        out_shape=jax.ShapeDtypeStruct((m, len(col_starts) * tn), BF16),
        scratch_shapes=[pltpu.VMEM((d, tn), F32), pltpu.VMEM((d, tn), BF16), pltpu.SemaphoreType.DMA],
        compiler_params=_params(2),
        name="proj_qgc",
    )(h, w2d, qg)


def _proj_glu_conv_kernel(h_ref, hs_ref, w_hbm, cw_ref, cb_ref, y_ref, us_ref, ut_ref,
                          stage_a, stage_b, wab, wbb, ubuf0, ubuf1, sem_a, sem_b,
                          *, tm, rc, tiles_per_seq, n_prompt_tiles, row0, col_a, col_b):
    j, i = pl.program_id(0), pl.program_id(1)
    d, tn = stage_a.shape
    nsl = tn // LANES
    base = HALO - HIST

    def weight_copies(jj):
        rows = pl.ds(row0, d)
        return (pltpu.make_async_copy(w_hbm.at[rows, pl.ds(pl.multiple_of(col_a + jj * tn, tn), tn)],
                                      stage_a, sem_a),
                pltpu.make_async_copy(w_hbm.at[rows, pl.ds(pl.multiple_of(col_b + jj * tn, tn), tn)],
                                      stage_b, sem_b))

    @pl.when(i == 0)
    def _():
        @pl.when(j == 0)
        def _():
            for copy in weight_copies(j):
                copy.start()

        for copy in weight_copies(j):
            copy.wait()
        wab[...] = stage_a[...].astype(BF16)
        wbb[...] = stage_b[...].astype(BF16)

        @pl.when(j + 1 < pl.num_programs(0))
        def _():
            for copy in weight_copies(j + 1):
                copy.start()

    @pl.when((i == 0) & (j == 0))
    def _():
        ubuf1[...] = jnp.zeros(ubuf1.shape, F32)

    def conv(prv):
        for c in range(nsl):
            bias = jnp.broadcast_to(cb_ref[c], (rc, LANES))
            for r0 in range(0, tm, rc):
                acc = bias
                for k in range(CONV_W):
                    acc = acc + cw_ref[c, pl.ds(k, 1), :] * prv[c, pl.ds(r0 + base + k, rc), :]
                y_ref[pl.ds(r0, rc), pl.ds(c * LANES, LANES)] = acc

    def glu(h, c0, width):
        cols = pl.ds(c0, width)
        a = jnp.dot(h, wab[:, cols], preferred_element_type=F32)
        b = jnp.dot(h, wbb[:, cols], preferred_element_type=F32)
        return a * jax.nn.sigmoid(b)

    mxu_cols = 2 * LANES

    def prompt_step(cur, prv):
        conv(prv)
        h = h_ref[...]
        seq_start = lax.rem(i, tiles_per_seq) == 0
        for c0 in range(0, tn, mxu_cols):
            u = glu(h, c0, mxu_cols)
            for c in range(c0 // LANES, (c0 + mxu_cols) // LANES):
                cur[c, pl.ds(0, HALO), :] = jnp.where(seq_start, 0.0, prv[c, pl.ds(tm, HALO), :])
                cur[c, pl.ds(HALO, tm), :] = u[:, c * LANES - c0:(c + 1) * LANES - c0]

        @pl.when(lax.rem(i, tiles_per_seq) == tiles_per_seq - 1)
        def _():
            for c in range(nsl):
                ut_ref[c] = cur[c, pl.ds(tm, HALO), :]

    def sample_step(prv):
        conv(prv)
        h = hs_ref[...]
        for c0 in range(0, tn, mxu_cols):
            u = glu(h, c0, mxu_cols)
            for c in range(c0 // LANES, (c0 + mxu_cols) // LANES):
                us_ref[c] = u[:, c * LANES - c0:(c + 1) * LANES - c0]

    is_prompt = i < n_prompt_tiles
    pl.when(is_prompt & (lax.rem(i, 2) == 0))(lambda: prompt_step(ubuf0, ubuf1))
    pl.when(is_prompt & (lax.rem(i, 2) == 1))(lambda: prompt_step(ubuf1, ubuf0))
    pl.when(i == n_prompt_tiles)(lambda: sample_step(ubuf1 if n_prompt_tiles % 2 == 0 else ubuf0))


def _slab_weights(conv_w, conv_b):
    cw = jnp.pad(conv_w, ((0, HALO - CONV_W), (0, 0)))
    cw = jnp.transpose(cw.reshape(HALO, N_SLAB, LANES), (1, 0, 2))
    return cw, conv_b.reshape(N_SLAB, 1, LANES)


def _proj_glu_conv(h, w2d, l, conv_w, conv_b, n_batch, seq, ms, tn=512):
    m, d = h.shape
    mp = n_batch * seq
    assert seq % TM == 0 and mp % ms == 0 and m == mp + ms
    npt = mp // TM
    assert npt % 2 == 0
    nsl = tn // LANES
    cw, cb = _slab_weights(conv_w, conv_b)
    return pl.pallas_call(
        functools.partial(_proj_glu_conv_kernel, tm=TM, rc=32, tiles_per_seq=seq // TM, n_prompt_tiles=npt,
                          row0=l * d, col_a=OFF_CA, col_b=OFF_CB),
        grid=(D_CONV // tn, npt + 1),
        in_specs=[
            pl.BlockSpec((TM, d), lambda j, i: (jnp.minimum(i, npt - 1), 0)),
            pl.BlockSpec((ms, d), lambda j, i: (mp // ms, 0), pipeline_mode=pl.Buffered(1)),
            pl.BlockSpec(memory_space=pl.ANY),
            pl.BlockSpec((nsl, HALO, LANES), lambda j, i: (j, 0, 0)),
            pl.BlockSpec((nsl, 1, LANES), lambda j, i: (j, 0, 0)),
        ],
        out_specs=[
            pl.BlockSpec((TM, tn), lambda j, i: (jnp.maximum(i - 1, 0), j)),
            pl.BlockSpec((nsl, ms, LANES), lambda j, i: (j, 0, 0)),
            pl.BlockSpec((None, nsl, HALO, LANES),
                         lambda j, i: (jnp.minimum(i // (seq // TM), n_batch - 1), j, 0, 0)),
        ],
        out_shape=[
            jax.ShapeDtypeStruct((mp, D_CONV), F32),
            jax.ShapeDtypeStruct((N_SLAB, ms, LANES), F32),
            jax.ShapeDtypeStruct((n_batch, N_SLAB, HALO, LANES), F32),
        ],
        scratch_shapes=[pltpu.VMEM((d, tn), F32), pltpu.VMEM((d, tn), F32),
                        pltpu.VMEM((d, tn), BF16), pltpu.VMEM((d, tn), BF16),
                        pltpu.VMEM((nsl, HALO + TM, LANES), F32), pltpu.VMEM((nsl, HALO + TM, LANES), F32),
                        pltpu.SemaphoreType.DMA, pltpu.SemaphoreType.DMA],
        compiler_params=_params(2, VMEM_PHYSICAL - 512 * 1024),
        name="proj_glu_conv",
    )(h, h, w2d, cw, cb)


def _prompt_attn_kernel(sinks_ref, q_ref, kp_ref, kc_ref, vp_ref, vc_ref, ga_ref, o_ref, bias, *, bq):
    qi = lax.broadcasted_iota(jnp.int32, (WINDOW, 2 * WINDOW), 0)
    kj = lax.broadcasted_iota(jnp.int32, (WINDOW, 2 * WINDOW), 1)

    @pl.when((pl.program_id(0) == 0) & (pl.program_id(1) == 0))
    def _():
        dist = WINDOW + qi - kj
        in_band = (dist >= 0) & (dist < WINDOW)
        distf = dist.astype(F32)
        for h in range(N_HEADS):
            bias[h] = jnp.where(in_band, (-SLOPES[h] * LOG2E) * distf, NEG_INF)

    has_prev = kj >= WINDOW * (pl.program_id(1) == 0).astype(jnp.int32)
    nsub = bq // WINDOW
    k_all = jnp.concatenate([kp_ref[...], kc_ref[...]], axis=0).astype(BF16)
    v_all = jnp.concatenate([vp_ref[...], vc_ref[...]], axis=0).astype(BF16)
    for s in range(nsub):
        rows = slice(s * WINDOW, (s + 1) * WINDOW)
        for kv in range(N_KV):
            cols = slice(kv * HEAD_DIM, (kv + 1) * HEAD_DIM)
            kb = k_all[s * WINDOW:(s + 2) * WINDOW, cols]
            vb = v_all[s * WINDOW:(s + 2) * WINDOW, cols]
            for g in range(GQA):
                h = kv * GQA + g
                hc = slice(h * HEAD_DIM, (h + 1) * HEAD_DIM)
                sc = lax.dot_general(q_ref[rows, hc], kb, (((1,), (1,)), ((), ())),
                                     preferred_element_type=F32)
                t = sc * (ATT_SCALE * LOG2E) + bias[h]
                if s == 0:
                    t = jnp.where(has_prev, t, NEG_INF)
                sink = sinks_ref[h] * LOG2E
                m = jnp.maximum(jnp.max(t, axis=-1, keepdims=True), sink)
                p = jnp.exp2(t - m)
                denom = jnp.sum(p, axis=-1, keepdims=True) + jnp.exp2(sink - m)
                o = jnp.dot(p.astype(BF16), vb, preferred_element_type=F32) / denom
                o_ref[rows, hc] = (o * ga_ref[rows, hc].astype(F32)).astype(o_ref.dtype)


def _sample_attn_kernel(slope_ref, sink_ref, qg_ref, kc_ref, vc_ref, new_ref,
                        o_ref, ko_ref, vo_ref, *, bn, s_len):
    q_ref, ga_ref = qg_ref.at[0], qg_ref.at[1]
    kn_ref, vn_ref = new_ref.at[0], new_ref.at[1]
    nk = WINDOW + SUBLANES
    r = GQA * s_len
    rows = bn * N_KV * r
    ri = lax.broadcasted_iota(jnp.int32, (rows, nk), 0)
    kj = lax.broadcasted_iota(jnp.int32, (rows, nk), 1)
    dist = lax.div(lax.rem(ri, r), GQA) + WINDOW - kj
    valid = (dist >= 0) & (dist < WINDOW)
    shift = s_len * N_KV
    keep = WINDOW * N_KV - shift

    def head_rows(win_ref, new_ref, b, kv):
        win = win_ref.at[b][pl.ds(kv, WINDOW, stride=N_KV), :]
        new = new_ref.at[b][pl.ds(kv, SUBLANES, stride=N_KV), :]
        return jnp.concatenate([win, new], axis=0).astype(BF16)

    chains = [(b, kv) for b in range(bn) for kv in range(N_KV)]
    sc = jnp.concatenate(
        [lax.dot_general(q_ref[pl.ds(c * r, r), :], head_rows(kc_ref, kn_ref, b, kv),
                         (((1,), (1,)), ((), ())), preferred_element_type=F32)
         for c, (b, kv) in enumerate(chains)], axis=0)
    sc = sc * ATT_SCALE - slope_ref[...] * dist.astype(F32)
    sc = jnp.where(valid, sc, NEG_INF)
    sink = sink_ref[...]
    m = jnp.maximum(jnp.max(sc, axis=-1, keepdims=True), sink)
    p = jnp.exp(sc - m)
    denom = jnp.sum(p, axis=-1, keepdims=True) + jnp.exp(sink - m)
    p = p.astype(BF16)
    o = jnp.concatenate(
        [jnp.dot(p[c * r:(c + 1) * r], head_rows(vc_ref, vn_ref, b, kv), preferred_element_type=F32)
         for c, (b, kv) in enumerate(chains)], axis=0)
    o_ref[...] = (o / denom * ga_ref[...].astype(F32)).astype(o_ref.dtype)

    for b in range(bn):
        for win_ref, new_ref, out_ref in ((kc_ref, kn_ref, ko_ref), (vc_ref, vn_ref, vo_ref)):
            out_ref[b, pl.ds(0, keep), :] = win_ref[b, pl.ds(shift, keep), :]
            out_ref[b, pl.ds(keep, shift), :] = new_ref[b, pl.ds(0, shift), :]


def _to_head_rows(a, n, s_len):
    planes = a.shape[1] // D_ATT
    a = a.reshape(s_len, n, planes, N_KV, GQA, HEAD_DIM)
    return jnp.transpose(a, (2, 1, 3, 0, 4, 5)).reshape(planes, n * N_KV * s_len * GQA, HEAD_DIM)


def _from_head_rows(a, n, s_len):
    a = a.reshape(n, N_KV, s_len, GQA, HEAD_DIM)
    return jnp.transpose(a, (2, 0, 1, 3, 4)).reshape(s_len * n, D_ATT)


def _sample_dwconv_kernel(st_ref, u_ref, cw_ref, cb_ref, y_ref, so_ref, *, s_len, n):
    half = n // 2
    for s in range(u_ref.shape[0]):
        lanes = pl.ds(s * LANES, LANES)
        for r0 in range(0, n, half):
            rows = pl.ds(r0, half)
            ext = lambda j: st_ref[j, rows, lanes] if j < HIST else u_ref[s, pl.ds((j - HIST) * n + r0, half), :]
            acc = [jnp.broadcast_to(cb_ref[:, lanes], (half, LANES)) for _ in range(s_len)]
            for j in range(HIST + s_len):
                x = ext(j)
                for t in range(max(0, j - HIST), min(s_len - 1, j) + 1):
                    acc[t] = acc[t] + cw_ref[pl.ds(j - t, 1), lanes] * x
                if j >= s_len:
                    so_ref[j - s_len, rows, lanes] = x
            for t in range(s_len):
                y_ref[pl.ds(t * n + r0, half), lanes] = acc[t]


def _attention_kernel(sinks_ref, q_ref, kp_ref, kc_ref, vp_ref, vc_ref, ga_ref,
                      slope_ref, sink_ref, qg_ref, wk_ref, wv_ref, new_ref,
                      st_ref, u_ref, cw_ref, cb_ref,
                      o_ref, os_ref, ko_ref, vo_ref, ys_ref, so_ref, bias, *, bq, bn, s_len, n):
    _prompt_attn_kernel(sinks_ref, q_ref, kp_ref, kc_ref, vp_ref, vc_ref, ga_ref, o_ref, bias, bq=bq)
    _sample_attn_kernel(slope_ref, sink_ref, qg_ref, wk_ref, wv_ref, new_ref, os_ref, ko_ref, vo_ref,
                        bn=bn, s_len=s_len)
    _sample_dwconv_kernel(st_ref, u_ref, cw_ref, cb_ref, ys_ref, so_ref, s_len=s_len, n=n)


def _attention(z, k, v, sinks, cache_k, cache_v, state_tm, u_slab, conv_w, conv_b, l, n_batch, seq, n, s_len,
               bq=512):
    mp, ms = n_batch * seq, s_len * n
    nb = seq // bq
    steps = n_batch * nb
    depth, _, _, c = state_tm.shape
    assert n % steps == 0 and c % (steps * LANES) == 0 and s_len <= SUBLANES and n % SUBLANES == 0
    bn, tc = n // steps, c // steps
    ratio = bq // WINDOW
    step = lambda b, i: b * nb + i
    cur = lambda b, i, s: (step(b, i), 0)
    prev = lambda b, i, s: (jnp.maximum(step(b, i) * ratio - 1, 0), 0)

    r = GQA * s_len
    rows = bn * N_KV * r
    per_row = lambda a: jnp.broadcast_to(a.astype(F32).reshape(1, N_KV, 1, GQA),
                                         (bn, N_KV, s_len, GQA)).reshape(rows, 1)
    rows_new = SUBLANES * N_KV
    new = jnp.stack([k[mp:], v[mp:]]).reshape(2, s_len, n, N_KV, HEAD_DIM)
    new = jnp.transpose(new, (0, 2, 1, 3, 4)).reshape(2, n, s_len * N_KV, HEAD_DIM)
    new = jnp.pad(new, ((0, 0), (0, 0), (0, rows_new - s_len * N_KV), (0, 0)))
    win_rows = lambda a: a.reshape(depth * n, WINDOW * N_KV, HEAD_DIM)
    small = pl.BlockSpec((rows, 1), lambda b, i, s: (0, 0))
    heads = pl.BlockSpec((rows, HEAD_DIM), cur)
    win_in = pl.BlockSpec((bn, WINDOW * N_KV, HEAD_DIM), lambda b, i, s: (l * steps + step(b, i), 0, 0))
    win_out = pl.BlockSpec((bn, WINDOW * N_KV, HEAD_DIM), lambda b, i, s: (step(b, i), 0, 0))
    win_shape = jax.ShapeDtypeStruct((n, WINDOW * N_KV, HEAD_DIM), F32)
    col = lambda b, i, s: (0, step(b, i))

    att, att_s, k_win, v_win, y_s, new_state = pl.pallas_call(
        functools.partial(_attention_kernel, bq=bq, bn=bn, s_len=s_len, n=n),
        grid_spec=pltpu.PrefetchScalarGridSpec(
            num_scalar_prefetch=1,
            grid=(n_batch, nb),
            in_specs=[
                pl.BlockSpec((bq, D_ATT), cur),
                pl.BlockSpec((WINDOW, KV_W), prev),
                pl.BlockSpec((bq, KV_W), cur),
                pl.BlockSpec((WINDOW, KV_W), prev),
                pl.BlockSpec((bq, KV_W), cur),
                pl.BlockSpec((bq, D_ATT), lambda b, i, s: (step(b, i), 1)),
                small, small,
                pl.BlockSpec((2, rows, HEAD_DIM), lambda b, i, s: (0, step(b, i), 0)),
                win_in, win_in,
                pl.BlockSpec((2, bn, rows_new, HEAD_DIM), lambda b, i, s: (0, step(b, i), 0, 0)),
                pl.BlockSpec((None, HIST, n, tc), lambda b, i, s: (l, 0, 0, step(b, i))),
                pl.BlockSpec((tc // LANES, ms, LANES), lambda b, i, s: (step(b, i), 0, 0)),
                pl.BlockSpec((CONV_W, tc), col),
                pl.BlockSpec((1, tc), col),
            ],
            out_specs=[
                pl.BlockSpec((bq, D_ATT), cur),
                heads, win_out, win_out,
                pl.BlockSpec((ms, tc), col),
                pl.BlockSpec((HIST, n, tc), lambda b, i, s: (0, 0, step(b, i))),
            ],
            scratch_shapes=[pltpu.VMEM((N_HEADS, WINDOW, 2 * WINDOW), F32)],
        ),
        out_shape=[jax.ShapeDtypeStruct((mp, D_ATT), BF16),
                   jax.ShapeDtypeStruct((n * N_KV * r, HEAD_DIM), BF16), win_shape, win_shape,
                   jax.ShapeDtypeStruct((ms, c), F32), jax.ShapeDtypeStruct((HIST, n, c), F32)],
        compiler_params=_params(2),
        name="attention",
    )(sinks, z, k, k, v, v, z,
      per_row(jnp.asarray(SLOPES, F32)), per_row(sinks), _to_head_rows(z[mp:, :2 * D_ATT], n, s_len),
      win_rows(cache_k), win_rows(cache_v), new,
      state_tm, u_slab, conv_w, conv_b.reshape(1, c))
    win5 = lambda a: a.reshape(n, WINDOW, N_KV, HEAD_DIM)
    return att, _from_head_rows(att_s, n, s_len), win5(k_win), win5(v_win), y_s, new_state


def _ln_pw_kernel(yp_ref, ys_ref, lng_ref, lnb_ref, wpw_ref, cg_ref, o_ref, wpb, *, n_prompt_tiles):
    @pl.when(pl.program_id(0) == 0)
    def _():
        wpb[...] = wpw_ref[...].astype(BF16)

    y = jnp.where(pl.program_id(0) < n_prompt_tiles, yp_ref[...], ys_ref[...])
    mu = jnp.mean(y, axis=-1, keepdims=True)
    yc = y - mu
    var = jnp.mean(yc * yc, axis=-1, keepdims=True)
    yn = yc * lax.rsqrt(var + LN_EPS) * lng_ref[...] + lnb_ref[...]
    a = _silu(yn).astype(BF16)
    o = jnp.dot(a, wpb[...], preferred_element_type=F32)
    o_ref[...] = (o * cg_ref[...].astype(F32)).astype(o_ref.dtype)


def _ln_pw(y_p, y_s, z, cg_block, ln_g, ln_b, w_pw2, l):
    mp, c = y_p.shape
    tm = y_s.shape[0]
    m = mp + tm
    depth = w_pw2.shape[0]
    assert mp % tm == 0 and z.shape[0] == m
    npt = mp // tm
    const = lambda i: (0, 0)
    return pl.pallas_call(
        functools.partial(_ln_pw_kernel, n_prompt_tiles=npt),
        grid=(npt + 1,),
        in_specs=[
            pl.BlockSpec((tm, c), lambda i: (jnp.minimum(i, npt - 1), 0)),
            pl.BlockSpec((tm, c), const),
            pl.BlockSpec((1, c), const),
            pl.BlockSpec((1, c), const),
            pl.BlockSpec((c, c), lambda i: (l, 0), pipeline_mode=pl.Buffered(1)),
            pl.BlockSpec((tm, c), lambda i: (i, cg_block)),
        ],
        out_specs=pl.BlockSpec((tm, c), lambda i: (i, 0)),
        out_shape=jax.ShapeDtypeStruct((m, c), BF16),
        scratch_shapes=[pltpu.VMEM((c, c), BF16)],
        compiler_params=_params(1),
        name="ln_pw",
    )(y_p, y_s, ln_g.reshape(1, -1), ln_b.reshape(1, -1), w_pw2.reshape(depth * c, c), z)


def _outproj_kernel(a_ref, c_ref, as_ref, cs_ref, w_ref, xp_ref, gp_ref, xs_ref, gs_ref, yp_ref, ys_ref, wb,
                    *, n_prompt_tiles, tiles_per_seq, n_batch, s_len):
    i = pl.program_id(1)

    @pl.when(i == 0)
    def _():
        wb[...] = w_ref[...].astype(BF16)

    def project(att_ref, conv_ref):
        o = jnp.dot(att_ref[...], wb[pl.ds(0, D_ATT), :], preferred_element_type=F32)
        return o + jnp.dot(conv_ref[...], wb[pl.ds(D_ATT, D_CONV), :], preferred_element_type=F32)

    @pl.when(i < n_prompt_tiles)
    def _():
        gate = _select_row(gp_ref, i // tiles_per_seq, n_batch)
        yp_ref[...] = xp_ref[...] + gate * project(a_ref, c_ref)

    @pl.when(i == n_prompt_tiles)
    def _():
        gate = jnp.concatenate([gs_ref[...]] * s_len, axis=0)
        ys = xs_ref[...] + gate * project(as_ref, cs_ref)
        if len(ys_ref.shape) == 2:
            ys_ref[...] = ys
        else:
            ns = ys_ref.shape[0]
            for t in range(s_len):
                ys_ref[:, t, :] = ys[t * ns:(t + 1) * ns]


def _outproj(att_p, att_s, conv, w_out, l, xp, xs, mod, n_batch, seq, s_len, batch_major_out, tn=512):
    mp, d = xp.shape
    ms = xs.shape[0]
    ns = ms // s_len
    depth = w_out.shape[0]
    assert mp % TM == 0 and seq % TM == 0 and mp % ms == 0 and ns % MOD_ROWS == 0
    npt = mp // TM
    gate_col0 = 2 * d // tn
    ptile = lambda j, i: (jnp.minimum(i, npt - 1), j)
    ptile_rows = lambda j, i: (jnp.minimum(i, npt - 1), 0)
    stile = lambda j, i: (mp // ms, 0)
    if batch_major_out:
        ys_spec = pl.BlockSpec((ns, s_len, tn), lambda j, i: (0, 0, j))
        ys_shape = jax.ShapeDtypeStruct((ns, s_len, d), F32)
    else:
        ys_spec = pl.BlockSpec((ms, tn), lambda j, i: (0, j))
        ys_shape = jax.ShapeDtypeStruct((ms, d), F32)
    return pl.pallas_call(
        functools.partial(_outproj_kernel, n_prompt_tiles=npt, tiles_per_seq=seq // TM, n_batch=n_batch,
                          s_len=s_len),
        grid=(d // tn, npt + 1),
        in_specs=[
            pl.BlockSpec((TM, D_ATT), ptile_rows),
            pl.BlockSpec((TM, D_CONV), ptile_rows),
            pl.BlockSpec((ms, D_ATT), lambda j, i: (0, 0)),
            pl.BlockSpec((ms, D_CONV), stile),
            pl.BlockSpec((D_ATT + D_CONV, tn), lambda j, i: (l, j)),
            pl.BlockSpec((TM, tn), ptile),
            pl.BlockSpec((MOD_ROWS, tn), lambda j, i: (ns // MOD_ROWS, gate_col0 + j)),
            pl.BlockSpec((ms, tn), lambda j, i: (0, j)),
            pl.BlockSpec((ns, tn), lambda j, i: (0, gate_col0 + j)),
        ],
        out_specs=[pl.BlockSpec((TM, tn), ptile), ys_spec],
        out_shape=[jax.ShapeDtypeStruct((mp, d), F32), ys_shape],
        scratch_shapes=[pltpu.VMEM((D_ATT + D_CONV, tn), BF16)],
        compiler_params=_params(2),
        name="outproj",
    )(att_p, conv, att_s, conv, w_out.reshape(depth * (D_ATT + D_CONV), d), xp, mod, xs, mod)


def kernel(x_prompt, x_sample, c_prompt, c_sample, cache_k_win, cache_v_win, state_conv, w_ada, b_ada,
           norm_g, w_in, q_norm_g, k_norm_g, sinks, conv_w, conv_b, ln_g, ln_b, w_pw2, w_out):
    depth, d, n_in = w_in.shape
    nb, seq, _ = x_prompt.shape
    ns, s_len, _ = x_sample.shape
    mp, ms = nb * seq, ns * s_len
    m_all = mp + ms
    tm_in = m_all // 8
    assert tm_in * 8 == m_all and tm_in % 16 == 0

    xp = x_prompt.reshape(mp, d)
    xs = jnp.transpose(x_sample, (1, 0, 2)).reshape(ms, d)
    pad = (-(ns + nb)) % 16
    c_all = jnp.concatenate([c_sample, c_prompt, jnp.zeros((pad, d), F32)], axis=0)
    w_in2d = w_in.reshape(depth * d, n_in)
    state_tm = jnp.transpose(state_conv, (0, 2, 1, 3))

    kp_l, vp_l, cp_l, ks_l, vs_l, cs_l = [], [], [], [], [], []
    for l in range(depth):
        mod = _modulation(c_all, w_ada, b_ada, l)

        qg = q_norm_g[l].reshape(1, HEAD_DIM)
        kg = k_norm_g[l].reshape(1, HEAD_DIM)
        hkv = _prenorm_kv(xp, norm_g[l], mod, ns, nb, None, w_in2d, l, kg, ms, m_all, 0, zero_tiles=1)
        h, k, v = _prenorm_kv(xs, norm_g[l], mod, 0, None, ns, w_in2d, l, kg, ms, m_all, mp, prev=hkv)
        y, u_s, u_tail = _proj_glu_conv(h, w_in2d, l, conv_w[l], conv_b[l], nb, seq, ms)
        z = _proj_qgc(h, w_in2d, l, qg, tm_in)

        att, att_s, k_win, v_win, y_s, conv_state = _attention(
            z, k, v, sinks[l], cache_k_win, cache_v_win, state_tm, u_s, conv_w[l], conv_b[l], l, nb, seq, ns, s_len)
        conv = _ln_pw(y, y_s, z, 2 * D_ATT // D_CONV, ln_g[l], ln_b[l], w_pw2, l)

        tail = lambda a, rows: jnp.stack(
            [lax.slice_in_dim(a, (b + 1) * seq - rows, (b + 1) * seq, axis=0) for b in range(nb)])
        kp_l.append(tail(k, WINDOW).reshape(nb, WINDOW, N_KV, HEAD_DIM))
        vp_l.append(tail(v, WINDOW).reshape(nb, WINDOW, N_KV, HEAD_DIM))
        cp_l.append(jnp.transpose(u_tail[:, :, HALO - HIST:, :], (0, 2, 1, 3)).reshape(nb, HIST, D_CONV))
        ks_l.append(k_win)
        vs_l.append(v_win)
        cs_l.append(jnp.transpose(conv_state, (1, 0, 2)))

        xp, xs = _outproj(att, att_s, conv, w_out, l, xp, xs, mod, nb, seq, s_len, l == depth - 1)

    return (xp.reshape(nb, seq, d), xs,
            jnp.stack(kp_l), jnp.stack(vp_l), jnp.stack(cp_l),
            jnp.stack(ks_l), jnp.stack(vs_l), jnp.stack(cs_l))
```
